```python
import math
import jax, jax.numpy as jnp
from jax import lax
import numpy as np

D_MODEL = 2048
BATCH = 4
SEQ = 2048
DEPTH = 4

N_EVEN = (DEPTH + 1) // 2
N_ODD = DEPTH // 2
RMS_EPS = 1e-6
ROPE_THETA = 500000.0
NEG_INF = -1e30
MLA_HEADS = 8
MLA_NOPE = 128
MLA_ROPE = 64
MLA_V = 128
Q_LORA = 512
KV_LORA = 256
MLA_QBLOCK = 128
HY_D = 1024
HY_ORDER = 2
HY_SHORT = 3
HY_EMB = 33
HY_FFN = 64
HY_DECAY_PCT_SHORT = 0.3
HY_DECAY_PCT_LONG = 1.5
HY_TARGET = 1e-2
GQA_HEADS = 16
GQA_KV_HEADS = 4
GQA_HEAD_DIM = 128
GQA_ROT = GQA_HEAD_DIM // 4
WINDOW = 128
BAND_BLOCK = 128
D_FF = 4 * D_MODEL
E_IN = Q_LORA + KV_LORA + MLA_ROPE + 3 * HY_D
MIX_EVEN = MLA_HEADS * MLA_V + HY_D
QKV_ODD = (GQA_HEADS + 2 * GQA_KV_HEADS) * GQA_HEAD_DIM

kernel_name = "hybrid_mla_hyena_swa_encoder"


def rms_norm(x, g):
    x32 = x.astype(jnp.float32)
    y = x32 * lax.rsqrt(jnp.mean(x32 * x32, axis=-1, keepdims=True) + RMS_EPS)
    return (y * g.astype(jnp.float32)).astype(x.dtype)


def rotary(x, pos):
    d = x.shape[-1]
    inv = ROPE_THETA ** (-jnp.arange(0, d, 2, dtype=jnp.float32) / d)
    ang = pos[:, None] * inv[None, :]
    cos = jnp.cos(ang)[None, :, None, :]
    sin = jnp.sin(ang)[None, :, None, :]
    x1, x2 = jnp.split(x.astype(jnp.float32), 2, axis=-1)
    return jnp.concatenate([x1 * cos - x2 * sin, x2 * cos + x1 * sin], axis=-1).astype(x.dtype)


def ada_modulation(c, w, b):
    m = jax.nn.silu(c) @ w + b
    shift, scale, gate = jnp.split(m[:, None, :], 3, axis=-1)
    return shift, scale, gate


def mla_attention(q_lat, kv_lat, k_rope, q_norm_g, kv_norm_g, w_uq, w_ukv, pos):
    B, S, _ = q_lat.shape
    q = (rms_norm(q_lat, q_norm_g) @ w_uq).reshape(B, S, MLA_HEADS, MLA_NOPE + MLA_ROPE)
    q_nope, q_rope = q[..., :MLA_NOPE], rotary(q[..., MLA_NOPE:], pos)
    kv = (rms_norm(kv_lat, kv_norm_g) @ w_ukv).reshape(B, S, MLA_HEADS, MLA_NOPE + MLA_V)
    k_nope, v = kv[..., :MLA_NOPE], kv[..., MLA_NOPE:]
    k_rope = rotary(k_rope[:, :, None, :], pos)[:, :, 0, :]
    scale = (MLA_NOPE + MLA_ROPE) ** -0.5
    nb = S // MLA_QBLOCK
    qn_b = q_nope.reshape(B, nb, MLA_QBLOCK, MLA_HEADS, MLA_NOPE).transpose(1, 0, 2, 3, 4)
    qr_b = q_rope.reshape(B, nb, MLA_QBLOCK, MLA_HEADS, MLA_ROPE).transpose(1, 0, 2, 3, 4)

    def block(args):
        qn, qr = args
        s = jnp.einsum('bqhd,bkhd->bhqk', qn, k_nope) + jnp.einsum('bqhr,bkr->bhqk', qr, k_rope)
        p = jax.nn.softmax(s.astype(jnp.float32) * scale, axis=-1).astype(v.dtype)
        return jnp.einsum('bhqk,bkhd->bqhd', p, v)

    o = lax.map(block, (qn_b, qr_b))
    return o.transpose(1, 0, 2, 3, 4).reshape(B, S, MLA_HEADS * MLA_V)


def short_conv(u, w, b):
    S = u.shape[1]
    pad = HY_SHORT // 2
    up = jnp.pad(u, ((0, 0), (pad, pad), (0, 0)))
    y = b
    for j in range(HY_SHORT):
        y = y + up[:, j:j + S] * w[j]
    return y


def hyena_filters(L, w1, b1, w2, b2, w3, b3, freq, w4):
    f32 = jnp.float32
    t = jnp.linspace(0.0, 1.0, L, dtype=f32)[:, None]
    bands = (HY_EMB - 1) // 2
    wpos = 2.0 * math.pi * jnp.arange(L, dtype=f32) / L
    fb = jnp.linspace(1e-4, bands - 1, bands, dtype=f32)
    fw = wpos[:, None] * fb[None, :]
    z = jnp.concatenate([t, jnp.cos(fw), -jnp.sin(fw)], axis=-1)
    fr = freq.astype(f32)
    h = jnp.sin(fr * (z @ w1.astype(f32) + b1.astype(f32)))
    h = jnp.sin(fr * (h @ w2.astype(f32) + b2.astype(f32)))
    h = jnp.sin(fr * (h @ w3.astype(f32) + b3.astype(f32)))
    h = (h @ w4.astype(f32)).reshape(L, HY_ORDER, 2, HY_D)
    max_decay = math.log(HY_TARGET) / HY_DECAY_PCT_SHORT
    min_decay = math.log(HY_TARGET) / HY_DECAY_PCT_LONG
    deltas = jnp.linspace(min_decay, max_decay, HY_D, dtype=f32)
    decay = jnp.exp(-t * jnp.abs(deltas)[None, :])
    h = h * decay[:, None, None, :]
    fwd, bwd = h[:, :, 0], h[:, :, 1]
    k = jnp.concatenate([fwd, jnp.zeros((1, HY_ORDER, HY_D), f32), bwd[:0:-1]], axis=0)
    return k / jnp.sum(jnp.abs(k), axis=0, keepdims=True)


def fft_long_conv(u, k_hat, skip):
    L = u.shape[1]
    u32 = u.astype(jnp.float32)
    u_hat = jnp.fft.rfft(u32, n=2 * L, axis=1)
    y = jnp.fft.irfft(u_hat * k_hat[None], n=2 * L, axis=1)[:, :L]
    return (y + u32 * skip.astype(jnp.float32)).astype(u.dtype)


def hyena_mixer(hy_in, conv_w, conv_b, f_w1, f_b1, f_w2, f_b2, f_w3, f_b3, f_freq, f_w4, hy_skip):
    S = hy_in.shape[1]
    u = short_conv(hy_in, conv_w, conv_b)
    v, x1, x2 = jnp.split(u, 3, axis=-1)
    k = hyena_filters(S, f_w1, f_b1, f_w2, f_b2, f_w3, f_b3, f_freq, f_w4)
    k_hat = jnp.fft.rfft(k, axis=0)
    z = x1 * fft_long_conv(v, k_hat[:, 0], hy_skip[0])
    return x2 * fft_long_conv(z, k_hat[:, 1], hy_skip[1])


def window_gqa(h, w_qkv, sinks, w_o, pos):
    B, S, _ = h.shape
    G = GQA_HEADS // GQA_KV_HEADS
    BB = BAND_BLOCK
    qkv = h @ w_qkv
    q, k, v = jnp.split(qkv, [GQA_HEADS * GQA_HEAD_DIM, (GQA_HEADS + GQA_KV_HEADS) * GQA_HEAD_DIM], axis=-1)
    q = q.reshape(B, S, GQA_HEADS, GQA_HEAD_DIM)
    k = k.reshape(B, S, GQA_KV_HEADS, GQA_HEAD_DIM)
    v = v.reshape(B, S, GQA_KV_HEADS, GQA_HEAD_DIM)
    q = jnp.concatenate([rotary(q[..., :GQA_ROT], pos), q[..., GQA_ROT:]], axis=-1)
    k = jnp.concatenate([rotary(k[..., :GQA_ROT], pos), k[..., GQA_ROT:]], axis=-1)
    nb = S // BB
    q = q.reshape(B, nb, BB, GQA_KV_HEADS, G, GQA_HEAD_DIM)

    def band(t):
        tp = jnp.pad(t, ((0, 0), (BB, BB), (0, 0), (0, 0))).reshape(B, nb + 2, BB, GQA_KV_HEADS, GQA_HEAD_DIM)
        return jnp.concatenate([tp[:, :-2], tp[:, 1:-1], tp[:, 2:]], axis=2)

    kb, vb = band(k), band(v)
    s = jnp.einsum('bnqhgd,bnkhd->bnhgqk', q, kb).astype(jnp.float32) * (GQA_HEAD_DIM ** -0.5)
    qpos = jnp.arange(nb)[:, None] * BB + jnp.arange(BB)[None, :]
    kpos = (jnp.arange(nb)[:, None] - 1) * BB + jnp.arange(3 * BB)[None, :]
    rel = kpos[:, None, :] - qpos[:, :, None]
    valid = (jnp.abs(rel) <= WINDOW) & (kpos[:, None, :] >= 0) & (kpos[:, None, :] < S)
    s = jnp.where(valid[None, :, None, None], s, NEG_INF)
    sink = jnp.broadcast_to(sinks.astype(jnp.float32).reshape(GQA_KV_HEADS, G)[None, None, :, :, None, None],
                            s.shape[:-1] + (1,))
    p = jax.nn.softmax(jnp.concatenate([s, sink], axis=-1), axis=-1)[..., :-1].astype(vb.dtype)
    o = jnp.einsum('bnhgqk,bnkhd->bnqhgd', p, vb).reshape(B, S, GQA_HEADS * GQA_HEAD_DIM)
    return o @ w_o


def setup_inputs(seed: int = 0) -> dict:
    key = jax.random.key(seed)
    ks = iter(jax.random.split(key, 40))
    f32 = jnp.float32

    def nrm(shape, scale):
        return jax.random.normal(next(ks), shape, f32) * scale

    def gain(shape):
        return 1.0 + nrm(shape, 0.01)

    D = D_MODEL
    return {
        "x": nrm((BATCH, SEQ, D), 1.0),
        "c": nrm((BATCH, D), 1.0),
        "ada_mix_w": nrm((DEPTH, D, 3 * D), 0.5 * D ** -0.5),
        "ada_mix_b": nrm((DEPTH, 3 * D), 0.01),
        "norm_mix_g": gain((DEPTH, D)),
        "ada_mlp_w": nrm((DEPTH, D, 3 * D), 0.5 * D ** -0.5),
        "ada_mlp_b": nrm((DEPTH, 3 * D), 0.01),
        "norm_mlp_g": gain((DEPTH, D)),
        "w_mlp_in": nrm((DEPTH, D, D_FF), D ** -0.5),
        "w_mlp_out": nrm((DEPTH, D_FF, D), D_FF ** -0.5),
        "e_w_in": nrm((N_EVEN, D, E_IN), D ** -0.5),
        "e_q_norm_g": gain((N_EVEN, Q_LORA)),
        "e_kv_norm_g": gain((N_EVEN, KV_LORA)),
        "e_w_uq": nrm((N_EVEN, Q_LORA, MLA_HEADS * (MLA_NOPE + MLA_ROPE)), Q_LORA ** -0.5),
        "e_w_ukv": nrm((N_EVEN, KV_LORA, MLA_HEADS * (MLA_NOPE + MLA_V)), KV_LORA ** -0.5),
        "e_conv_w": nrm((N_EVEN, HY_SHORT, 3 * HY_D), HY_SHORT ** -0.5),
        "e_conv_b": nrm((N_EVEN, 3 * HY_D), 0.01),
        "e_f_w1": nrm((N_EVEN, HY_EMB, HY_FFN), HY_EMB ** -0.5),
        "e_f_b1": nrm((N_EVEN, HY_FFN), 0.01),
        "e_f_w2": nrm((N_EVEN, HY_FFN, HY_FFN), HY_FFN ** -0.5),
        "e_f_b2": nrm((N_EVEN, HY_FFN), 0.01),
        "e_f_w3": nrm((N_EVEN, HY_FFN, HY_FFN), HY_FFN ** -0.5),
        "e_f_b3": nrm((N_EVEN, HY_FFN), 0.01),
        "e_f_freq": gain((N_EVEN, HY_FFN)),
        "e_f_w4": nrm((N_EVEN, HY_FFN, HY_ORDER * 2 * HY_D), HY_FFN ** -0.5),
        "e_hy_skip": nrm((N_EVEN, HY_ORDER, HY_D), 0.1),
        "e_w_out": nrm((N_EVEN, MIX_EVEN, D), MIX_EVEN ** -0.5),
        "o_w_qkv": nrm((N_ODD, D, QKV_ODD), D ** -0.5),
        "o_sinks": nrm((N_ODD, GQA_HEADS), 1.0),
        "o_w_o": nrm((N_ODD, GQA_HEADS * GQA_HEAD_DIM, D), (GQA_HEADS * GQA_HEAD_DIM) ** -0.5),
        "final_norm_g": gain((D,)),
    }


def reference(x, c, ada_mix_w, ada_mix_b, norm_mix_g, ada_mlp_w, ada_mlp_b, norm_mlp_g, w_mlp_in, w_mlp_out,
              e_w_in, e_q_norm_g, e_kv_norm_g, e_w_uq, e_w_ukv, e_conv_w, e_conv_b,
              e_f_w1, e_f_b1, e_f_w2, e_f_b2, e_f_w3, e_f_b3, e_f_freq, e_f_w4, e_hy_skip, e_w_out,
              o_w_qkv, o_sinks, o_w_o, final_norm_g):
    S = x.shape[1]
    pos = jnp.arange(S, dtype=jnp.float32)
    split_pts = [Q_LORA, Q_LORA + KV_LORA, Q_LORA + KV_LORA + MLA_ROPE]
    for l in range(DEPTH):
        shift, scale, gate = ada_modulation(c, ada_mix_w[l], ada_mix_b[l])
        h = rms_norm(x, norm_mix_g[l]) * (1.0 + scale) + shift
        if l % 2 == 0:
            i = l // 2
            z = h @ e_w_in[i]
            q_lat, kv_lat, k_rope, hy_in = jnp.split(z, split_pts, axis=-1)
            a = mla_attention(q_lat, kv_lat, k_rope, e_q_norm_g[i], e_kv_norm_g[i], e_w_uq[i], e_w_ukv[i], pos)
            b = hyena_mixer(hy_in, e_conv_w[i], e_conv_b[i], e_f_w1[i], e_f_b1[i], e_f_w2[i], e_f_b2[i],
                            e_f_w3[i], e_f_b3[i], e_f_freq[i], e_f_w4[i], e_hy_skip[i])
            mix = jnp.concatenate([a, b], axis=-1) @ e_w_out[i]
        else:
            i = l // 2
            mix = window_gqa(h, o_w_qkv[i], o_sinks[i], o_w_o[i], pos)
        x = x + gate * mix
        shift, scale, gate = ada_modulation(c, ada_mlp_w[l], ada_mlp_b[l])
        h = rms_norm(x, norm_mlp_g[l]) * (1.0 + scale) + shift
        x = x + gate * (jnp.square(jax.nn.relu(h @ w_mlp_in[l])) @ w_mlp_out[l])
    return rms_norm(x, final_norm_g)
```

```python
import functools
import math

import jax
import jax.numpy as jnp
from jax import lax
from jax.experimental import pallas as pl
from jax.experimental.pallas import tpu as pltpu

F32 = jnp.float32
BF16 = jnp.bfloat16
HIGHEST = lax.Precision.HIGHEST

D_MODEL = 2048
BATCH = 4
SEQ = 2048
DEPTH = 4
RMS_EPS = 1e-6
ROPE_THETA = 500000.0
NEG_INF = -1e30
MLA_HEADS = 8
MLA_NOPE = 128
MLA_ROPE = 64
MLA_V = 128
Q_LORA = 512
KV_LORA = 256
HY_D = 1024
HY_ORDER = 2
HY_SHORT = 3
HY_EMB = 33
HY_FFN = 64
HY_DECAY_PCT_SHORT = 0.3
HY_DECAY_PCT_LONG = 1.5
HY_TARGET = 1e-2
GQA_HEADS = 16
GQA_KV_HEADS = 4
GQA_HEAD_DIM = 128
GQA_ROT = GQA_HEAD_DIM // 4
GQA_GROUP = GQA_HEADS // GQA_KV_HEADS
WINDOW = 128
BAND = 128
D_FF = 4 * D_MODEL
LAT_W = Q_LORA + KV_LORA + 128
MLA_QK = 256
DFT_N = 2 * SEQ
MLP_ACC_COLS = 512

LANE = 128
V7X_VMEM_BYTES = 64 * 1024 * 1024
V7X_VMEM_BUDGET = 56 * 1024 * 1024


def _cparams(semantics, est_bytes):
    limit = int(min(V7X_VMEM_BUDGET, max(32 * 1024 * 1024, est_bytes * 3 // 2)))
    return pltpu.CompilerParams(dimension_semantics=semantics, vmem_limit_bytes=limit)


def _nbytes(shape, dtype):
    return math.prod(shape) * jnp.dtype(dtype).itemsize


def _rms(x, g):
    ms = jnp.mean(x * x, axis=-1, keepdims=True)
    return x * lax.rsqrt(ms + RMS_EPS) * g


def _ada_kernel(c_ref, w_ref, b_ref, o_ref):
    cv = c_ref[...]
    s = cv * (1.0 / (1.0 + jnp.exp(-cv)))
    o_ref[0] = jnp.dot(s, w_ref[0], preferred_element_type=F32, precision=HIGHEST) + b_ref[0]


def _ada(c8, w, b, tn=512):
    n_l, d, n = w.shape
    est = 2 * _nbytes((d, tn), F32) + 4 * _nbytes((8, tn), F32) + 2 * _nbytes((8, d), F32)
    return pl.pallas_call(
        _ada_kernel,
        name="ada_modulation",
        grid=(n_l, n // tn),
        in_specs=[
            pl.BlockSpec((8, d), lambda l, j: (0, 0)),
            pl.BlockSpec((1, d, tn), lambda l, j: (l, 0, j)),
            pl.BlockSpec((1, 1, tn), lambda l, j: (l, 0, j)),
        ],
        out_specs=pl.BlockSpec((1, 8, tn), lambda l, j: (l, 0, j)),
        out_shape=jax.ShapeDtypeStruct((n_l, 8, n), F32),
        compiler_params=_cparams(("parallel", "parallel"), est),
    )(c8, w, b.reshape(n_l, 1, n))


def _normmod_to_scratch(x_ref, shift_ref, scale_ref, g_ref, h_ref, rows):
    g = g_ref[...]
    sh = shift_ref[0]
    sc = 1.0 + scale_ref[0]
    tm = h_ref.shape[0]

    def body(r, carry):
        sl = pl.ds(pl.multiple_of(r * rows, rows), rows)
        h_ref[sl, :] = (_rms(x_ref[0, sl, :], g) * sc + sh).astype(BF16)
        return carry

    lax.fori_loop(0, tm // rows, body, 0)


def _normmod_matmul_kernel(x_ref, shift_ref, scale_ref, g_ref, w_ref, o_ref, h_ref):
    @pl.when(pl.program_id(2) == 0)
    def _():
        _normmod_to_scratch(x_ref, shift_ref, scale_ref, g_ref, h_ref, 128)

    o_ref[0] = jnp.dot(h_ref[...], w_ref[...].astype(BF16), preferred_element_type=F32).astype(o_ref.dtype)


def _normmod_matmul(x, mod, g, w, out_dtype, tm=1024, tn=512):
    b, s, d = x.shape
    n = w.shape[1]
    tn = min(tn, n)
    est = (2 * _nbytes((tm, d), F32) + _nbytes((tm, d), BF16) + 2 * _nbytes((d, tn), F32)
           + _nbytes((d, tn), BF16) + 2 * _nbytes((tm, tn), F32) + _nbytes((tm, tn), F32))
    return pl.pallas_call(
        _normmod_matmul_kernel,
        name="normmod_matmul",
        grid=(b, s // tm, n // tn),
        in_specs=[
            pl.BlockSpec((1, tm, d), lambda bi, i, j: (bi, i, 0)),
            pl.BlockSpec((1, 1, d), lambda bi, i, j: (bi, 0, 0)),
            pl.BlockSpec((1, 1, d), lambda bi, i, j: (bi, 0, 1)),
            pl.BlockSpec((1, d), lambda bi, i, j: (0, 0)),
            pl.BlockSpec((d, tn), lambda bi, i, j: (0, j)),
        ],
        out_specs=pl.BlockSpec((1, tm, tn), lambda bi, i, j: (bi, i, j)),
        out_shape=jax.ShapeDtypeStruct((b, s, n), out_dtype),
        scratch_shapes=[pltpu.VMEM((tm, d), BF16)],
        compiler_params=_cparams(("parallel", "parallel", "arbitrary"), est),
    )(x, mod, mod, g.reshape(1, d), w)


def _mlp_kernel(x_ref, shift_ref, scale_ref, gate_ref, g_ref, w1_ref, w2_ref, fg_ref, o_ref, h_ref, *, final_norm):
    f = pl.program_id(2)

    @pl.when(f == 0)
    def _():
        _normmod_to_scratch(x_ref, shift_ref, scale_ref, g_ref, h_ref, 128)

        o_ref[...] = jnp.zeros_like(o_ref)

    a = jnp.dot(h_ref[...], w1_ref[...].astype(BF16), preferred_element_type=F32)
    a = jnp.square(jnp.maximum(a, 0.0)).astype(BF16)
    d = o_ref.shape[-1]
    for n0 in range(0, d, MLP_ACC_COLS):
        cols = slice(n0, n0 + MLP_ACC_COLS)
        o_ref[0, :, cols] += jnp.dot(a, w2_ref[:, cols].astype(BF16), preferred_element_type=F32)

    @pl.when(f == pl.num_programs(2) - 1)
    def _():
        gate = gate_ref[0]
        fg = fg_ref[...]
        rows = 128

        def body(r, carry):
            sl = pl.ds(pl.multiple_of(r * rows, rows), rows)
            y = x_ref[0, sl, :] + gate * o_ref[0, sl, :]
            if final_norm:
                y = _rms(y, fg)
            o_ref[0, sl, :] = y
            return carry

        lax.fori_loop(0, o_ref.shape[1] // rows, body, 0)


def _mlp(x, mod, g, w1, w2, final_g, final_norm, tm=1024, tf=256):
    b, s, d = x.shape
    ff = w1.shape[1]
    est = (4 * _nbytes((tm, d), F32) + _nbytes((tm, d), BF16) + 4 * _nbytes((d, tf), F32)
           + 2 * _nbytes((d, tf), BF16) + 2 * _nbytes((tm, MLP_ACC_COLS), F32) + 2 * _nbytes((tm, tf), F32))
    return pl.pallas_call(
        functools.partial(_mlp_kernel, final_norm=final_norm),
        name="mlp_relu2",
        grid=(b, s // tm, ff // tf),
        in_specs=[
            pl.BlockSpec((1, tm, d), lambda bi, i, f: (bi, i, 0)),
            pl.BlockSpec((1, 1, d), lambda bi, i, f: (bi, 0, 0)),
            pl.BlockSpec((1, 1, d), lambda bi, i, f: (bi, 0, 1)),
            pl.BlockSpec((1, 1, d), lambda bi, i, f: (bi, 0, 2)),
            pl.BlockSpec((1, d), lambda bi, i, f: (0, 0)),
            pl.BlockSpec((d, tf), lambda bi, i, f: (0, f)),
            pl.BlockSpec((tf, d), lambda bi, i, f: (f, 0)),
            pl.BlockSpec((1, d), lambda bi, i, f: (0, 0)),
        ],
        out_specs=pl.BlockSpec((1, tm, d), lambda bi, i, f: (bi, i, 0)),
        out_shape=jax.ShapeDtypeStruct((b, s, d), F32),
        scratch_shapes=[pltpu.VMEM((tm, d), BF16)],
        compiler_params=_cparams(("parallel", "parallel", "arbitrary"), est),
    )(x, mod, mod, mod, g.reshape(1, d), w1, w2, final_g.reshape(1, d))


def _proj_res_kernel(*refs, n_in):
    a_refs, w_refs = refs[:n_in], refs[n_in:2 * n_in]
    x_ref, gate_ref, o_ref = refs[2 * n_in:]
    acc = None
    for a_ref, w_ref in zip(a_refs, w_refs):
        p = jnp.dot(a_ref[0], w_ref[...].astype(BF16), preferred_element_type=F32)
        acc = p if acc is None else acc + p
    o_ref[0] = x_ref[0] + gate_ref[0] * acc


def _proj_res(a_list, w, x, mod, tm=1024, tn=512):
    b, s, d = x.shape
    n_in = len(a_list)
    kk = a_list[0].shape[-1]
    assert all(a.shape[-1] == kk for a in a_list) and w.shape[0] == n_in * kk
    est = (n_in * (2 * _nbytes((tm, kk), BF16) + 2 * _nbytes((kk, tn), F32) + _nbytes((kk, tn), BF16))
           + 6 * _nbytes((tm, tn), F32))
    in_specs = [pl.BlockSpec((1, tm, kk), lambda bi, i, j: (bi, i, 0)) for _ in a_list]
    in_specs += [pl.BlockSpec((kk, tn), functools.partial(lambda bi, i, j, r: (r, j), r=r)) for r in range(n_in)]
    in_specs += [
        pl.BlockSpec((1, tm, tn), lambda bi, i, j: (bi, i, j)),
        pl.BlockSpec((1, 1, tn), lambda bi, i, j: (bi, 0, 2 * (d // tn) + j)),
    ]
    return pl.pallas_call(
        functools.partial(_proj_res_kernel, n_in=n_in),
        name="proj_residual",
        grid=(b, s // tm, d // tn),
        in_specs=in_specs,
        out_specs=pl.BlockSpec((1, tm, tn), lambda bi, i, j: (bi, i, j)),
        out_shape=jax.ShapeDtypeStruct((b, s, d), F32),
        compiler_params=_cparams(("parallel", "parallel", "parallel"), est),
    )(*a_list, *([w] * n_in), x, mod)


def _rot_lanes(blk, cos_t, sin_up, sin_dn, half):
    return (blk * cos_t + pltpu.roll(blk, half, 1) * sin_up + pltpu.roll(blk, LANE - half, 1) * sin_dn)


def _mla_proj_kernel(lat_ref, gq_ref, gkv_ref, wq_ref, wk_ref, wv_ref, cos_ref, sup_ref, sdn_ref,
                     q_ref, k_ref, v_ref, *, q_scale):
    lat = lat_ref[0]
    qn = _rms(lat[:, :Q_LORA], gq_ref[...]).astype(BF16)
    kvn = _rms(lat[:, Q_LORA:Q_LORA + KV_LORA], gkv_ref[...]).astype(BF16)
    cos_t, sin_up, sin_dn = cos_ref[...], sup_ref[...], sdn_ref[...]
    half = MLA_ROPE // 2
    q = jnp.dot(qn, wq_ref[...], preferred_element_type=F32) * q_scale
    kn = jnp.dot(kvn, wk_ref[...], preferred_element_type=F32)
    v_ref[0] = jnp.dot(kvn, wv_ref[...], preferred_element_type=F32).astype(BF16)
    kr = _rot_lanes(lat[:, Q_LORA + KV_LORA:], cos_t, sin_up, sin_dn, half).astype(BF16)
    nope_w = MLA_HEADS * MLA_NOPE
    for h in range(MLA_HEADS):
        c0 = h * MLA_QK
        q_ref[0, :, c0:c0 + LANE] = q[:, h * LANE:(h + 1) * LANE].astype(BF16)
        qr = q[:, nope_w + h * LANE:nope_w + (h + 1) * LANE]
        q_ref[0, :, c0 + LANE:c0 + 2 * LANE] = _rot_lanes(qr, cos_t, sin_up, sin_dn, half).astype(BF16)
        k_ref[0, :, c0:c0 + LANE] = kn[:, h * LANE:(h + 1) * LANE].astype(BF16)
        k_ref[0, :, c0 + LANE:c0 + 2 * LANE] = kr


def _mla_proj(lat, gq, gkv, wq, wk, wv, tabs, q_scale, tm=512):
    b, s, _ = lat.shape
    hq = MLA_HEADS * MLA_QK
    hv = MLA_HEADS * MLA_V
    est = (2 * _nbytes((tm, LAT_W), F32) + 2 * _nbytes(wq.shape, BF16) + 2 * _nbytes(wk.shape, BF16)
           + 2 * _nbytes(wv.shape, BF16) + 4 * _nbytes((tm, hq), BF16) + 2 * _nbytes((tm, hv), BF16)
           + 3 * _nbytes((tm, hq), F32))
    full = lambda shape: pl.BlockSpec(shape, lambda bi, i: (0,) * len(shape))
    tab = pl.BlockSpec((tm, LANE), lambda bi, i: (i, 0))
    return pl.pallas_call(
        functools.partial(_mla_proj_kernel, q_scale=q_scale),
        name="mla_proj",
        grid=(b, s // tm),
        in_specs=[
            pl.BlockSpec((1, tm, LAT_W), lambda bi, i: (bi, i, 0)),
            full((1, Q_LORA)), full((1, KV_LORA)), full(wq.shape), full(wk.shape), full(wv.shape),
            tab, tab, tab,
        ],
        out_specs=[
            pl.BlockSpec((1, tm, hq), lambda bi, i: (bi, i, 0)),
            pl.BlockSpec((1, tm, hq), lambda bi, i: (bi, i, 0)),
            pl.BlockSpec((1, tm, hv), lambda bi, i: (bi, i, 0)),
        ],
        out_shape=[
            jax.ShapeDtypeStruct((b, s, hq), BF16),
            jax.ShapeDtypeStruct((b, s, hq), BF16),
            jax.ShapeDtypeStruct((b, s, hv), BF16),
        ],
        compiler_params=_cparams(("parallel", "parallel"), est),
    )(lat, gq.reshape(1, -1), gkv.reshape(1, -1), wq, wk, wv, *tabs)


def _mla_attn_kernel(q_ref, k_ref, v_ref, o_ref):
    s = lax.dot_general(q_ref[0], k_ref[0], (((1,), (1,)), ((), ())), preferred_element_type=F32)
    m = jnp.max(s, axis=-1, keepdims=True)
    p = jnp.exp2(s - m)
    l = jnp.sum(p, axis=-1, keepdims=True)
    o = jnp.dot(p.astype(BF16), v_ref[0], preferred_element_type=F32)
    o_ref[0] = (o / l).astype(BF16)


def _mla_attn(q, k, v, tq=512):
    b, s, _ = q.shape
    est = (2 * _nbytes((tq, MLA_QK), BF16) + 2 * _nbytes((s, MLA_QK), BF16) + 2 * _nbytes((s, MLA_V), BF16)
           + 2 * _nbytes((tq, MLA_V), BF16) + 3 * _nbytes((tq, s), F32))
    return pl.pallas_call(
        _mla_attn_kernel,
        name="mla_attn",
        grid=(b, MLA_HEADS, s // tq),
        in_specs=[
            pl.BlockSpec((1, tq, MLA_QK), lambda bi, h, i: (bi, i, h)),
            pl.BlockSpec((1, s, MLA_QK), lambda bi, h, i: (bi, 0, h)),
            pl.BlockSpec((1, s, MLA_V), lambda bi, h, i: (bi, 0, h)),
        ],
        out_specs=pl.BlockSpec((1, tq, MLA_V), lambda bi, h, i: (bi, i, h)),
        out_shape=jax.ShapeDtypeStruct((b, s, MLA_HEADS * MLA_V), BF16),
        compiler_params=_cparams(("parallel", "parallel", "parallel"), est),
    )(q, k, v)


def _gqa_kernel(q_ref, k_ref, v_ref, sink_ref, cos_ref, sup_ref, sdn_ref, o_ref, k_scr, *, scale):
    half = GQA_ROT // 2
    s_len = k_scr.shape[0]
    k_scr[...] = _rot_lanes(k_ref[0].astype(F32), cos_ref[...], sup_ref[...], sdn_ref[...], half).astype(BF16)
    sink = sink_ref[0]
    rows = GQA_GROUP * BAND
    for n in range(s_len // BAND):
        r0 = n * BAND
        cos_t, sin_up, sin_dn = cos_ref[r0:r0 + BAND, :], sup_ref[r0:r0 + BAND, :], sdn_ref[r0:r0 + BAND, :]
        qn = q_ref[0, r0:r0 + BAND, :].astype(F32)
        q_st = jnp.concatenate(
            [_rot_lanes(qn[:, g * LANE:(g + 1) * LANE], cos_t, sin_up, sin_dn, half) for g in range(GQA_GROUP)],
            axis=0)
        q_st = (q_st * scale).astype(BF16)
        lo, hi = max(0, r0 - BAND), min(s_len, r0 + 2 * BAND)
        w = hi - lo
        s = lax.dot_general(q_st, k_scr[lo:hi, :], (((1,), (1,)), ((), ())), preferred_element_type=F32)
        kpos = lo + lax.broadcasted_iota(jnp.int32, (rows, w), 1)
        qpos = r0 + (lax.broadcasted_iota(jnp.int32, (rows, w), 0) & (BAND - 1))
        s = jnp.where(jnp.abs(kpos - qpos) <= WINDOW, s, NEG_INF)
        m = jnp.maximum(jnp.max(s, axis=-1, keepdims=True), sink)
        p = jnp.exp(s - m)
        l = jnp.sum(p, axis=-1, keepdims=True) + jnp.exp(sink - m)
        o = jnp.dot(p.astype(BF16), v_ref[0, lo:hi, :], preferred_element_type=F32) / l
        for g in range(GQA_GROUP):
            o_ref[0, r0:r0 + BAND, g * LANE:(g + 1) * LANE] = o[g * BAND:(g + 1) * BAND, :].astype(BF16)


def _gqa_attn(qkv, sinks, tabs):
    b, s, _ = qkv.shape
    gw = GQA_GROUP * GQA_HEAD_DIM
    sink_col = jnp.broadcast_to(sinks.astype(F32).reshape(GQA_KV_HEADS, GQA_GROUP, 1, 1),
                                (GQA_KV_HEADS, GQA_GROUP, BAND, 1)).reshape(GQA_KV_HEADS, GQA_GROUP * BAND, 1)
    est = (4 * _nbytes((s, gw), BF16) + 5 * _nbytes((s, LANE), BF16) + 6 * _nbytes((s, LANE), F32)
           + 8 * _nbytes((GQA_GROUP * BAND, 3 * BAND), F32))
    tab = pl.BlockSpec((s, LANE), lambda bi, h: (0, 0))
    return pl.pallas_call(
        functools.partial(_gqa_kernel, scale=GQA_HEAD_DIM ** -0.5),
        name="gqa_window_attn",
        grid=(b, GQA_KV_HEADS),
        in_specs=[
            pl.BlockSpec((1, s, gw), lambda bi, h: (bi, 0, h)),
            pl.BlockSpec((1, s, LANE), lambda bi, h: (bi, 0, GQA_HEADS + h)),
            pl.BlockSpec((1, s, LANE), lambda bi, h: (bi, 0, GQA_HEADS + GQA_KV_HEADS + h)),
            pl.BlockSpec((1, GQA_GROUP * BAND, 1), lambda bi, h: (h, 0, 0)),
            tab, tab, tab,
        ],
        out_specs=pl.BlockSpec((1, s, gw), lambda bi, h: (bi, 0, h)),
        out_shape=jax.ShapeDtypeStruct((b, s, GQA_HEADS * GQA_HEAD_DIM), BF16),
        scratch_shapes=[pltpu.VMEM((s, LANE), BF16)],
        compiler_params=_cparams(("parallel", "parallel"), est),
    )(qkv, qkv, qkv, sink_col, *tabs)


def _short_conv_kernel(x_ref, w_ref, b_ref, o_ref):
    x = x_ref[0].astype(F32)
    s_len = x.shape[0]
    row = lax.broadcasted_iota(jnp.int32, x.shape, 0)
    prev = jnp.where(row == 0, 0.0, pltpu.roll(x, 1, 0))
    nxt = jnp.where(row == s_len - 1, 0.0, pltpu.roll(x, s_len - 1, 0))
    y = b_ref[...] + prev * w_ref[0:1, :]
    y = y + x * w_ref[1:2, :]
    y = y + nxt * w_ref[2:3, :]
    o_ref[0] = y.astype(BF16)


def _short_conv(hy, w, bias, tc=256):
    b, s, c = hy.shape
    est = 4 * _nbytes((s, tc), BF16) + 6 * _nbytes((s, tc), F32)
    return pl.pallas_call(
        _short_conv_kernel,
        name="hyena_short_conv",
        grid=(b, c // tc),
        in_specs=[
            pl.BlockSpec((1, s, tc), lambda bi, j: (bi, 0, j)),
            pl.BlockSpec((HY_SHORT, tc), lambda bi, j: (0, j)),
            pl.BlockSpec((1, tc), lambda bi, j: (0, j)),
        ],
        out_specs=pl.BlockSpec((1, s, tc), lambda bi, j: (bi, 0, j)),
        out_shape=jax.ShapeDtypeStruct((b, s, c), BF16),
        compiler_params=_cparams(("parallel", "parallel"), est),
    )(hy, w, bias.reshape(1, c))


def _filter_gen_kernel(z_ref, w1_ref, b1_ref, w2_ref, b2_ref, w3_ref, b3_ref, fr_ref, w4f_ref, w4b_ref,
                       kf_ref, kb_ref):
    dot = functools.partial(jnp.dot, preferred_element_type=F32, precision=HIGHEST)
    fr = fr_ref[...]
    h = jnp.sin(fr * (dot(z_ref[...], w1_ref[...]) + b1_ref[...]))
    h = jnp.sin(fr * (dot(h, w2_ref[...]) + b2_ref[...]))
    h = jnp.sin(fr * (dot(h, w3_ref[...]) + b3_ref[...]))
    hf = dot(h, w4f_ref[...])
    hb = dot(h, w4b_ref[...])
    n_lag, tn = hf.shape
    row = lax.broadcasted_iota(jnp.int32, (n_lag, tn), 0)
    col = pl.program_id(1) * tn + lax.broadcasted_iota(jnp.int32, (1, tn), 1)
    t = row.astype(F32) / (n_lag - 1)
    max_decay = math.log(HY_TARGET) / HY_DECAY_PCT_SHORT
    min_decay = math.log(HY_TARGET) / HY_DECAY_PCT_LONG
    delta = min_decay + (max_decay - min_decay) * (col.astype(F32) / (HY_D - 1))
    decay = jnp.exp(-t * jnp.abs(delta))
    kf = hf * decay
    kb = jnp.where(row == 0, 0.0, hb * decay)
    inv = 1.0 / (jnp.sum(jnp.abs(kf), axis=0, keepdims=True) + jnp.sum(jnp.abs(kb), axis=0, keepdims=True))
    kf_ref[0] = (kf * inv).astype(BF16)
    kb_ref[0] = (kb * inv).astype(BF16)


def _filter_gen(z, w1, b1, w2, b2, w3, b3, freq, w4, tn=256):
    n_lag = z.shape[0]
    nblk = HY_D // tn
    full = lambda shape: pl.BlockSpec(shape, lambda o, j: (0,) * len(shape))
    est = 10 * _nbytes((n_lag, tn), F32) + 4 * _nbytes((n_lag, LANE), F32) + 4 * _nbytes((n_lag, tn), BF16)
    out = jax.ShapeDtypeStruct((HY_ORDER, n_lag, HY_D), BF16)
    return pl.pallas_call(
        _filter_gen_kernel,
        name="hyena_filter_gen",
        grid=(HY_ORDER, nblk),
        in_specs=[
            full(z.shape), full(w1.shape), full((1, HY_FFN)), full(w2.shape), full((1, HY_FFN)),
            full(w3.shape), full((1, HY_FFN)), full((1, HY_FFN)),
            pl.BlockSpec((HY_FFN, tn), lambda o, j: (0, 2 * o * nblk + j)),
            pl.BlockSpec((HY_FFN, tn), lambda o, j: (0, (2 * o + 1) * nblk + j)),
        ],
        out_specs=[pl.BlockSpec((1, n_lag, tn), lambda o, j: (o, 0, j))] * 2,
        out_shape=[out, out],
        compiler_params=_cparams(("parallel", "parallel"), est),
    )(z, w1, b1.reshape(1, -1), w2, b2.reshape(1, -1), w3, b3.reshape(1, -1), freq.reshape(1, -1), w4, w4)


def _filter_dft_kernel(ct_ref, st_ref, kf_ref, kb_ref, a_ref, bq_ref, a2_ref):
    dot = functools.partial(jnp.dot, preferred_element_type=F32)
    ct, st = ct_ref[...], st_ref[...]
    kf, kb = kf_ref[0], kb_ref[0]
    fc, fs, bc, bs = dot(ct, kf), dot(st, kf), dot(ct, kb), dot(st, kb)
    tf = ct.shape[0]
    row = pl.program_id(2) * tf + lax.broadcasted_iota(jnp.int32, fc.shape, 0)
    is0 = row == 0
    wgt = jnp.where(is0, 1.0 / DFT_N, 2.0 / DFT_N)
    a = (fc + bc) * wgt
    a_ref[0] = a
    bq_ref[0] = jnp.where(is0, 0.0, (bs - fs) * wgt)
    a2_ref[0] = jnp.where(is0, (fs + bs) * wgt, a)


def _filter_dft(ct, st, kf, kb, tf=512, tn=512):
    n_f, n_s = ct.shape
    est = (4 * _nbytes((tf, n_s), BF16) + 4 * _nbytes((n_s, tn), BF16) + 6 * _nbytes((tf, tn), F32)
           + 8 * _nbytes((tf, tn), F32))
    out = jax.ShapeDtypeStruct((HY_ORDER, n_f, HY_D), F32)
    return pl.pallas_call(
        _filter_dft_kernel,
        name="hyena_filter_dft",
        grid=(HY_ORDER, HY_D // tn, n_f // tf),
        in_specs=[
            pl.BlockSpec((tf, n_s), lambda o, j, k: (k, 0)),
            pl.BlockSpec((tf, n_s), lambda o, j, k: (k, 0)),
            pl.BlockSpec((1, n_s, tn), lambda o, j, k: (o, 0, j)),
            pl.BlockSpec((1, n_s, tn), lambda o, j, k: (o, 0, j)),
        ],
        out_specs=[pl.BlockSpec((1, tf, tn), lambda o, j, k: (o, k, j))] * 3,
        out_shape=[out, out, out],
        compiler_params=_cparams(("parallel", "parallel", "parallel"), est),
    )(ct, st, kf, kb)


def _conv_fwd_kernel(ct_ref, st_ref, u_ref, a_ref, bq_ref, a2_ref, yr_ref, ys_ref):
    u = u_ref[0]
    uc = jnp.dot(ct_ref[...], u, preferred_element_type=F32)
    us = jnp.dot(st_ref[...], u, preferred_element_type=F32)
    bq = bq_ref[0]
    yr_ref[0] = (uc * a_ref[0] + us * bq).astype(BF16)
    ys_ref[0] = (us * a2_ref[0] - uc * bq).astype(BF16)


def _conv_fwd(ct, st, u, u_blk0, spec, order, tf=512, tn=512):
    a, bq, a2 = spec
    b, s, _ = u.shape
    n_f = ct.shape[0]
    est = (4 * _nbytes((tf, s), BF16) + 2 * _nbytes((s, tn), BF16) + 6 * _nbytes((tf, tn), F32)
           + 4 * _nbytes((tf, tn), BF16) + 6 * _nbytes((tf, tn), F32))
    coef = pl.BlockSpec((1, tf, tn), lambda bi, j, k: (order, k, j))
    out = jax.ShapeDtypeStruct((b, n_f, HY_D), BF16)
    return pl.pallas_call(
        _conv_fwd_kernel,
        name="hyena_conv_fwd",
        grid=(b, HY_D // tn, n_f // tf),
        in_specs=[
            pl.BlockSpec((tf, s), lambda bi, j, k: (k, 0)),
            pl.BlockSpec((tf, s), lambda bi, j, k: (k, 0)),
            pl.BlockSpec((1, s, tn), lambda bi, j, k: (bi, 0, u_blk0 * (HY_D // tn) + j)),
            coef, coef, coef,
        ],
        out_specs=[pl.BlockSpec((1, tf, tn), lambda bi, j, k: (bi, k, j))] * 2,
        out_shape=[out, out],
        compiler_params=_cparams(("parallel", "parallel", "parallel"), est),
    )(ct, st, u, a, bq, a2)


def _conv_inv_kernel(ct_ref, stt_ref, yr_ref, ys_ref, g_ref, u_ref, skip_ref, o_ref):
    y = jnp.dot(ct_ref[...], yr_ref[0], preferred_element_type=F32)
    y = y + jnp.dot(stt_ref[...], ys_ref[0], preferred_element_type=F32)
    u = u_ref[0].astype(F32)
    o_ref[0] = (g_ref[0].astype(F32) * (y + u * skip_ref[0])).astype(BF16)


def _conv_inv(ct, stt, yr, ys, gsrc, g_blk0, usrc, u_blk0, skip, order, tt=512, tn=512):
    b, n_f, _ = yr.shape
    s = ct.shape[0]
    nblk = HY_D // tn
    est = (4 * _nbytes((tt, n_f), BF16) + 4 * _nbytes((n_f, tn), BF16) + 6 * _nbytes((tt, tn), BF16)
           + 6 * _nbytes((tt, tn), F32))
    return pl.pallas_call(
        _conv_inv_kernel,
        name="hyena_conv_inv",
        grid=(b, nblk, s // tt),
        in_specs=[
            pl.BlockSpec((tt, n_f), lambda bi, j, k: (k, 0)),
            pl.BlockSpec((tt, n_f), lambda bi, j, k: (k, 0)),
            pl.BlockSpec((1, n_f, tn), lambda bi, j, k: (bi, 0, j)),
            pl.BlockSpec((1, n_f, tn), lambda bi, j, k: (bi, 0, j)),
            pl.BlockSpec((1, tt, tn), lambda bi, j, k: (bi, k, g_blk0 * nblk + j)),
            pl.BlockSpec((1, tt, tn), lambda bi, j, k: (bi, k, u_blk0 * nblk + j)),
            pl.BlockSpec((1, 1, tn), lambda bi, j, k: (order, 0, j)),
        ],
        out_specs=pl.BlockSpec((1, tt, tn), lambda bi, j, k: (bi, k, j)),
        out_shape=jax.ShapeDtypeStruct((b, s, HY_D), BF16),
        compiler_params=_cparams(("parallel", "parallel", "parallel"), est),
    )(ct, stt, yr, ys, gsrc, usrc, skip.reshape(HY_ORDER, 1, HY_D))


def _rope_tables(seq, rot_dim):
    half = rot_dim // 2
    pos = jnp.arange(seq, dtype=F32)
    inv = ROPE_THETA ** (-jnp.arange(0, rot_dim, 2, dtype=F32) / rot_dim)
    ang = pos[:, None] * inv[None, :]
    cos, sin = jnp.cos(ang), jnp.sin(ang)
    rest = LANE - rot_dim
    return cos, sin, half, rest


def _rope_lane_tables(seq, rot_dim, rest_passthrough):
    cos, sin, half, rest = _rope_tables(seq, rot_dim)
    fill = jnp.ones((seq, rest), F32) if rest_passthrough else jnp.zeros((seq, rest), F32)
    zero_h = jnp.zeros((seq, half), F32)
    zero_r = jnp.zeros((seq, rest), F32)
    cos_t = jnp.concatenate([cos, cos, fill], axis=1)
    sin_up = jnp.concatenate([zero_h, sin, zero_r], axis=1)
    sin_dn = jnp.concatenate([-sin, zero_h, zero_r], axis=1)
    return cos_t, sin_up, sin_dn


def _dft_tables(n_half):
    n = 2 * n_half
    f = jnp.arange(n_half, dtype=jnp.int32)[:, None]
    s = jnp.arange(n_half, dtype=jnp.int32)[None, :]
    ang = ((f * s) & (n - 1)).astype(F32) * (2.0 * math.pi / n)
    ct = jnp.cos(ang)
    alt = (1 - 2 * (s & 1)).astype(F32)
    st = jnp.where(f == 0, alt, jnp.sin(ang))
    return ct.astype(BF16), st.astype(BF16), st.T.astype(BF16)


def _filter_features(n_lag):
    t = jnp.linspace(0.0, 1.0, n_lag, dtype=F32)[:, None]
    bands = (HY_EMB - 1) // 2
    wpos = 2.0 * math.pi * jnp.arange(n_lag, dtype=F32) / n_lag
    fb = jnp.linspace(1e-4, bands - 1, bands, dtype=F32)
    fw = wpos[:, None] * fb[None, :]
    z = jnp.concatenate([t, jnp.cos(fw), -jnp.sin(fw)], axis=-1)
    return jnp.pad(z, ((0, 0), (0, LANE - HY_EMB)))


def kernel(x, c, ada_mix_w, ada_mix_b, norm_mix_g, ada_mlp_w, ada_mlp_b, norm_mlp_g, w_mlp_in, w_mlp_out, e_w_in, e_q_norm_g, e_kv_norm_g, e_w_uq, e_w_ukv, e_conv_w, e_conv_b, e_f_w1, e_f_b1, e_f_w2, e_f_b2, e_f_w3, e_f_b3, e_f_freq, e_f_w4, e_hy_skip, e_w_out, o_w_qkv, o_sinks, o_w_o, final_norm_g):
    b, s, d = x.shape
    c8 = jnp.pad(c, ((0, 8 - b), (0, 0)))
    mod_mix = _ada(c8, ada_mix_w, ada_mix_b)[:, :b].reshape(DEPTH, b, 1, 3 * d)
    mod_mlp = _ada(c8, ada_mlp_w, ada_mlp_b)[:, :b].reshape(DEPTH, b, 1, 3 * d)

    mla_tabs = _rope_lane_tables(s, MLA_ROPE, rest_passthrough=False)
    gqa_tabs = _rope_lane_tables(s, GQA_ROT, rest_passthrough=True)
    ct, st, stt = _dft_tables(s)
    z_feat = _filter_features(s)
    q_scale = (MLA_NOPE + MLA_ROPE) ** -0.5 * math.log2(math.e)

    for l in range(DEPTH):
        i = l // 2
        if l % 2 == 0:
            w_in = e_w_in[i]
            w_lat = jnp.pad(w_in[:, :Q_LORA + KV_LORA + MLA_ROPE], ((0, 0), (0, LANE - MLA_ROPE)))
            w_hy = w_in[:, Q_LORA + KV_LORA + MLA_ROPE:]
            lat = _normmod_matmul(x, mod_mix[l], norm_mix_g[l], w_lat, F32, tn=LAT_W)
            hy_in = _normmod_matmul(x, mod_mix[l], norm_mix_g[l], w_hy, BF16)

            w_uq = e_w_uq[i].reshape(Q_LORA, MLA_HEADS, MLA_NOPE + MLA_ROPE)
            wq = jnp.concatenate([
                w_uq[:, :, :MLA_NOPE].reshape(Q_LORA, -1),
                jnp.pad(w_uq[:, :, MLA_NOPE:], ((0, 0), (0, 0), (0, LANE - MLA_ROPE))).reshape(Q_LORA, -1),
            ], axis=1).astype(BF16)
            w_ukv = e_w_ukv[i].reshape(KV_LORA, MLA_HEADS, MLA_NOPE + MLA_V)
            wk = w_ukv[:, :, :MLA_NOPE].reshape(KV_LORA, -1).astype(BF16)
            wv = w_ukv[:, :, MLA_NOPE:].reshape(KV_LORA, -1).astype(BF16)
            q, k, v = _mla_proj(lat, e_q_norm_g[i], e_kv_norm_g[i], wq, wk, wv, mla_tabs, q_scale)
            a_mla = _mla_attn(q, k, v)

            w1 = jnp.pad(e_f_w1[i], ((0, LANE - HY_EMB), (0, 0)))
            kf, kb = _filter_gen(z_feat, w1, e_f_b1[i], e_f_w2[i], e_f_b2[i], e_f_w3[i], e_f_b3[i],
                                 e_f_freq[i], e_f_w4[i])
            spec = _filter_dft(ct, st, kf, kb)
            u = _short_conv(hy_in, e_conv_w[i], e_conv_b[i])
            yr, ys = _conv_fwd(ct, st, u, 0, spec, 0)
            zc = _conv_inv(ct, stt, yr, ys, u, 1, u, 0, e_hy_skip[i], 0)
            yr, ys = _conv_fwd(ct, st, zc, 0, spec, 1)
            b_hy = _conv_inv(ct, stt, yr, ys, u, 2, zc, 0, e_hy_skip[i], 1)

            x = _proj_res([a_mla, b_hy], e_w_out[i], x, mod_mix[l])
        else:
            qkv = _normmod_matmul(x, mod_mix[l], norm_mix_g[l], o_w_qkv[i], BF16)
            o = _gqa_attn(qkv, o_sinks[i], gqa_tabs)
            x = _proj_res([o], o_w_o[i], x, mod_mix[l])
        x = _mlp(x, mod_mlp[l], norm_mlp_g[l], w_mlp_in[l], w_mlp_out[l], final_norm_g, l == DEPTH - 1)
    return x
```

```python
import functools
import math

import jax
import jax.numpy as jnp
from jax import lax
from jax.experimental import pallas as pl
from jax.experimental.pallas import tpu as pltpu

F32 = jnp.float32
BF16 = jnp.bfloat16
HIGHEST = lax.Precision.HIGHEST

D_MODEL = 2048
BATCH = 4
SEQ = 2048
DEPTH = 4
RMS_EPS = 1e-6
ROPE_THETA = 500000.0
NEG_INF = -1e30
MLA_HEADS = 8
MLA_NOPE = 128
MLA_ROPE = 64
MLA_V = 128
Q_LORA = 512
KV_LORA = 256
HY_D = 1024
HY_ORDER = 2
HY_SHORT = 3
HY_EMB = 33
HY_FFN = 64
HY_DECAY_PCT_SHORT = 0.3
HY_DECAY_PCT_LONG = 1.5
HY_TARGET = 1e-2
GQA_HEADS = 16
GQA_KV_HEADS = 4
GQA_HEAD_DIM = 128
GQA_ROT = GQA_HEAD_DIM // 4
GQA_GROUP = GQA_HEADS // GQA_KV_HEADS
WINDOW = 128
BAND = 128
D_FF = 4 * D_MODEL
LAT_W = Q_LORA + KV_LORA + 128
MLA_QK = 256
DFT_N = 2 * SEQ

LANE = 128
V7X_VMEM_BYTES = 64 * 1024 * 1024
V7X_VMEM_BUDGET = 56 * 1024 * 1024


def _cparams(semantics, est_bytes):
    limit = int(min(V7X_VMEM_BUDGET, max(32 * 1024 * 1024, est_bytes * 3 // 2)))
    return pltpu.CompilerParams(dimension_semantics=semantics, vmem_limit_bytes=limit)


def _nbytes(shape, dtype):
    return math.prod(shape) * jnp.dtype(dtype).itemsize


def _rms(x, g):
    ms = jnp.mean(x * x, axis=-1, keepdims=True)
    return x * lax.rsqrt(ms + RMS_EPS) * g


def _ada_kernel(c_ref, w_ref, b_ref, o_ref):
    cv = c_ref[...]
    s = cv * (1.0 / (1.0 + jnp.exp(-cv)))
    o_ref[0] = jnp.dot(s.astype(BF16), w_ref[0].astype(BF16), preferred_element_type=F32) + b_ref[0]


def _ada(c8, w, b, tn=512):
    n_l, d, n = w.shape
    est = 2 * _nbytes((d, tn), F32) + 4 * _nbytes((8, tn), F32) + 2 * _nbytes((8, d), F32)
    return pl.pallas_call(
        _ada_kernel,
        name="ada_modulation",
        grid=(n_l, n // tn),
        in_specs=[
            pl.BlockSpec((8, d), lambda l, j: (0, 0)),
            pl.BlockSpec((1, d, tn), lambda l, j: (l, 0, j)),
            pl.BlockSpec((1, 1, tn), lambda l, j: (l, 0, j)),
        ],
        out_specs=pl.BlockSpec((1, 8, tn), lambda l, j: (l, 0, j)),
        out_shape=jax.ShapeDtypeStruct((n_l, 8, n), F32),
        compiler_params=_cparams(("parallel", "parallel"), est),
    )(c8, w, b.reshape(n_l, 1, n))


def _normmod_to_scratch(x_ref, shift_ref, scale_ref, g_ref, h_ref, rows):
    g = g_ref[...]
    sh = shift_ref[0]
    sc = 1.0 + scale_ref[0]
    tm = h_ref.shape[0]

    def body(r, carry):
        sl = pl.ds(pl.multiple_of(r * rows, rows), rows)
        h_ref[sl, :] = (_rms(x_ref[0, sl, :], g) * sc + sh).astype(BF16)
        return carry

    lax.fori_loop(0, tm // rows, body, 0)


def _normmod_matmul_kernel(x_ref, shift_ref, scale_ref, g_ref, w_ref, o_ref, h_ref):
    @pl.when(pl.program_id(2) == 0)
    def _():
        _normmod_to_scratch(x_ref, shift_ref, scale_ref, g_ref, h_ref, 128)

    o_ref[0] = jnp.dot(h_ref[...], w_ref[...].astype(BF16), preferred_element_type=F32).astype(o_ref.dtype)


def _normmod_matmul(x, mod, g, w_all, layer, out_dtype, tm=1024, tn=512):
    b, s, d = x.shape
    n = w_all.shape[2]
    tn = min(tn, n)
    est = (2 * _nbytes((tm, d), F32) + _nbytes((tm, d), BF16) + 2 * _nbytes((d, tn), F32)
           + _nbytes((d, tn), BF16) + 2 * _nbytes((tm, tn), F32) + _nbytes((tm, tn), F32))
    return pl.pallas_call(
        _normmod_matmul_kernel,
        name="normmod_matmul",
        grid=(b, s // tm, n // tn),
        in_specs=[
            pl.BlockSpec((1, tm, d), lambda bi, i, j: (bi, i, 0)),
            pl.BlockSpec((1, 1, d), lambda bi, i, j: (bi, 0, 0)),
            pl.BlockSpec((1, 1, d), lambda bi, i, j: (bi, 0, 1)),
            pl.BlockSpec((1, d), lambda bi, i, j: (0, 0)),
            pl.BlockSpec((None, d, tn), lambda bi, i, j: (layer, 0, j)),
        ],
        out_specs=pl.BlockSpec((1, tm, tn), lambda bi, i, j: (bi, i, j)),
        out_shape=jax.ShapeDtypeStruct((b, s, n), out_dtype),
        scratch_shapes=[pltpu.VMEM((tm, d), BF16)],
        compiler_params=_cparams(("parallel", "parallel", "arbitrary"), est),
    )(x, mod, mod, g.reshape(1, d), w_all)


def _mlp_kernel(x_ref, shift_ref, scale_ref, gate_ref, g_ref, w1_ref, w2_ref, fg_ref, o_ref, h_ref, *, final_norm):
    f = pl.program_id(2)

    @pl.when(f == 0)
    def _():
        _normmod_to_scratch(x_ref, shift_ref, scale_ref, g_ref, h_ref, 128)

        o_ref[...] = jnp.zeros_like(o_ref)

    a = jnp.dot(h_ref[...], w1_ref[...].astype(BF16), preferred_element_type=F32)
    a = jnp.square(jnp.maximum(a, 0.0)).astype(BF16)
    o_ref[0] += jnp.dot(a, w2_ref[...].astype(BF16), preferred_element_type=F32)

    @pl.when(f == pl.num_programs(2) - 1)
    def _():
        gate = gate_ref[0]
        fg = fg_ref[...]
        rows = 128

        def body(r, carry):
            sl = pl.ds(pl.multiple_of(r * rows, rows), rows)
            y = x_ref[0, sl, :] + gate * o_ref[0, sl, :]
            if final_norm:
                y = _rms(y, fg)
            o_ref[0, sl, :] = y
            return carry

        lax.fori_loop(0, o_ref.shape[1] // rows, body, 0)


def _mlp(x, mod, g, w1_all, w2_all, layer, final_g, final_norm, tm=1024, tf=512):
    b, s, d = x.shape
    ff = w1_all.shape[2]
    est = (3 * _nbytes((tm, d), F32) + _nbytes((tm, d), BF16) + 4 * _nbytes((d, tf), F32)
           + 2 * _nbytes((d, tf), BF16) + 2 * _nbytes((tm, tf), F32))
    return pl.pallas_call(
        functools.partial(_mlp_kernel, final_norm=final_norm),
        name="mlp_relu2",
        grid=(b, s // tm, ff // tf),
        in_specs=[
            pl.BlockSpec((1, tm, d), lambda bi, i, f: (bi, i, 0), pipeline_mode=pl.Buffered(1)),
            pl.BlockSpec((1, 1, d), lambda bi, i, f: (bi, 0, 0)),
            pl.BlockSpec((1, 1, d), lambda bi, i, f: (bi, 0, 1)),
            pl.BlockSpec((1, 1, d), lambda bi, i, f: (bi, 0, 2)),
            pl.BlockSpec((1, d), lambda bi, i, f: (0, 0)),
            pl.BlockSpec((None, d, tf), lambda bi, i, f: (layer, 0, f)),
            pl.BlockSpec((None, tf, d), lambda bi, i, f: (layer, f, 0)),
            pl.BlockSpec((1, d), lambda bi, i, f: (0, 0)),
        ],
        out_specs=pl.BlockSpec((1, tm, d), lambda bi, i, f: (bi, i, 0)),
        out_shape=jax.ShapeDtypeStruct((b, s, d), F32),
        scratch_shapes=[pltpu.VMEM((tm, d), BF16)],
        compiler_params=_cparams(("parallel", "parallel", "arbitrary"), est),
    )(x, mod, mod, mod, g.reshape(1, d), w1_all, w2_all, final_g.reshape(1, d))


def _proj_res_kernel(*refs, n_in):
    a_refs, w_refs = refs[:n_in], refs[n_in:2 * n_in]
    x_ref, gate_ref, o_ref = refs[2 * n_in:]
    acc = None
    for a_ref, w_ref in zip(a_refs, w_refs):
        p = jnp.dot(a_ref[0], w_ref[...].astype(BF16), preferred_element_type=F32)
        acc = p if acc is None else acc + p
    o_ref[0] = x_ref[0] + gate_ref[0] * acc


def _proj_res(a_list, w_all, layer, x, mod, tm=2048, tn=512):
    b, s, d = x.shape
    n_in = len(a_list)
    kk = a_list[0].shape[-1]
    assert all(a.shape[-1] == kk for a in a_list) and w_all.shape[1] == n_in * kk
    est = (n_in * (2 * _nbytes((tm, kk), BF16) + 2 * _nbytes((kk, tn), F32) + _nbytes((kk, tn), BF16))
           + 6 * _nbytes((tm, tn), F32))
    in_specs = [pl.BlockSpec((1, tm, kk), lambda bi, i, j: (bi, i, 0)) for _ in a_list]
    in_specs += [pl.BlockSpec((None, kk, tn), functools.partial(lambda bi, i, j, r: (layer, r, j), r=r))
                 for r in range(n_in)]
    in_specs += [
        pl.BlockSpec((1, tm, tn), lambda bi, i, j: (bi, i, j)),
        pl.BlockSpec((1, 1, tn), lambda bi, i, j: (bi, 0, 2 * (d // tn) + j)),
    ]
    return pl.pallas_call(
        functools.partial(_proj_res_kernel, n_in=n_in),
        name="proj_residual",
        grid=(b, s // tm, d // tn),
        in_specs=in_specs,
        out_specs=pl.BlockSpec((1, tm, tn), lambda bi, i, j: (bi, i, j)),
        out_shape=jax.ShapeDtypeStruct((b, s, d), F32),
        compiler_params=_cparams(("parallel", "parallel", "parallel"), est),
    )(*a_list, *([w_all] * n_in), x, mod)


def _rot_lanes(blk, cos_t, sin_up, sin_dn, half):
    return (blk * cos_t + pltpu.roll(blk, half, 1) * sin_up + pltpu.roll(blk, LANE - half, 1) * sin_dn)


def _mla_proj_kernel(lat_ref, gq_ref, gkv_ref, wq_ref, wk_ref, wv_ref, cos_ref, sup_ref, sdn_ref,
                     q_ref, k_ref, v_ref, *, q_scale):
    lat = lat_ref[0]
    qn = _rms(lat[:, :Q_LORA], gq_ref[...]).astype(BF16)
    kvn = _rms(lat[:, Q_LORA:Q_LORA + KV_LORA], gkv_ref[...]).astype(BF16)
    cos_t, sin_up, sin_dn = cos_ref[...], sup_ref[...], sdn_ref[...]
    half = MLA_ROPE // 2
    q = jnp.dot(qn, wq_ref[...], preferred_element_type=F32) * q_scale
    kn = jnp.dot(kvn, wk_ref[...], preferred_element_type=F32)
    v_ref[0] = jnp.dot(kvn, wv_ref[...], preferred_element_type=F32).astype(BF16)
    kr = _rot_lanes(lat[:, Q_LORA + KV_LORA:], cos_t, sin_up, sin_dn, half).astype(BF16)
    nope_w = MLA_HEADS * MLA_NOPE
    for h in range(MLA_HEADS):
        c0 = h * MLA_QK
        q_ref[0, :, c0:c0 + LANE] = q[:, h * LANE:(h + 1) * LANE].astype(BF16)
        qr = q[:, nope_w + h * LANE:nope_w + (h + 1) * LANE]
        q_ref[0, :, c0 + LANE:c0 + 2 * LANE] = _rot_lanes(qr, cos_t, sin_up, sin_dn, half).astype(BF16)
        k_ref[0, :, c0:c0 + LANE] = kn[:, h * LANE:(h + 1) * LANE].astype(BF16)
        k_ref[0, :, c0 + LANE:c0 + 2 * LANE] = kr


def _mla_proj(lat, gq, gkv, wq, wk, wv, tabs, q_scale, tm=512):
    b, s, _ = lat.shape
    hq = MLA_HEADS * MLA_QK
    hv = MLA_HEADS * MLA_V
    est = (2 * _nbytes((tm, LAT_W), F32) + 2 * _nbytes(wq.shape, BF16) + 2 * _nbytes(wk.shape, BF16)
           + 2 * _nbytes(wv.shape, BF16) + 4 * _nbytes((tm, hq), BF16) + 2 * _nbytes((tm, hv), BF16)
           + 3 * _nbytes((tm, hq), F32))
    full = lambda shape: pl.BlockSpec(shape, lambda bi, i: (0,) * len(shape))
    tab = pl.BlockSpec((tm, LANE), lambda bi, i: (i, 0))
    return pl.pallas_call(
        functools.partial(_mla_proj_kernel, q_scale=q_scale),
        name="mla_proj",
        grid=(b, s // tm),
        in_specs=[
            pl.BlockSpec((1, tm, LAT_W), lambda bi, i: (bi, i, 0)),
            full((1, Q_LORA)), full((1, KV_LORA)), full(wq.shape), full(wk.shape), full(wv.shape),
            tab, tab, tab,
        ],
        out_specs=[
            pl.BlockSpec((1, tm, hq), lambda bi, i: (bi, i, 0)),
            pl.BlockSpec((1, tm, hq), lambda bi, i: (bi, i, 0)),
            pl.BlockSpec((1, tm, hv), lambda bi, i: (bi, i, 0)),
        ],
        out_shape=[
            jax.ShapeDtypeStruct((b, s, hq), BF16),
            jax.ShapeDtypeStruct((b, s, hq), BF16),
            jax.ShapeDtypeStruct((b, s, hv), BF16),
        ],
        compiler_params=_cparams(("parallel", "parallel"), est),
    )(lat, gq.reshape(1, -1), gkv.reshape(1, -1), wq, wk, wv, *tabs)


def _mla_attn_kernel(q_ref, k_ref, v_ref, o_ref, *, chain_rows):
    for r0 in range(0, q_ref.shape[1], chain_rows):
        rs = slice(r0, r0 + chain_rows)
        s = lax.dot_general(q_ref[0, rs, :], k_ref[0], (((1,), (1,)), ((), ())), preferred_element_type=F32)
        m = jnp.max(s, axis=-1, keepdims=True)
        p = jnp.exp2(s - m)
        l = jnp.sum(p, axis=-1, keepdims=True)
        o = jnp.dot(p.astype(BF16), v_ref[0], preferred_element_type=F32)
        o_ref[0, rs, :] = (o / l).astype(BF16)


def _mla_attn(q, k, v, tq=1024, chain_rows=256):
    b, s, _ = q.shape
    est = (2 * _nbytes((tq, MLA_QK), BF16) + 2 * _nbytes((s, MLA_QK), BF16) + 2 * _nbytes((s, MLA_V), BF16)
           + 2 * _nbytes((tq, MLA_V), BF16) + 3 * _nbytes((tq, s), F32))
    return pl.pallas_call(
        functools.partial(_mla_attn_kernel, chain_rows=chain_rows),
        name="mla_attn",
        grid=(b, MLA_HEADS, s // tq),
        in_specs=[
            pl.BlockSpec((1, tq, MLA_QK), lambda bi, h, i: (bi, i, h)),
            pl.BlockSpec((1, s, MLA_QK), lambda bi, h, i: (bi, 0, h)),
            pl.BlockSpec((1, s, MLA_V), lambda bi, h, i: (bi, 0, h)),
        ],
        out_specs=pl.BlockSpec((1, tq, MLA_V), lambda bi, h, i: (bi, i, h)),
        out_shape=jax.ShapeDtypeStruct((b, s, MLA_HEADS * MLA_V), BF16),
        compiler_params=_cparams(("parallel", "parallel", "parallel"), est),
    )(q, k, v)


def _gqa_kernel(q_ref, k_ref, v_ref, sink_ref, cos_ref, sup_ref, sdn_ref, o_ref, k_scr, *, scale):
    half = GQA_ROT // 2
    s_len = k_scr.shape[0]
    k_scr[...] = _rot_lanes(k_ref[0].astype(F32), cos_ref[...], sup_ref[...], sdn_ref[...], half).astype(BF16)
    sink = sink_ref[0]
    rows = GQA_GROUP * BAND
    rel = (lax.broadcasted_iota(jnp.int32, (rows, 3 * BAND), 1) - BAND
           - (lax.broadcasted_iota(jnp.int32, (rows, 3 * BAND), 0) & (BAND - 1)))
    bias = jnp.where(jnp.abs(rel) <= WINDOW, 0.0, NEG_INF)
    for n in range(s_len // BAND):
        r0 = n * BAND
        cos_t, sin_up, sin_dn = cos_ref[r0:r0 + BAND, :], sup_ref[r0:r0 + BAND, :], sdn_ref[r0:r0 + BAND, :]
        qn = q_ref[0, r0:r0 + BAND, :].astype(F32)
        q_st = jnp.concatenate(
            [_rot_lanes(qn[:, g * LANE:(g + 1) * LANE], cos_t, sin_up, sin_dn, half) for g in range(GQA_GROUP)],
            axis=0)
        q_st = (q_st * scale).astype(BF16)
        lo, hi = max(0, r0 - BAND), min(s_len, r0 + 2 * BAND)
        c0 = lo - (r0 - BAND)
        s = lax.dot_general(q_st, k_scr[lo:hi, :], (((1,), (1,)), ((), ())), preferred_element_type=F32)
        s = s + bias[:, c0:c0 + hi - lo]
        m = jnp.maximum(jnp.max(s, axis=-1, keepdims=True), sink)
        p = jnp.exp(s - m)
        l = jnp.sum(p, axis=-1, keepdims=True) + jnp.exp(sink - m)
        o = jnp.dot(p.astype(BF16), v_ref[0, lo:hi, :], preferred_element_type=F32) / l
        for g in range(GQA_GROUP):
            o_ref[0, r0:r0 + BAND, g * LANE:(g + 1) * LANE] = o[g * BAND:(g + 1) * BAND, :].astype(BF16)


def _gqa_attn(qkv, sinks, tabs):
    b, s, _ = qkv.shape
    gw = GQA_GROUP * GQA_HEAD_DIM
    sink_col = jnp.broadcast_to(sinks.astype(F32).reshape(GQA_KV_HEADS, GQA_GROUP, 1, 1),
                                (GQA_KV_HEADS, GQA_GROUP, BAND, 1)).reshape(GQA_KV_HEADS, GQA_GROUP * BAND, 1)
    est = (4 * _nbytes((s, gw), BF16) + 5 * _nbytes((s, LANE), BF16) + 6 * _nbytes((s, LANE), F32)
           + 8 * _nbytes((GQA_GROUP * BAND, 3 * BAND), F32))
    tab = pl.BlockSpec((s, LANE), lambda bi, h: (0, 0))
    return pl.pallas_call(
        functools.partial(_gqa_kernel, scale=GQA_HEAD_DIM ** -0.5),
        name="gqa_window_attn",
        grid=(b, GQA_KV_HEADS),
        in_specs=[
            pl.BlockSpec((1, s, gw), lambda bi, h: (bi, 0, h)),
            pl.BlockSpec((1, s, LANE), lambda bi, h: (bi, 0, GQA_HEADS + h)),
            pl.BlockSpec((1, s, LANE), lambda bi, h: (bi, 0, GQA_HEADS + GQA_KV_HEADS + h)),
            pl.BlockSpec((1, GQA_GROUP * BAND, 1), lambda bi, h: (h, 0, 0)),
            tab, tab, tab,
        ],
        out_specs=pl.BlockSpec((1, s, gw), lambda bi, h: (bi, 0, h)),
        out_shape=jax.ShapeDtypeStruct((b, s, GQA_HEADS * GQA_HEAD_DIM), BF16),
        scratch_shapes=[pltpu.VMEM((s, LANE), BF16)],
        compiler_params=_cparams(("parallel", "parallel"), est),
    )(qkv, qkv, qkv, sink_col, *tabs)


def _short_conv_kernel(x_ref, w_ref, b_ref, o_ref):
    x = x_ref[0].astype(F32)
    s_len = x.shape[0]
    row = lax.broadcasted_iota(jnp.int32, x.shape, 0)
    prev = jnp.where(row == 0, 0.0, pltpu.roll(x, 1, 0))
    nxt = jnp.where(row == s_len - 1, 0.0, pltpu.roll(x, s_len - 1, 0))
    y = b_ref[...] + prev * w_ref[0:1, :]
    y = y + x * w_ref[1:2, :]
    y = y + nxt * w_ref[2:3, :]
    o_ref[0] = y.astype(BF16)


def _short_conv(hy, w, bias, tc=256):
    b, s, c = hy.shape
    est = 4 * _nbytes((s, tc), BF16) + 6 * _nbytes((s, tc), F32)
    return pl.pallas_call(
        _short_conv_kernel,
        name="hyena_short_conv",
        grid=(b, c // tc),
        in_specs=[
            pl.BlockSpec((1, s, tc), lambda bi, j: (bi, 0, j)),
            pl.BlockSpec((HY_SHORT, tc), lambda bi, j: (0, j)),
            pl.BlockSpec((1, tc), lambda bi, j: (0, j)),
        ],
        out_specs=pl.BlockSpec((1, s, tc), lambda bi, j: (bi, 0, j)),
        out_shape=jax.ShapeDtypeStruct((b, s, c), BF16),
        compiler_params=_cparams(("parallel", "parallel"), est),
    )(hy, w, bias.reshape(1, c))


def _filter_mlp_kernel(z_ref, w1_ref, b1_ref, w2_ref, b2_ref, w3_ref, b3_ref, fr_ref, h_ref):
    dot = functools.partial(jnp.dot, preferred_element_type=F32, precision=HIGHEST)
    fr = fr_ref[...]
    h = jnp.sin(fr * (dot(z_ref[...], w1_ref[...]) + b1_ref[...]))
    h = jnp.sin(fr * (dot(h, w2_ref[...]) + b2_ref[...]))
    h_ref[...] = jnp.sin(fr * (dot(h, w3_ref[...]) + b3_ref[...]))


def _filter_mlp(z, w1, b1, w2, b2, w3, b3, freq):
    n_lag = z.shape[0]
    full = lambda shape: pl.BlockSpec(shape, lambda i: (0,) * len(shape))
    est = 12 * _nbytes((n_lag, LANE), F32)
    return pl.pallas_call(
        _filter_mlp_kernel,
        name="hyena_filter_mlp",
        grid=(1,),
        in_specs=[full(z.shape), full(w1.shape), full((1, HY_FFN)), full(w2.shape), full((1, HY_FFN)),
                  full(w3.shape), full((1, HY_FFN)), full((1, HY_FFN))],
        out_specs=full((n_lag, HY_FFN)),
        out_shape=jax.ShapeDtypeStruct((n_lag, HY_FFN), F32),
        compiler_params=_cparams(("arbitrary",), est),
    )(z, w1, b1.reshape(1, -1), w2, b2.reshape(1, -1), w3, b3.reshape(1, -1), freq.reshape(1, -1))


def _filter_gen_kernel(h_ref, w4f_ref, w4b_ref, kf_ref, kb_ref):
    dot = functools.partial(jnp.dot, preferred_element_type=F32, precision=HIGHEST)
    h = h_ref[...]
    hf = dot(h, w4f_ref[...])
    hb = dot(h, w4b_ref[...])
    n_lag, tn = hf.shape
    row = lax.broadcasted_iota(jnp.int32, (n_lag, tn), 0)
    col = pl.program_id(1) * tn + lax.broadcasted_iota(jnp.int32, (1, tn), 1)
    t = row.astype(F32) / (n_lag - 1)
    max_decay = math.log(HY_TARGET) / HY_DECAY_PCT_SHORT
    min_decay = math.log(HY_TARGET) / HY_DECAY_PCT_LONG
    delta = min_decay + (max_decay - min_decay) * (col.astype(F32) / (HY_D - 1))
    decay = jnp.exp(-t * jnp.abs(delta))
    kf = hf * decay
    kb = jnp.where(row == 0, 0.0, hb * decay)
    inv = 1.0 / (jnp.sum(jnp.abs(kf), axis=0, keepdims=True) + jnp.sum(jnp.abs(kb), axis=0, keepdims=True))
    kf_ref[0] = (kf * inv).astype(BF16)
    kb_ref[0] = (kb * inv).astype(BF16)


def _filter_gen(h, w4, tn=256):
    n_lag = h.shape[0]
    nblk = HY_D // tn
    est = 10 * _nbytes((n_lag, tn), F32) + 4 * _nbytes((n_lag, LANE), F32) + 4 * _nbytes((n_lag, tn), BF16)
    out = jax.ShapeDtypeStruct((HY_ORDER, n_lag, HY_D), BF16)
    return pl.pallas_call(
        _filter_gen_kernel,
        name="hyena_filter_gen",
        grid=(HY_ORDER, nblk),
        in_specs=[
            pl.BlockSpec((n_lag, HY_FFN), lambda o, j: (0, 0)),
            pl.BlockSpec((HY_FFN, tn), lambda o, j: (0, 2 * o * nblk + j)),
            pl.BlockSpec((HY_FFN, tn), lambda o, j: (0, (2 * o + 1) * nblk + j)),
        ],
        out_specs=[pl.BlockSpec((1, n_lag, tn), lambda o, j: (o, 0, j))] * 2,
        out_shape=[out, out],
        compiler_params=_cparams(("parallel", "parallel"), est),
    )(h, w4, w4)


def _filter_dft_kernel(ct_ref, st_ref, kf_ref, kb_ref, a_ref, bq_ref, a2_ref):
    dot = functools.partial(jnp.dot, preferred_element_type=F32)
    ct, st = ct_ref[...], st_ref[...]
    kf, kb = kf_ref[0], kb_ref[0]
    fc, fs, bc, bs = dot(ct, kf), dot(st, kf), dot(ct, kb), dot(st, kb)
    tf = ct.shape[0]
    row = pl.program_id(2) * tf + lax.broadcasted_iota(jnp.int32, fc.shape, 0)
    is0 = row == 0
    wgt = jnp.where(is0, 1.0 / DFT_N, 2.0 / DFT_N)
    a = (fc + bc) * wgt
    a_ref[0] = a
    bq_ref[0] = jnp.where(is0, 0.0, (bs - fs) * wgt)
    a2_ref[0] = jnp.where(is0, (fs + bs) * wgt, a)


def _filter_dft(ct, st, kf, kb, tf=512, tn=512):
    n_f, n_s = ct.shape
    est = (4 * _nbytes((tf, n_s), BF16) + 4 * _nbytes((n_s, tn), BF16) + 6 * _nbytes((tf, tn), F32)
           + 8 * _nbytes((tf, tn), F32))
    out = jax.ShapeDtypeStruct((HY_ORDER, n_f, HY_D), F32)
    return pl.pallas_call(
        _filter_dft_kernel,
        name="hyena_filter_dft",
        grid=(HY_ORDER, HY_D // tn, n_f // tf),
        in_specs=[
            pl.BlockSpec((tf, n_s), lambda o, j, k: (k, 0)),
            pl.BlockSpec((tf, n_s), lambda o, j, k: (k, 0)),
            pl.BlockSpec((1, n_s, tn), lambda o, j, k: (o, 0, j)),
            pl.BlockSpec((1, n_s, tn), lambda o, j, k: (o, 0, j)),
        ],
        out_specs=[pl.BlockSpec((1, tf, tn), lambda o, j, k: (o, k, j))] * 3,
        out_shape=[out, out, out],
        compiler_params=_cparams(("parallel", "parallel", "parallel"), est),
    )(ct, st, kf, kb)


def _conv_fwd_kernel(ct_ref, st_ref, u_ref, a_ref, bq_ref, a2_ref, yr_ref, ys_ref):
    u = u_ref[0]
    uc = jnp.dot(ct_ref[...], u, preferred_element_type=F32)
    us = jnp.dot(st_ref[...], u, preferred_element_type=F32)
    bq = bq_ref[0]
    yr_ref[0] = (uc * a_ref[0] + us * bq).astype(BF16)
    ys_ref[0] = (us * a2_ref[0] - uc * bq).astype(BF16)


def _conv_fwd(ct, st, u, u_blk0, spec, order, tf=512, tn=512):
    a, bq, a2 = spec
    b, s, _ = u.shape
    n_f = ct.shape[0]
    est = (4 * _nbytes((tf, s), BF16) + 2 * _nbytes((s, tn), BF16) + 6 * _nbytes((tf, tn), F32)
           + 4 * _nbytes((tf, tn), BF16) + 6 * _nbytes((tf, tn), F32))
    coef = pl.BlockSpec((1, tf, tn), lambda bi, j, k: (order, k, j))
    out = jax.ShapeDtypeStruct((b, n_f, HY_D), BF16)
    return pl.pallas_call(
        _conv_fwd_kernel,
        name="hyena_conv_fwd",
        grid=(b, HY_D // tn, n_f // tf),
        in_specs=[
            pl.BlockSpec((tf, s), lambda bi, j, k: (k, 0)),
            pl.BlockSpec((tf, s), lambda bi, j, k: (k, 0)),
            pl.BlockSpec((1, s, tn), lambda bi, j, k: (bi, 0, u_blk0 * (HY_D // tn) + j)),
            coef, coef, coef,
        ],
        out_specs=[pl.BlockSpec((1, tf, tn), lambda bi, j, k: (bi, k, j))] * 2,
        out_shape=[out, out],
        compiler_params=_cparams(("parallel", "parallel", "parallel"), est),
    )(ct, st, u, a, bq, a2)


def _conv_inv_kernel(ct_ref, stt_ref, yr_ref, ys_ref, g_ref, u_ref, skip_ref, o_ref):
    y = jnp.dot(ct_ref[...], yr_ref[0], preferred_element_type=F32)
    y = y + jnp.dot(stt_ref[...], ys_ref[0], preferred_element_type=F32)
    u = u_ref[0].astype(F32)
    o_ref[0] = (g_ref[0].astype(F32) * (y + u * skip_ref[0])).astype(BF16)


def _conv_inv(ct, stt, yr, ys, gsrc, g_blk0, usrc, u_blk0, skip, order, tt=512, tn=512):
    b, n_f, _ = yr.shape
    s = ct.shape[0]
    nblk = HY_D // tn
    est = (4 * _nbytes((tt, n_f), BF16) + 4 * _nbytes((n_f, tn), BF16) + 6 * _nbytes((tt, tn), BF16)
           + 6 * _nbytes((tt, tn), F32))
    return pl.pallas_call(
        _conv_inv_kernel,
        name="hyena_conv_inv",
        grid=(b, nblk, s // tt),
        in_specs=[
            pl.BlockSpec((tt, n_f), lambda bi, j, k: (k, 0)),
            pl.BlockSpec((tt, n_f), lambda bi, j, k: (k, 0)),
            pl.BlockSpec((1, n_f, tn), lambda bi, j, k: (bi, 0, j)),
            pl.BlockSpec((1, n_f, tn), lambda bi, j, k: (bi, 0, j)),
            pl.BlockSpec((1, tt, tn), lambda bi, j, k: (bi, k, g_blk0 * nblk + j)),
            pl.BlockSpec((1, tt, tn), lambda bi, j, k: (bi, k, u_blk0 * nblk + j)),
            pl.BlockSpec((1, 1, tn), lambda bi, j, k: (order, 0, j)),
        ],
        out_specs=pl.BlockSpec((1, tt, tn), lambda bi, j, k: (bi, k, j)),
        out_shape=jax.ShapeDtypeStruct((b, s, HY_D), BF16),
        compiler_params=_cparams(("parallel", "parallel", "parallel"), est),
    )(ct, stt, yr, ys, gsrc, usrc, skip.reshape(HY_ORDER, 1, HY_D))


def _rope_tables(seq, rot_dim):
    half = rot_dim // 2
    pos = jnp.arange(seq, dtype=F32)
    inv = ROPE_THETA ** (-jnp.arange(0, rot_dim, 2, dtype=F32) / rot_dim)
    ang = pos[:, None] * inv[None, :]
    cos, sin = jnp.cos(ang), jnp.sin(ang)
    rest = LANE - rot_dim
    return cos, sin, half, rest


def _rope_lane_tables(seq, rot_dim, rest_passthrough):
    cos, sin, half, rest = _rope_tables(seq, rot_dim)
    fill = jnp.ones((seq, rest), F32) if rest_passthrough else jnp.zeros((seq, rest), F32)
    zero_h = jnp.zeros((seq, half), F32)
    zero_r = jnp.zeros((seq, rest), F32)
    cos_t = jnp.concatenate([cos, cos, fill], axis=1)
    sin_up = jnp.concatenate([zero_h, sin, zero_r], axis=1)
    sin_dn = jnp.concatenate([-sin, zero_h, zero_r], axis=1)
    return cos_t, sin_up, sin_dn


def _dft_tables(n_half):
    n = 2 * n_half
    f = jnp.arange(n_half, dtype=jnp.int32)[:, None]
    s = jnp.arange(n_half, dtype=jnp.int32)[None, :]
    ang = ((f * s) & (n - 1)).astype(F32) * (2.0 * math.pi / n)
    ct = jnp.cos(ang)
    alt = (1 - 2 * (s & 1)).astype(F32)
    st = jnp.where(f == 0, alt, jnp.sin(ang))
    return ct.astype(BF16), st.astype(BF16), st.T.astype(BF16)


def _filter_features(n_lag):
    t = jnp.linspace(0.0, 1.0, n_lag, dtype=F32)[:, None]
    bands = (HY_EMB - 1) // 2
    wpos = 2.0 * math.pi * jnp.arange(n_lag, dtype=F32) / n_lag
    fb = jnp.linspace(1e-4, bands - 1, bands, dtype=F32)
    fw = wpos[:, None] * fb[None, :]
    z = jnp.concatenate([t, jnp.cos(fw), -jnp.sin(fw)], axis=-1)
    return jnp.pad(z, ((0, 0), (0, LANE - HY_EMB)))


def kernel(x, c, ada_mix_w, ada_mix_b, norm_mix_g, ada_mlp_w, ada_mlp_b, norm_mlp_g, w_mlp_in, w_mlp_out, e_w_in, e_q_norm_g, e_kv_norm_g, e_w_uq, e_w_ukv, e_conv_w, e_conv_b, e_f_w1, e_f_b1, e_f_w2, e_f_b2, e_f_w3, e_f_b3, e_f_freq, e_f_w4, e_hy_skip, e_w_out, o_w_qkv, o_sinks, o_w_o, final_norm_g):
    b, s, d = x.shape
    c8 = jnp.pad(c, ((0, 8 - b), (0, 0)))
    mod_mix = _ada(c8, ada_mix_w, ada_mix_b)[:, :b].reshape(DEPTH, b, 1, 3 * d)
    mod_mlp = _ada(c8, ada_mlp_w, ada_mlp_b)[:, :b].reshape(DEPTH, b, 1, 3 * d)

    mla_tabs = _rope_lane_tables(s, MLA_ROPE, rest_passthrough=False)
    gqa_tabs = _rope_lane_tables(s, GQA_ROT, rest_passthrough=True)
    ct, st, stt = _dft_tables(s)
    z_feat = _filter_features(s)
    q_scale = (MLA_NOPE + MLA_ROPE) ** -0.5 * math.log2(math.e)

    for l in range(DEPTH):
        i = l // 2
        if l % 2 == 0:
            w_in = e_w_in[i]
            w_lat = jnp.pad(w_in[:, :Q_LORA + KV_LORA + MLA_ROPE], ((0, 0), (0, LANE - MLA_ROPE)))
            w_hy = w_in[:, Q_LORA + KV_LORA + MLA_ROPE:]
            lat = _normmod_matmul(x, mod_mix[l], norm_mix_g[l], w_lat[None], 0, F32, tn=LAT_W)
            hy_in = _normmod_matmul(x, mod_mix[l], norm_mix_g[l], w_hy[None], 0, BF16)

            w_uq = e_w_uq[i].reshape(Q_LORA, MLA_HEADS, MLA_NOPE + MLA_ROPE)
            wq = jnp.concatenate([
                w_uq[:, :, :MLA_NOPE].reshape(Q_LORA, -1),
                jnp.pad(w_uq[:, :, MLA_NOPE:], ((0, 0), (0, 0), (0, LANE - MLA_ROPE))).reshape(Q_LORA, -1),
            ], axis=1).astype(BF16)
            w_ukv = e_w_ukv[i].reshape(KV_LORA, MLA_HEADS, MLA_NOPE + MLA_V)
            wk = w_ukv[:, :, :MLA_NOPE].reshape(KV_LORA, -1).astype(BF16)
            wv = w_ukv[:, :, MLA_NOPE:].reshape(KV_LORA, -1).astype(BF16)
            q, k, v = _mla_proj(lat, e_q_norm_g[i], e_kv_norm_g[i], wq, wk, wv, mla_tabs, q_scale)
            a_mla = _mla_attn(q, k, v)

            w1 = jnp.pad(e_f_w1[i], ((0, LANE - HY_EMB), (0, 0)))
            h_filt = _filter_mlp(z_feat, w1, e_f_b1[i], e_f_w2[i], e_f_b2[i], e_f_w3[i], e_f_b3[i], e_f_freq[i])
            kf, kb = _filter_gen(h_filt, e_f_w4[i])
            spec = _filter_dft(ct, st, kf, kb)
            u = _short_conv(hy_in, e_conv_w[i], e_conv_b[i])
            yr, ys = _conv_fwd(ct, st, u, 0, spec, 0)
            zc = _conv_inv(ct, stt, yr, ys, u, 1, u, 0, e_hy_skip[i], 0)
            yr, ys = _conv_fwd(ct, st, zc, 0, spec, 1)
            b_hy = _conv_inv(ct, stt, yr, ys, u, 2, zc, 0, e_hy_skip[i], 1)

            x = _proj_res([a_mla, b_hy], e_w_out, i, x, mod_mix[l])
        else:
            qkv = _normmod_matmul(x, mod_mix[l], norm_mix_g[l], o_w_qkv, i, BF16)
            o = _gqa_attn(qkv, o_sinks[i], gqa_tabs)
            x = _proj_res([o], o_w_o, i, x, mod_mix[l])
        x = _mlp(x, mod_mlp[l], norm_mlp_g[l], w_mlp_in, w_mlp_out, l, final_norm_g, l == DEPTH - 1)
    return x
```

```python
import functools
import math

import jax
import jax.numpy as jnp
from jax import lax
from jax.experimental import pallas as pl
from jax.experimental.pallas import tpu as pltpu

F32 = jnp.float32
BF16 = jnp.bfloat16
HIGHEST = lax.Precision.HIGHEST

D_MODEL = 2048
BATCH = 4
SEQ = 2048
DEPTH = 4
RMS_EPS = 1e-6
ROPE_THETA = 500000.0
NEG_INF = -1e30
MLA_HEADS = 8
MLA_NOPE = 128
MLA_ROPE = 64
MLA_V = 128
Q_LORA = 512
KV_LORA = 256
HY_D = 1024
HY_ORDER = 2
HY_SHORT = 3
HY_EMB = 33
HY_FFN = 64
HY_DECAY_PCT_SHORT = 0.3
HY_DECAY_PCT_LONG = 1.5
HY_TARGET = 1e-2
GQA_HEADS = 16
GQA_KV_HEADS = 4
GQA_HEAD_DIM = 128
GQA_ROT = GQA_HEAD_DIM // 4
GQA_GROUP = GQA_HEADS // GQA_KV_HEADS
WINDOW = 128
BAND = 128
D_FF = 4 * D_MODEL
LAT_W = Q_LORA + KV_LORA + 128
MLA_QK = 256
DFT_N = 2 * SEQ

LANE = 128
V7X_VMEM_BYTES = 64 * 1024 * 1024
V7X_VMEM_BUDGET = 56 * 1024 * 1024


def _cparams(semantics, est_bytes):
    limit = int(min(V7X_VMEM_BUDGET, max(32 * 1024 * 1024, est_bytes * 3 // 2)))
    return pltpu.CompilerParams(dimension_semantics=semantics, vmem_limit_bytes=limit)


def _nbytes(shape, dtype):
    return math.prod(shape) * jnp.dtype(dtype).itemsize


def _rms(x, g):
    ms = jnp.mean(x * x, axis=-1, keepdims=True)
    return x * lax.rsqrt(ms + RMS_EPS) * g


def _ada_kernel(c_ref, w_ref, b_ref, o_ref):
    cv = c_ref[...]
    s = cv * (1.0 / (1.0 + jnp.exp(-cv)))
    o_ref[0] = jnp.dot(s.astype(BF16), w_ref[0].astype(BF16), preferred_element_type=F32) + b_ref[0]


def _ada(c8, w, b, tn=512):
    n_l, d, n = w.shape
    est = 2 * _nbytes((d, tn), F32) + 4 * _nbytes((8, tn), F32) + 2 * _nbytes((8, d), F32)
    return pl.pallas_call(
        _ada_kernel,
        name="ada_modulation",
        grid=(n_l, n // tn),
        in_specs=[
            pl.BlockSpec((8, d), lambda l, j: (0, 0)),
            pl.BlockSpec((1, d, tn), lambda l, j: (l, 0, j)),
            pl.BlockSpec((1, 1, tn), lambda l, j: (l, 0, j)),
        ],
        out_specs=pl.BlockSpec((1, 8, tn), lambda l, j: (l, 0, j)),
        out_shape=jax.ShapeDtypeStruct((n_l, 8, n), F32),
        compiler_params=_cparams(("parallel", "parallel"), est),
    )(c8, w, b.reshape(n_l, 1, n))


def _normmod_rows(x_ref, shift_ref, scale_ref, g_ref, h_ref, rows):
    gain = g_ref[...] * (1.0 + scale_ref[0])
    sh = shift_ref[0]
    tm = h_ref.shape[0]

    def body(r, carry):
        sl = pl.ds(pl.multiple_of(r * rows, rows), rows)
        xv = x_ref[0, sl, :]
        inv = lax.rsqrt(jnp.mean(xv * xv, axis=-1, keepdims=True) + RMS_EPS)
        h_ref[sl, :] = (xv * inv * gain + sh).astype(BF16)
        return carry

    lax.fori_loop(0, tm // rows, body, 0)


def _normmod_kernel(x_ref, shift_ref, scale_ref, g_ref, h_ref):
    _normmod_rows(x_ref, shift_ref, scale_ref, g_ref, h_ref.at[0], 128)


def _normmod(x, mod, g, tm=512):
    b, s, d = x.shape
    est = 2 * _nbytes((tm, d), F32) + 2 * _nbytes((tm, d), BF16) + 8 * _nbytes((128, d), F32)
    return pl.pallas_call(
        _normmod_kernel,
        name="normmod",
        grid=(b, s // tm),
        in_specs=[
            pl.BlockSpec((1, tm, d), lambda bi, i: (bi, i, 0)),
            pl.BlockSpec((1, 1, d), lambda bi, i: (bi, 0, 0)),
            pl.BlockSpec((1, 1, d), lambda bi, i: (bi, 0, 1)),
            pl.BlockSpec((1, d), lambda bi, i: (0, 0)),
        ],
        out_specs=pl.BlockSpec((1, tm, d), lambda bi, i: (bi, i, 0)),
        out_shape=jax.ShapeDtypeStruct((b, s, d), BF16),
        compiler_params=_cparams(("parallel", "parallel"), est),
    )(x, mod, mod, g.reshape(1, d))


def _matmul_kernel(h_ref, w_ref, o_ref):
    o_ref[0] = jnp.dot(h_ref[0], w_ref[...].astype(BF16), preferred_element_type=F32).astype(o_ref.dtype)


def _matmul(h, w_all, layer, out_dtype, tm=2048, tn=512):
    b, s, d = h.shape
    n = w_all.shape[2]
    tn = min(tn, n)
    est = (2 * _nbytes((tm, d), BF16) + 2 * _nbytes((d, tn), F32) + _nbytes((d, tn), BF16)
           + 3 * _nbytes((tm, tn), F32))
    return pl.pallas_call(
        _matmul_kernel,
        name="matmul",
        grid=(b, s // tm, n // tn),
        in_specs=[
            pl.BlockSpec((1, tm, d), lambda bi, i, j: (bi, i, 0)),
            pl.BlockSpec((None, d, tn), lambda bi, i, j: (layer, 0, j)),
        ],
        out_specs=pl.BlockSpec((1, tm, tn), lambda bi, i, j: (bi, i, j)),
        out_shape=jax.ShapeDtypeStruct((b, s, n), out_dtype),
        compiler_params=_cparams(("parallel", "parallel", "parallel"), est),
    )(h, w_all)


def _mlp_kernel(x_ref, shift_ref, scale_ref, gate_ref, g_ref, w1_ref, w2_ref, fg_ref, o_ref, h_ref, *, final_norm):
    f = pl.program_id(2)

    @pl.when(f == 0)
    def _():
        _normmod_rows(x_ref, shift_ref, scale_ref, g_ref, h_ref, 128)

        o_ref[...] = jnp.zeros_like(o_ref)

    a = jnp.dot(h_ref[...], w1_ref[...].astype(BF16), preferred_element_type=F32)
    a = jnp.square(jnp.maximum(a, 0.0)).astype(BF16)
    o_ref[0] += jnp.dot(a, w2_ref[...].astype(BF16), preferred_element_type=F32)

    @pl.when(f == pl.num_programs(2) - 1)
    def _():
        gate = gate_ref[0]
        fg = fg_ref[...]
        rows = 128

        def body(r, carry):
            sl = pl.ds(pl.multiple_of(r * rows, rows), rows)
            y = x_ref[0, sl, :] + gate * o_ref[0, sl, :]
            if final_norm:
                y = _rms(y, fg)
            o_ref[0, sl, :] = y
            return carry

        lax.fori_loop(0, o_ref.shape[1] // rows, body, 0)


def _mlp(x, mod, g, w1_all, w2_all, layer, final_g, final_norm, tm=1024, tf=512):
    b, s, d = x.shape
    ff = w1_all.shape[2]
    est = (3 * _nbytes((tm, d), F32) + _nbytes((tm, d), BF16) + 4 * _nbytes((d, tf), F32)
           + 2 * _nbytes((d, tf), BF16) + 2 * _nbytes((tm, tf), F32))
    return pl.pallas_call(
        functools.partial(_mlp_kernel, final_norm=final_norm),
        name="mlp_relu2",
        grid=(b, s // tm, ff // tf),
        in_specs=[
            pl.BlockSpec((1, tm, d), lambda bi, i, f: (bi, i, 0), pipeline_mode=pl.Buffered(1)),
            pl.BlockSpec((1, 1, d), lambda bi, i, f: (bi, 0, 0)),
            pl.BlockSpec((1, 1, d), lambda bi, i, f: (bi, 0, 1)),
            pl.BlockSpec((1, 1, d), lambda bi, i, f: (bi, 0, 2)),
            pl.BlockSpec((1, d), lambda bi, i, f: (0, 0)),
            pl.BlockSpec((None, d, tf), lambda bi, i, f: (layer, 0, f)),
            pl.BlockSpec((None, tf, d), lambda bi, i, f: (layer, f, 0)),
            pl.BlockSpec((1, d), lambda bi, i, f: (0, 0)),
        ],
        out_specs=pl.BlockSpec((1, tm, d), lambda bi, i, f: (bi, i, 0)),
        out_shape=jax.ShapeDtypeStruct((b, s, d), F32),
        scratch_shapes=[pltpu.VMEM((tm, d), BF16)],
        compiler_params=_cparams(("parallel", "parallel", "arbitrary"), est),
    )(x, mod, mod, mod, g.reshape(1, d), w1_all, w2_all, final_g.reshape(1, d))


def _proj_res_kernel(*refs, n_in):
    a_refs, w_refs = refs[:n_in], refs[n_in:2 * n_in]
    x_ref, gate_ref, o_ref = refs[2 * n_in:]
    acc = None
    for a_ref, w_ref in zip(a_refs, w_refs):
        p = jnp.dot(a_ref[0], w_ref[...].astype(BF16), preferred_element_type=F32)
        acc = p if acc is None else acc + p
    o_ref[0] = x_ref[0] + gate_ref[0] * acc


def _proj_res(a_list, w_all, layer, x, mod, tm=2048, tn=512):
    b, s, d = x.shape
    n_in = len(a_list)
    kk = a_list[0].shape[-1]
    assert all(a.shape[-1] == kk for a in a_list) and w_all.shape[1] == n_in * kk
    est = (n_in * (2 * _nbytes((tm, kk), BF16) + 2 * _nbytes((kk, tn), F32) + _nbytes((kk, tn), BF16))
           + 6 * _nbytes((tm, tn), F32))
    in_specs = [pl.BlockSpec((1, tm, kk), lambda bi, i, j: (bi, i, 0)) for _ in a_list]
    in_specs += [pl.BlockSpec((None, kk, tn), functools.partial(lambda bi, i, j, r: (layer, r, j), r=r))
                 for r in range(n_in)]
    in_specs += [
        pl.BlockSpec((1, tm, tn), lambda bi, i, j: (bi, i, j)),
        pl.BlockSpec((1, 1, tn), lambda bi, i, j: (bi, 0, 2 * (d // tn) + j)),
    ]
    return pl.pallas_call(
        functools.partial(_proj_res_kernel, n_in=n_in),
        name="proj_residual",
        grid=(b, s // tm, d // tn),
        in_specs=in_specs,
        out_specs=pl.BlockSpec((1, tm, tn), lambda bi, i, j: (bi, i, j)),
        out_shape=jax.ShapeDtypeStruct((b, s, d), F32),
        compiler_params=_cparams(("parallel", "parallel", "parallel"), est),
    )(*a_list, *([w_all] * n_in), x, mod)


def _rot_lanes(blk, cos_t, sin_up, sin_dn, half):
    return (blk * cos_t + pltpu.roll(blk, half, 1) * sin_up + pltpu.roll(blk, LANE - half, 1) * sin_dn)


def _mla_proj_kernel(lat_ref, gq_ref, gkv_ref, wq_ref, wk_ref, wv_ref, cos_ref, sup_ref, sdn_ref,
                     q_ref, k_ref, v_ref, *, q_scale):
    lat = lat_ref[0]
    qn = _rms(lat[:, :Q_LORA], gq_ref[...]).astype(BF16)
    kvn = _rms(lat[:, Q_LORA:Q_LORA + KV_LORA], gkv_ref[...]).astype(BF16)
    cos_t, sin_up, sin_dn = cos_ref[...], sup_ref[...], sdn_ref[...]
    half = MLA_ROPE // 2
    q = jnp.dot(qn, wq_ref[...], preferred_element_type=F32) * q_scale
    kn = jnp.dot(kvn, wk_ref[...], preferred_element_type=F32)
    v = jnp.dot(kvn, wv_ref[...], preferred_element_type=F32).astype(BF16)
    ones = jnp.ones((v.shape[0], MLA_V), BF16)
    kr = _rot_lanes(lat[:, Q_LORA + KV_LORA:], cos_t, sin_up, sin_dn, half).astype(BF16)
    nope_w = MLA_HEADS * MLA_NOPE
    for h in range(MLA_HEADS):
        c0 = h * MLA_QK
        v_ref[0, :, 2 * h * MLA_V:(2 * h + 1) * MLA_V] = v[:, h * MLA_V:(h + 1) * MLA_V]
        v_ref[0, :, (2 * h + 1) * MLA_V:(2 * h + 2) * MLA_V] = ones
        q_ref[0, :, c0:c0 + LANE] = q[:, h * LANE:(h + 1) * LANE].astype(BF16)
        qr = q[:, nope_w + h * LANE:nope_w + (h + 1) * LANE]
        q_ref[0, :, c0 + LANE:c0 + 2 * LANE] = _rot_lanes(qr, cos_t, sin_up, sin_dn, half).astype(BF16)
        k_ref[0, :, c0:c0 + LANE] = kn[:, h * LANE:(h + 1) * LANE].astype(BF16)
        k_ref[0, :, c0 + LANE:c0 + 2 * LANE] = kr


def _mla_proj(lat, gq, gkv, wq, wk, wv, tabs, q_scale, tm=512):
    b, s, _ = lat.shape
    hq = MLA_HEADS * MLA_QK
    hv = MLA_HEADS * 2 * MLA_V
    est = (2 * _nbytes((tm, LAT_W), F32) + 2 * _nbytes(wq.shape, BF16) + 2 * _nbytes(wk.shape, BF16)
           + 2 * _nbytes(wv.shape, BF16) + 4 * _nbytes((tm, hq), BF16) + 2 * _nbytes((tm, hv), BF16)
           + 3 * _nbytes((tm, hq), F32))
    full = lambda shape: pl.BlockSpec(shape, lambda bi, i: (0,) * len(shape))
    tab = pl.BlockSpec((tm, LANE), lambda bi, i: (i, 0))
    return pl.pallas_call(
        functools.partial(_mla_proj_kernel, q_scale=q_scale),
        name="mla_proj",
        grid=(b, s // tm),
        in_specs=[
            pl.BlockSpec((1, tm, LAT_W), lambda bi, i: (bi, i, 0)),
            full((1, Q_LORA)), full((1, KV_LORA)), full(wq.shape), full(wk.shape), full(wv.shape),
            tab, tab, tab,
        ],
        out_specs=[
            pl.BlockSpec((1, tm, hq), lambda bi, i: (bi, i, 0)),
            pl.BlockSpec((1, tm, hq), lambda bi, i: (bi, i, 0)),
            pl.BlockSpec((1, tm, hv), lambda bi, i: (bi, i, 0)),
        ],
        out_shape=[
            jax.ShapeDtypeStruct((b, s, hq), BF16),
            jax.ShapeDtypeStruct((b, s, hq), BF16),
            jax.ShapeDtypeStruct((b, s, hv), BF16),
        ],
        compiler_params=_cparams(("parallel", "parallel"), est),
    )(lat, gq.reshape(1, -1), gkv.reshape(1, -1), wq, wk, wv, *tabs)


def _mla_attn_kernel(q_ref, k_ref, v_ref, o_ref, *, chain_rows):
    n_chains = q_ref.shape[1] // chain_rows

    def scores(c):
        rs = slice(c * chain_rows, (c + 1) * chain_rows)
        return lax.dot_general(q_ref[0, rs, :], k_ref[0], (((1,), (1,)), ((), ())), preferred_element_type=F32)

    s_next = scores(0)
    for c in range(n_chains):
        s = s_next
        if c + 1 < n_chains:
            s_next = scores(c + 1)
        m = jnp.max(s, axis=-1, keepdims=True)
        p = jnp.exp2(s - m).astype(BF16)
        oe = jnp.dot(p, v_ref[0], preferred_element_type=F32)
        o_ref[0, c * chain_rows:(c + 1) * chain_rows, :] = (oe[:, :MLA_V] / oe[:, MLA_V:]).astype(BF16)


def _mla_attn(q, k, v, tq=2048, chain_rows=1024):
    b, s, _ = q.shape
    est = (2 * _nbytes((tq, MLA_QK), BF16) + 2 * _nbytes((s, MLA_QK), BF16) + 2 * _nbytes((s, 2 * MLA_V), BF16)
           + 2 * _nbytes((tq, MLA_V), BF16) + 5 * _nbytes((chain_rows, s), F32))
    return pl.pallas_call(
        functools.partial(_mla_attn_kernel, chain_rows=chain_rows),
        name="mla_attn",
        grid=(b, MLA_HEADS, s // tq),
        in_specs=[
            pl.BlockSpec((1, tq, MLA_QK), lambda bi, h, i: (bi, i, h)),
            pl.BlockSpec((1, s, MLA_QK), lambda bi, h, i: (bi, 0, h)),
            pl.BlockSpec((1, s, 2 * MLA_V), lambda bi, h, i: (bi, 0, h)),
        ],
        out_specs=pl.BlockSpec((1, tq, MLA_V), lambda bi, h, i: (bi, i, h)),
        out_shape=jax.ShapeDtypeStruct((b, s, MLA_HEADS * MLA_V), BF16),
        compiler_params=_cparams(("parallel", "parallel", "parallel"), est),
    )(q, k, v)


def _gqa_kernel(q_ref, k_ref, v_ref, sink_ref, cos_ref, sin_ref, swap_ref, o_ref, k_scr, v_scr, *, scale):
    s_len = k_scr.shape[0]
    n_blocks = s_len // BAND
    rows = GQA_GROUP * BAND
    swap = swap_ref[...]

    def rot(x, cos_t, sin_t):
        return x.astype(F32) * cos_t + jnp.dot(x, swap, preferred_element_type=F32) * sin_t

    k_scr[...] = rot(k_ref[0], cos_ref[...], sin_ref[...]).astype(BF16)
    v_scr[:, :LANE] = v_ref[0]
    v_scr[:, LANE:] = jnp.ones((s_len, LANE), BF16)
    sink = sink_ref[0]
    rel = (lax.broadcasted_iota(jnp.int32, (rows, 3 * BAND), 1) - BAND
           - (lax.broadcasted_iota(jnp.int32, (rows, 3 * BAND), 0) & (BAND - 1)))
    bias = jnp.where(jnp.abs(rel) <= WINDOW, 0.0, NEG_INF)

    def window(n):
        r0 = n * BAND
        return max(0, r0 - BAND), min(s_len, r0 + 2 * BAND)

    def scores(n):
        r0 = n * BAND
        cos_t = jnp.concatenate([cos_ref[r0:r0 + BAND, :]] * GQA_GROUP, axis=0)
        sin_t = jnp.concatenate([sin_ref[r0:r0 + BAND, :]] * GQA_GROUP, axis=0)
        q_st = jnp.concatenate([q_ref[0, r0:r0 + BAND, g * LANE:(g + 1) * LANE] for g in range(GQA_GROUP)], axis=0)
        q_st = (rot(q_st, cos_t, sin_t) * scale).astype(BF16)
        lo, hi = window(n)
        c0 = lo - (r0 - BAND)
        s = lax.dot_general(q_st, k_scr[lo:hi, :], (((1,), (1,)), ((), ())), preferred_element_type=F32)
        return s + bias[:, c0:c0 + hi - lo]

    s_next = scores(0)
    for n in range(n_blocks):
        s = s_next
        if n + 1 < n_blocks:
            s_next = scores(n + 1)
        r0 = n * BAND
        lo, hi = window(n)
        m = jnp.maximum(jnp.max(s, axis=-1, keepdims=True), sink)
        p = jnp.exp(s - m).astype(BF16)
        oe = jnp.dot(p, v_scr[lo:hi, :], preferred_element_type=F32)
        o = oe[:, :LANE] / (oe[:, LANE:] + jnp.exp(sink - m))
        for g in range(GQA_GROUP):
            o_ref[0, r0:r0 + BAND, g * LANE:(g + 1) * LANE] = o[g * BAND:(g + 1) * BAND, :].astype(BF16)


def _gqa_attn(qkv, sinks, tabs):
    b, s, _ = qkv.shape
    gw = GQA_GROUP * GQA_HEAD_DIM
    sink_col = jnp.broadcast_to(sinks.astype(F32).reshape(GQA_KV_HEADS, GQA_GROUP, 1, 1),
                                (GQA_KV_HEADS, GQA_GROUP, BAND, 1)).reshape(GQA_KV_HEADS, GQA_GROUP * BAND, 1)
    est = (4 * _nbytes((s, gw), BF16) + 7 * _nbytes((s, LANE), BF16) + 6 * _nbytes((s, LANE), F32)
           + 8 * _nbytes((GQA_GROUP * BAND, 3 * BAND), F32))
    tab = pl.BlockSpec((s, LANE), lambda bi, h: (0, 0))
    cos_t, sin_t = tabs
    half = GQA_ROT // 2
    lane = jnp.arange(LANE)
    swap = (((lane[None, :] < half) & (lane[:, None] == lane[None, :] + half))
            | ((lane[None, :] >= half) & (lane[None, :] < 2 * half) & (lane[:, None] == lane[None, :] - half)))
    return pl.pallas_call(
        functools.partial(_gqa_kernel, scale=GQA_HEAD_DIM ** -0.5),
        name="gqa_window_attn",
        grid=(b, GQA_KV_HEADS),
        in_specs=[
            pl.BlockSpec((1, s, gw), lambda bi, h: (bi, 0, h)),
            pl.BlockSpec((1, s, LANE), lambda bi, h: (bi, 0, GQA_HEADS + h)),
            pl.BlockSpec((1, s, LANE), lambda bi, h: (bi, 0, GQA_HEADS + GQA_KV_HEADS + h)),
            pl.BlockSpec((1, GQA_GROUP * BAND, 1), lambda bi, h: (h, 0, 0)),
            tab, tab,
            pl.BlockSpec((LANE, LANE), lambda bi, h: (0, 0)),
        ],
        out_specs=pl.BlockSpec((1, s, gw), lambda bi, h: (bi, 0, h)),
        out_shape=jax.ShapeDtypeStruct((b, s, GQA_HEADS * GQA_HEAD_DIM), BF16),
        scratch_shapes=[pltpu.VMEM((s, LANE), BF16), pltpu.VMEM((s, 2 * LANE), BF16)],
        compiler_params=_cparams(("parallel", "parallel"), est),
    )(qkv, qkv, qkv, sink_col, cos_t, sin_t, swap.astype(BF16))


def _short_conv_kernel(x_ref, w_ref, b_ref, o_ref):
    x = x_ref[0].astype(F32)
    s_len = x.shape[0]
    row = lax.broadcasted_iota(jnp.int32, x.shape, 0)
    prev = jnp.where(row == 0, 0.0, pltpu.roll(x, 1, 0))
    nxt = jnp.where(row == s_len - 1, 0.0, pltpu.roll(x, s_len - 1, 0))
    y = b_ref[...] + prev * w_ref[0:1, :]
    y = y + x * w_ref[1:2, :]
    y = y + nxt * w_ref[2:3, :]
    o_ref[0] = y.astype(BF16)


def _short_conv(hy, w, bias, tc=256):
    b, s, c = hy.shape
    est = 4 * _nbytes((s, tc), BF16) + 6 * _nbytes((s, tc), F32)
    return pl.pallas_call(
        _short_conv_kernel,
        name="hyena_short_conv",
        grid=(b, c // tc),
        in_specs=[
            pl.BlockSpec((1, s, tc), lambda bi, j: (bi, 0, j)),
            pl.BlockSpec((HY_SHORT, tc), lambda bi, j: (0, j)),
            pl.BlockSpec((1, tc), lambda bi, j: (0, j)),
        ],
        out_specs=pl.BlockSpec((1, s, tc), lambda bi, j: (bi, 0, j)),
        out_shape=jax.ShapeDtypeStruct((b, s, c), BF16),
        compiler_params=_cparams(("parallel", "parallel"), est),
    )(hy, w, bias.reshape(1, c))


def _filter_mlp_kernel(z_ref, w1_ref, b1_ref, w2_ref, b2_ref, w3_ref, b3_ref, fr_ref, h_ref):
    dot = functools.partial(jnp.dot, preferred_element_type=F32, precision=HIGHEST)
    fr = fr_ref[...]
    h = jnp.sin(fr * (dot(z_ref[...], w1_ref[...]) + b1_ref[...]))
    h = jnp.sin(fr * (dot(h, w2_ref[...]) + b2_ref[...]))
    h_ref[...] = jnp.sin(fr * (dot(h, w3_ref[...]) + b3_ref[...]))


def _filter_mlp(z, w1, b1, w2, b2, w3, b3, freq):
    n_lag = z.shape[0]
    full = lambda shape: pl.BlockSpec(shape, lambda i: (0,) * len(shape))
    est = 12 * _nbytes((n_lag, LANE), F32)
    return pl.pallas_call(
        _filter_mlp_kernel,
        name="hyena_filter_mlp",
        grid=(1,),
        in_specs=[full(z.shape), full(w1.shape), full((1, HY_FFN)), full(w2.shape), full((1, HY_FFN)),
                  full(w3.shape), full((1, HY_FFN)), full((1, HY_FFN))],
        out_specs=full((n_lag, HY_FFN)),
        out_shape=jax.ShapeDtypeStruct((n_lag, HY_FFN), F32),
        compiler_params=_cparams(("arbitrary",), est),
    )(z, w1, b1.reshape(1, -1), w2, b2.reshape(1, -1), w3, b3.reshape(1, -1), freq.reshape(1, -1))


def _filter_gen_kernel(h_ref, w4f_ref, w4b_ref, ksum_ref, kdiff_ref, nyq_ref):
    dot = functools.partial(jnp.dot, preferred_element_type=F32, precision=HIGHEST)
    h = h_ref[...]
    hf = dot(h, w4f_ref[...])
    hb = dot(h, w4b_ref[...])
    n_lag, tn = hf.shape
    row = lax.broadcasted_iota(jnp.int32, (n_lag, tn), 0)
    col = pl.program_id(1) * tn + lax.broadcasted_iota(jnp.int32, (1, tn), 1)
    t = row.astype(F32) / (n_lag - 1)
    max_decay = math.log(HY_TARGET) / HY_DECAY_PCT_SHORT
    min_decay = math.log(HY_TARGET) / HY_DECAY_PCT_LONG
    delta = min_decay + (max_decay - min_decay) * (col.astype(F32) / (HY_D - 1))
    decay = jnp.exp(-t * jnp.abs(delta))
    kf = hf * decay
    kb = jnp.where(row == 0, 0.0, hb * decay)
    inv = 1.0 / (jnp.sum(jnp.abs(kf), axis=0, keepdims=True) + jnp.sum(jnp.abs(kb), axis=0, keepdims=True))
    ksum = (kf + kb) * inv
    ksum_ref[0] = ksum.astype(BF16)
    kdiff_ref[0] = ((kb - kf) * inv).astype(BF16)
    alt = (1 - 2 * (row & 1)).astype(F32)
    nyq_ref[0] = jnp.sum(ksum * alt, axis=0, keepdims=True)


def _filter_gen(h, w4, tn=256):
    n_lag = h.shape[0]
    nblk = HY_D // tn
    est = 10 * _nbytes((n_lag, tn), F32) + 4 * _nbytes((n_lag, LANE), F32) + 4 * _nbytes((n_lag, tn), BF16)
    out = jax.ShapeDtypeStruct((HY_ORDER, n_lag, HY_D), BF16)
    return pl.pallas_call(
        _filter_gen_kernel,
        name="hyena_filter_gen",
        grid=(HY_ORDER, nblk),
        in_specs=[
            pl.BlockSpec((n_lag, HY_FFN), lambda o, j: (0, 0)),
            pl.BlockSpec((HY_FFN, tn), lambda o, j: (0, 2 * o * nblk + j)),
            pl.BlockSpec((HY_FFN, tn), lambda o, j: (0, (2 * o + 1) * nblk + j)),
        ],
        out_specs=[pl.BlockSpec((1, n_lag, tn), lambda o, j: (o, 0, j))] * 2
        + [pl.BlockSpec((1, 1, tn), lambda o, j: (o, 0, j))],
        out_shape=[out, out, jax.ShapeDtypeStruct((HY_ORDER, 1, HY_D), F32)],
        compiler_params=_cparams(("parallel", "parallel"), est),
    )(h, w4, w4)


def _filter_dft_kernel(ct_ref, st_ref, ksum_ref, kdiff_ref, nyq_ref, a_ref, bq_ref, a2_ref):
    re = jnp.dot(ct_ref[...], ksum_ref[0], preferred_element_type=F32)
    im = jnp.dot(st_ref[...], kdiff_ref[0], preferred_element_type=F32)
    tf = re.shape[0]
    row = pl.program_id(2) * tf + lax.broadcasted_iota(jnp.int32, re.shape, 0)
    is0 = row == 0
    wgt = jnp.where(is0, 1.0 / DFT_N, 2.0 / DFT_N)
    a = re * wgt
    a_ref[0] = a
    bq_ref[0] = jnp.where(is0, 0.0, im * wgt)
    a2_ref[0] = jnp.where(is0, nyq_ref[0] * wgt, a)


def _filter_dft(ct, st, ksum, kdiff, nyq, tf=512, tn=512):
    n_f, n_s = ct.shape
    est = (4 * _nbytes((tf, n_s), BF16) + 4 * _nbytes((n_s, tn), BF16) + 6 * _nbytes((tf, tn), F32)
           + 8 * _nbytes((tf, tn), F32))
    out = jax.ShapeDtypeStruct((HY_ORDER, n_f, HY_D), F32)
    return pl.pallas_call(
        _filter_dft_kernel,
        name="hyena_filter_dft",
        grid=(HY_ORDER, HY_D // tn, n_f // tf),
        in_specs=[
            pl.BlockSpec((tf, n_s), lambda o, j, k: (k, 0)),
            pl.BlockSpec((tf, n_s), lambda o, j, k: (k, 0)),
            pl.BlockSpec((1, n_s, tn), lambda o, j, k: (o, 0, j)),
            pl.BlockSpec((1, n_s, tn), lambda o, j, k: (o, 0, j)),
            pl.BlockSpec((1, 1, tn), lambda o, j, k: (o, 0, j)),
        ],
        out_specs=[pl.BlockSpec((1, tf, tn), lambda o, j, k: (o, k, j))] * 3,
        out_shape=[out, out, out],
        compiler_params=_cparams(("parallel", "parallel", "parallel"), est),
    )(ct, st, ksum, kdiff, nyq)


def _conv_fwd_kernel(ct_ref, st_ref, u_ref, a_ref, bq_ref, a2_ref, yr_ref, ys_ref):
    u = u_ref[0]
    uc = jnp.dot(ct_ref[...], u, preferred_element_type=F32)
    us = jnp.dot(st_ref[...], u, preferred_element_type=F32)
    bq = bq_ref[0]
    yr_ref[0] = (uc * a_ref[0] + us * bq).astype(BF16)
    ys_ref[0] = (us * a2_ref[0] - uc * bq).astype(BF16)


def _conv_fwd(ct, st, u, u_blk0, spec, order, tf=512, tn=512):
    a, bq, a2 = spec
    b, s, _ = u.shape
    n_f = ct.shape[0]
    est = (4 * _nbytes((tf, s), BF16) + 2 * _nbytes((s, tn), BF16) + 6 * _nbytes((tf, tn), F32)
           + 4 * _nbytes((tf, tn), BF16) + 6 * _nbytes((tf, tn), F32))
    coef = pl.BlockSpec((1, tf, tn), lambda bi, j, k: (order, k, j))
    out = jax.ShapeDtypeStruct((b, n_f, HY_D), BF16)
    return pl.pallas_call(
        _conv_fwd_kernel,
        name="hyena_conv_fwd",
        grid=(b, HY_D // tn, n_f // tf),
        in_specs=[
            pl.BlockSpec((tf, s), lambda bi, j, k: (k, 0)),
            pl.BlockSpec((tf, s), lambda bi, j, k: (k, 0)),
            pl.BlockSpec((1, s, tn), lambda bi, j, k: (bi, 0, u_blk0 * (HY_D // tn) + j)),
            coef, coef, coef,
        ],
        out_specs=[pl.BlockSpec((1, tf, tn), lambda bi, j, k: (bi, k, j))] * 2,
        out_shape=[out, out],
        compiler_params=_cparams(("parallel", "parallel", "parallel"), est),
    )(ct, st, u, a, bq, a2)


def _conv_inv_kernel(ct_ref, stt_ref, yr_ref, ys_ref, g_ref, u_ref, skip_ref, o_ref):
    y = jnp.dot(ct_ref[...], yr_ref[0], preferred_element_type=F32)
    y = y + jnp.dot(stt_ref[...], ys_ref[0], preferred_element_type=F32)
    u = u_ref[0].astype(F32)
    o_ref[0] = (g_ref[0].astype(F32) * (y + u * skip_ref[0])).astype(BF16)


def _conv_inv(ct, stt, yr, ys, gsrc, g_blk0, usrc, u_blk0, skip, order, tt=512, tn=512):
    b, n_f, _ = yr.shape
    s = ct.shape[0]
    nblk = HY_D // tn
    est = (4 * _nbytes((tt, n_f), BF16) + 4 * _nbytes((n_f, tn), BF16) + 6 * _nbytes((tt, tn), BF16)
           + 6 * _nbytes((tt, tn), F32))
    return pl.pallas_call(
        _conv_inv_kernel,
        name="hyena_conv_inv",
        grid=(b, nblk, s // tt),
        in_specs=[
            pl.BlockSpec((tt, n_f), lambda bi, j, k: (k, 0)),
            pl.BlockSpec((tt, n_f), lambda bi, j, k: (k, 0)),
            pl.BlockSpec((1, n_f, tn), lambda bi, j, k: (bi, 0, j)),
            pl.BlockSpec((1, n_f, tn), lambda bi, j, k: (bi, 0, j)),
            pl.BlockSpec((1, tt, tn), lambda bi, j, k: (bi, k, g_blk0 * nblk + j)),
            pl.BlockSpec((1, tt, tn), lambda bi, j, k: (bi, k, u_blk0 * nblk + j)),
            pl.BlockSpec((1, 1, tn), lambda bi, j, k: (order, 0, j)),
        ],
        out_specs=pl.BlockSpec((1, tt, tn), lambda bi, j, k: (bi, k, j)),
        out_shape=jax.ShapeDtypeStruct((b, s, HY_D), BF16),
        compiler_params=_cparams(("parallel", "parallel", "parallel"), est),
    )(ct, stt, yr, ys, gsrc, usrc, skip.reshape(HY_ORDER, 1, HY_D))


def _rope_tables(seq, rot_dim):
    half = rot_dim // 2
    pos = jnp.arange(seq, dtype=F32)
    inv = ROPE_THETA ** (-jnp.arange(0, rot_dim, 2, dtype=F32) / rot_dim)
    ang = pos[:, None] * inv[None, :]
    cos, sin = jnp.cos(ang), jnp.sin(ang)
    rest = LANE - rot_dim
    return cos, sin, half, rest


def _rope_lane_tables(seq, rot_dim, rest_passthrough):
    cos, sin, half, rest = _rope_tables(seq, rot_dim)
    fill = jnp.ones((seq, rest), F32) if rest_passthrough else jnp.zeros((seq, rest), F32)
    zero_h = jnp.zeros((seq, half), F32)
    zero_r = jnp.zeros((seq, rest), F32)
    cos_t = jnp.concatenate([cos, cos, fill], axis=1)
    sin_up = jnp.concatenate([zero_h, sin, zero_r], axis=1)
    sin_dn = jnp.concatenate([-sin, zero_h, zero_r], axis=1)
    return cos_t, sin_up, sin_dn


def _dft_tables(n_half):
    n = 2 * n_half
    blk = 64
    idx = jnp.arange(n_half, dtype=jnp.int32)
    unit = 2.0 * math.pi / n
    ang_a = ((blk * idx[:n_half // blk, None] * idx[None, :]) & (n - 1)).astype(F32) * unit
    ang_b = ((idx[:blk, None] * idx[None, :]) & (n - 1)).astype(F32) * unit
    ca, sa = jnp.cos(ang_a)[:, None, :], jnp.sin(ang_a)[:, None, :]
    cb, sb = jnp.cos(ang_b)[None], jnp.sin(ang_b)[None]
    ct = (ca * cb - sa * sb).reshape(n_half, n_half)
    sn = (sa * cb + ca * sb).reshape(n_half, n_half)
    alt = (1 - 2 * (idx & 1)).astype(F32)
    st = jnp.where(idx[:, None] == 0, alt[None, :], sn)
    stt = jnp.where(idx[None, :] == 0, alt[:, None], sn)
    return ct.astype(BF16), st.astype(BF16), stt.astype(BF16)


def _filter_features(n_lag):
    t = jnp.linspace(0.0, 1.0, n_lag, dtype=F32)[:, None]
    bands = (HY_EMB - 1) // 2
    wpos = 2.0 * math.pi * jnp.arange(n_lag, dtype=F32) / n_lag
    fb = jnp.linspace(1e-4, bands - 1, bands, dtype=F32)
    fw = wpos[:, None] * fb[None, :]
    z = jnp.concatenate([t, jnp.cos(fw), -jnp.sin(fw)], axis=-1)
    return jnp.pad(z, ((0, 0), (0, LANE - HY_EMB)))


def kernel(x, c, ada_mix_w, ada_mix_b, norm_mix_g, ada_mlp_w, ada_mlp_b, norm_mlp_g, w_mlp_in, w_mlp_out, e_w_in, e_q_norm_g, e_kv_norm_g, e_w_uq, e_w_ukv, e_conv_w, e_conv_b, e_f_w1, e_f_b1, e_f_w2, e_f_b2, e_f_w3, e_f_b3, e_f_freq, e_f_w4, e_hy_skip, e_w_out, o_w_qkv, o_sinks, o_w_o, final_norm_g):
    b, s, d = x.shape
    c8 = jnp.pad(c, ((0, 8 - b), (0, 0)))
    mod_mix = _ada(c8, ada_mix_w, ada_mix_b)[:, :b].reshape(DEPTH, b, 1, 3 * d)
    mod_mlp = _ada(c8, ada_mlp_w, ada_mlp_b)[:, :b].reshape(DEPTH, b, 1, 3 * d)

    mla_tabs = _rope_lane_tables(s, MLA_ROPE, rest_passthrough=False)
    gqa_cos, gqa_sin_up, gqa_sin_dn = _rope_lane_tables(s, GQA_ROT, rest_passthrough=True)
    gqa_tabs = (gqa_cos, gqa_sin_up + gqa_sin_dn)
    ct, st, stt = _dft_tables(s)
    z_feat = _filter_features(s)
    q_scale = (MLA_NOPE + MLA_ROPE) ** -0.5 * math.log2(math.e)

    for l in range(DEPTH):
        i = l // 2
        if l % 2 == 0:
            w_in = e_w_in[i]
            w_lat = jnp.pad(w_in[:, :Q_LORA + KV_LORA + MLA_ROPE], ((0, 0), (0, LANE - MLA_ROPE)))
            w_hy = w_in[:, Q_LORA + KV_LORA + MLA_ROPE:]
            h = _normmod(x, mod_mix[l], norm_mix_g[l])
            lat = _matmul(h, w_lat[None], 0, F32, tn=LAT_W)
            hy_in = _matmul(h, w_hy[None], 0, BF16)

            w_uq = e_w_uq[i].reshape(Q_LORA, MLA_HEADS, MLA_NOPE + MLA_ROPE)
            wq = jnp.concatenate([
                w_uq[:, :, :MLA_NOPE].reshape(Q_LORA, -1),
                jnp.pad(w_uq[:, :, MLA_NOPE:], ((0, 0), (0, 0), (0, LANE - MLA_ROPE))).reshape(Q_LORA, -1),
            ], axis=1).astype(BF16)
            w_ukv = e_w_ukv[i].reshape(KV_LORA, MLA_HEADS, MLA_NOPE + MLA_V)
            wk = w_ukv[:, :, :MLA_NOPE].reshape(KV_LORA, -1).astype(BF16)
            wv = w_ukv[:, :, MLA_NOPE:].reshape(KV_LORA, -1).astype(BF16)
            q, k, v = _mla_proj(lat, e_q_norm_g[i], e_kv_norm_g[i], wq, wk, wv, mla_tabs, q_scale)
            a_mla = _mla_attn(q, k, v)

            w1 = jnp.pad(e_f_w1[i], ((0, LANE - HY_EMB), (0, 0)))
            h_filt = _filter_mlp(z_feat, w1, e_f_b1[i], e_f_w2[i], e_f_b2[i], e_f_w3[i], e_f_b3[i], e_f_freq[i])
            spec = _filter_dft(ct, st, *_filter_gen(h_filt, e_f_w4[i]))
            u = _short_conv(hy_in, e_conv_w[i], e_conv_b[i])
            yr, ys = _conv_fwd(ct, st, u, 0, spec, 0)
            zc = _conv_inv(ct, stt, yr, ys, u, 1, u, 0, e_hy_skip[i], 0)
            yr, ys = _conv_fwd(ct, st, zc, 0, spec, 1)
            b_hy = _conv_inv(ct, stt, yr, ys, u, 2, zc, 0, e_hy_skip[i], 1)

            x = _proj_res([a_mla, b_hy], e_w_out, i, x, mod_mix[l])
        else:
            h = _normmod(x, mod_mix[l], norm_mix_g[l])
            qkv = _matmul(h, o_w_qkv, i, BF16)
            o = _gqa_attn(qkv, o_sinks[i], gqa_tabs)
            x = _proj_res([o], o_w_o, i, x, mod_mix[l])
        x = _mlp(x, mod_mlp[l], norm_mlp_g[l], w_mlp_in, w_mlp_out, l, final_norm_g, l == DEPTH - 1)
    return x
```

```python
import functools
import math

import jax
import jax.numpy as jnp
from jax import lax
from jax.experimental import pallas as pl
from jax.experimental.pallas import tpu as pltpu

F32 = jnp.float32
BF16 = jnp.bfloat16
HIGHEST = lax.Precision.HIGHEST

D_MODEL = 2048
BATCH = 4
SEQ = 2048
DEPTH = 4
RMS_EPS = 1e-6
ROPE_THETA = 500000.0
NEG_INF = -1e30
MLA_HEADS = 8
MLA_NOPE = 128
MLA_ROPE = 64
MLA_V = 128
Q_LORA = 512
KV_LORA = 256
HY_D = 1024
HY_ORDER = 2
HY_SHORT = 3
HY_EMB = 33
HY_FFN = 64
HY_DECAY_PCT_SHORT = 0.3
HY_DECAY_PCT_LONG = 1.5
HY_TARGET = 1e-2
GQA_HEADS = 16
GQA_KV_HEADS = 4
GQA_HEAD_DIM = 128
GQA_ROT = GQA_HEAD_DIM // 4
GQA_GROUP = GQA_HEADS // GQA_KV_HEADS
WINDOW = 128
BAND = 128
D_FF = 4 * D_MODEL
LAT_W = Q_LORA + KV_LORA + 128
MLA_QK = 256
DFT_N = 2 * SEQ

LANE = 128
V7X_VMEM_BYTES = 64 * 1024 * 1024
V7X_VMEM_BUDGET = 56 * 1024 * 1024


def _cparams(semantics, est_bytes):
    limit = int(min(V7X_VMEM_BUDGET, max(32 * 1024 * 1024, est_bytes * 3 // 2)))
    return pltpu.CompilerParams(dimension_semantics=semantics, vmem_limit_bytes=limit)


def _nbytes(shape, dtype):
    return math.prod(shape) * jnp.dtype(dtype).itemsize


def _rms(x, g):
    ms = jnp.mean(x * x, axis=-1, keepdims=True)
    return x * lax.rsqrt(ms + RMS_EPS) * g


def _ada_kernel(c_ref, w_ref, b_ref, o_ref):
    cv = c_ref[...]
    s = cv * (1.0 / (1.0 + jnp.exp(-cv)))
    o_ref[0] = jnp.dot(s.astype(BF16), w_ref[0].astype(BF16), preferred_element_type=F32) + b_ref[0]


def _ada(c8, w, b, tn=1536):
    n_l, d, n = w.shape
    est = 2 * _nbytes((d, tn), F32) + _nbytes((d, tn), BF16) + 4 * _nbytes((8, tn), F32) + 2 * _nbytes((8, d), F32)
    return pl.pallas_call(
        _ada_kernel,
        name="ada_modulation",
        grid=(n_l, n // tn),
        in_specs=[
            pl.BlockSpec((8, d), lambda l, j: (0, 0)),
            pl.BlockSpec((1, d, tn), lambda l, j: (l, 0, j)),
            pl.BlockSpec((1, 1, tn), lambda l, j: (l, 0, j)),
        ],
        out_specs=pl.BlockSpec((1, 8, tn), lambda l, j: (l, 0, j)),
        out_shape=jax.ShapeDtypeStruct((n_l, 8, n), F32),
        compiler_params=_cparams(("parallel", "parallel"), est),
    )(c8, w, b.reshape(n_l, 1, n))


def _normmod_rows(x_ref, shift_ref, scale_ref, g_ref, h_ref, rows):
    gain = g_ref[...] * (1.0 + scale_ref[0])
    sh = shift_ref[0]
    tm = h_ref.shape[0]

    def body(r, carry):
        sl = pl.ds(pl.multiple_of(r * rows, rows), rows)
        xv = x_ref[0, sl, :]
        inv = lax.rsqrt(jnp.mean(xv * xv, axis=-1, keepdims=True) + RMS_EPS)
        h_ref[sl, :] = (xv * inv * gain + sh).astype(BF16)
        return carry

    lax.fori_loop(0, tm // rows, body, 0)


def _normmod_kernel(x_ref, shift_ref, scale_ref, g_ref, h_ref):
    _normmod_rows(x_ref, shift_ref, scale_ref, g_ref, h_ref.at[0], 128)


def _normmod(x, mod, g, tm=512):
    b, s, d = x.shape
    est = 2 * _nbytes((tm, d), F32) + 2 * _nbytes((tm, d), BF16) + 8 * _nbytes((128, d), F32)
    return pl.pallas_call(
        _normmod_kernel,
        name="normmod",
        grid=(b, s // tm),
        in_specs=[
            pl.BlockSpec((1, tm, d), lambda bi, i: (bi, i, 0)),
            pl.BlockSpec((1, 1, d), lambda bi, i: (bi, 0, 0)),
            pl.BlockSpec((1, 1, d), lambda bi, i: (bi, 0, 1)),
            pl.BlockSpec((1, d), lambda bi, i: (0, 0)),
        ],
        out_specs=pl.BlockSpec((1, tm, d), lambda bi, i: (bi, i, 0)),
        out_shape=jax.ShapeDtypeStruct((b, s, d), BF16),
        compiler_params=_cparams(("parallel", "parallel"), est),
    )(x, mod, mod, g.reshape(1, d))


def _matmul_kernel(h_ref, w_ref, o_ref):
    o_ref[0] = jnp.dot(h_ref[0], w_ref[...].astype(BF16), preferred_element_type=F32).astype(o_ref.dtype)


def _matmul(h, w_all, layer, out_dtype, tm=2048, tn=512, n_cols=None):
    b, s, d = h.shape
    n = w_all.shape[2] if n_cols is None else n_cols
    tn = min(tn, n)
    est = (2 * _nbytes((tm, d), BF16) + 2 * _nbytes((d, tn), F32) + _nbytes((d, tn), BF16)
           + 3 * _nbytes((tm, tn), F32))
    return pl.pallas_call(
        _matmul_kernel,
        name="matmul",
        grid=(b, s // tm, n // tn),
        in_specs=[
            pl.BlockSpec((1, tm, d), lambda bi, i, j: (bi, i, 0)),
            pl.BlockSpec((None, d, tn), lambda bi, i, j: (layer, 0, j)),
        ],
        out_specs=pl.BlockSpec((1, tm, tn), lambda bi, i, j: (bi, i, j)),
        out_shape=jax.ShapeDtypeStruct((b, s, n), out_dtype),
        compiler_params=_cparams(("parallel", "parallel", "parallel"), est),
    )(h, w_all)


def _mlp_kernel(x_ref, shift_ref, scale_ref, gate_ref, g_ref, w1_ref, w2_ref, fg_ref, o_ref, h_ref, *, final_norm):
    f = pl.program_id(2)

    @pl.when(f == 0)
    def _():
        _normmod_rows(x_ref, shift_ref, scale_ref, g_ref, h_ref, 128)

        o_ref[...] = jnp.zeros_like(o_ref)

    a = jnp.dot(h_ref[...], w1_ref[...].astype(BF16), preferred_element_type=F32)
    a = jnp.square(jnp.maximum(a, 0.0)).astype(BF16)
    o_ref[0] += jnp.dot(a, w2_ref[...].astype(BF16), preferred_element_type=F32)

    @pl.when(f == pl.num_programs(2) - 1)
    def _():
        gate = gate_ref[0]
        fg = fg_ref[...]
        rows = 128

        def body(r, carry):
            sl = pl.ds(pl.multiple_of(r * rows, rows), rows)
            y = x_ref[0, sl, :] + gate * o_ref[0, sl, :]
            if final_norm:
                y = _rms(y, fg)
            o_ref[0, sl, :] = y
            return carry

        lax.fori_loop(0, o_ref.shape[1] // rows, body, 0)


def _mlp(x, mod, g, w1_all, w2_all, layer, final_g, final_norm, tm=1024, tf=512):
    b, s, d = x.shape
    ff = w1_all.shape[2]
    est = (4 * _nbytes((tm, d), F32) + _nbytes((tm, d), BF16) + 4 * _nbytes((d, tf), F32)
           + 2 * _nbytes((d, tf), BF16) + 2 * _nbytes((tm, tf), F32))
    return pl.pallas_call(
        functools.partial(_mlp_kernel, final_norm=final_norm),
        name="mlp_relu2",
        grid=(b, s // tm, ff // tf),
        in_specs=[
            pl.BlockSpec((1, tm, d), lambda bi, i, f: (bi, i, 0)),
            pl.BlockSpec((1, 1, d), lambda bi, i, f: (bi, 0, 0)),
            pl.BlockSpec((1, 1, d), lambda bi, i, f: (bi, 0, 1)),
            pl.BlockSpec((1, 1, d), lambda bi, i, f: (bi, 0, 2)),
            pl.BlockSpec((1, d), lambda bi, i, f: (0, 0)),
            pl.BlockSpec((None, d, tf), lambda bi, i, f: (layer, 0, f)),
            pl.BlockSpec((None, tf, d), lambda bi, i, f: (layer, f, 0)),
            pl.BlockSpec((1, d), lambda bi, i, f: (0, 0)),
        ],
        out_specs=pl.BlockSpec((1, tm, d), lambda bi, i, f: (bi, i, 0)),
        out_shape=jax.ShapeDtypeStruct((b, s, d), F32),
        scratch_shapes=[pltpu.VMEM((tm, d), BF16)],
        compiler_params=_cparams(("parallel", "parallel", "arbitrary"), est),
    )(x, mod, mod, mod, g.reshape(1, d), w1_all, w2_all, final_g.reshape(1, d))


def _proj_res_kernel(*refs, n_in):
    a_refs, w_refs = refs[:n_in], refs[n_in:2 * n_in]
    x_ref, gate_ref, o_ref = refs[2 * n_in:]
    acc = None
    for a_ref, w_ref in zip(a_refs, w_refs):
        p = jnp.dot(a_ref[0], w_ref[...].astype(BF16), preferred_element_type=F32)
        acc = p if acc is None else acc + p
    o_ref[0] = x_ref[0] + gate_ref[0] * acc


def _proj_res(a_list, w_all, layer, x, mod, tm=2048, tn=512):
    b, s, d = x.shape
    n_in = len(a_list)
    kk = a_list[0].shape[-1]
    assert all(a.shape[-1] == kk for a in a_list) and w_all.shape[1] == n_in * kk
    est = (n_in * (2 * _nbytes((tm, kk), BF16) + 2 * _nbytes((kk, tn), F32) + _nbytes((kk, tn), BF16))
           + 6 * _nbytes((tm, tn), F32))
    in_specs = [pl.BlockSpec((1, tm, kk), lambda bi, i, j: (bi, i, 0)) for _ in a_list]
    in_specs += [pl.BlockSpec((None, kk, tn), functools.partial(lambda bi, i, j, r: (layer, r, j), r=r))
                 for r in range(n_in)]
    in_specs += [
        pl.BlockSpec((1, tm, tn), lambda bi, i, j: (bi, i, j)),
        pl.BlockSpec((1, 1, tn), lambda bi, i, j: (bi, 0, 2 * (d // tn) + j)),
    ]
    return pl.pallas_call(
        functools.partial(_proj_res_kernel, n_in=n_in),
        name="proj_residual",
        grid=(b, s // tm, d // tn),
        in_specs=in_specs,
        out_specs=pl.BlockSpec((1, tm, tn), lambda bi, i, j: (bi, i, j)),
        out_shape=jax.ShapeDtypeStruct((b, s, d), F32),
        compiler_params=_cparams(("parallel", "parallel", "parallel"), est),
    )(*a_list, *([w_all] * n_in), x, mod)


def _rot_lanes(blk, cos_t, sin_up, sin_dn, half):
    return (blk * cos_t + pltpu.roll(blk, half, 1) * sin_up + pltpu.roll(blk, LANE - half, 1) * sin_dn)


def _mla_proj_kernel(lat_ref, gq_ref, gkv_ref, wq_ref, wk_ref, wv_ref, cos_ref, sup_ref, sdn_ref,
                     q_ref, k_ref, v_ref, *, q_scale):
    lat = lat_ref[0]
    qn = _rms(lat[:, :Q_LORA], gq_ref[...]).astype(BF16)
    kvn = _rms(lat[:, Q_LORA:Q_LORA + KV_LORA], gkv_ref[...]).astype(BF16)
    cos_t, sin_up, sin_dn = cos_ref[...], sup_ref[...], sdn_ref[...]
    half = MLA_ROPE // 2
    q = jnp.dot(qn, wq_ref[...], preferred_element_type=F32) * q_scale
    kn = jnp.dot(kvn, wk_ref[...], preferred_element_type=F32)
    v = jnp.dot(kvn, wv_ref[...], preferred_element_type=F32).astype(BF16)
    ones = jnp.ones((v.shape[0], MLA_V), BF16)
    kr = _rot_lanes(lat[:, Q_LORA + KV_LORA:], cos_t, sin_up, sin_dn, half).astype(BF16)
    nope_w = MLA_HEADS * MLA_NOPE
    for h in range(MLA_HEADS):
        c0 = h * MLA_QK
        v_ref[0, :, 2 * h * MLA_V:(2 * h + 1) * MLA_V] = v[:, h * MLA_V:(h + 1) * MLA_V]
        v_ref[0, :, (2 * h + 1) * MLA_V:(2 * h + 2) * MLA_V] = ones
        q_ref[0, :, c0:c0 + LANE] = q[:, h * LANE:(h + 1) * LANE].astype(BF16)
        qr = q[:, nope_w + h * LANE:nope_w + (h + 1) * LANE]
        q_ref[0, :, c0 + LANE:c0 + 2 * LANE] = _rot_lanes(qr, cos_t, sin_up, sin_dn, half).astype(BF16)
        k_ref[0, :, c0:c0 + LANE] = kn[:, h * LANE:(h + 1) * LANE].astype(BF16)
        k_ref[0, :, c0 + LANE:c0 + 2 * LANE] = kr


def _mla_proj(lat, gq, gkv, wq, wk, wv, tabs, q_scale, tm=512):
    b, s, _ = lat.shape
    hq = MLA_HEADS * MLA_QK
    hv = MLA_HEADS * 2 * MLA_V
    est = (2 * _nbytes((tm, LAT_W), F32) + 2 * _nbytes(wq.shape, BF16) + 2 * _nbytes(wk.shape, BF16)
           + 2 * _nbytes(wv.shape, BF16) + 4 * _nbytes((tm, hq), BF16) + 2 * _nbytes((tm, hv), BF16)
           + 3 * _nbytes((tm, hq), F32))
    full = lambda shape: pl.BlockSpec(shape, lambda bi, i: (0,) * len(shape))
    tab = pl.BlockSpec((tm, LANE), lambda bi, i: (i, 0))
    return pl.pallas_call(
        functools.partial(_mla_proj_kernel, q_scale=q_scale),
        name="mla_proj",
        grid=(b, s // tm),
        in_specs=[
            pl.BlockSpec((1, tm, LAT_W), lambda bi, i: (bi, i, 0)),
            full((1, Q_LORA)), full((1, KV_LORA)), full(wq.shape), full(wk.shape), full(wv.shape),
            tab, tab, tab,
        ],
        out_specs=[
            pl.BlockSpec((1, tm, hq), lambda bi, i: (bi, i, 0)),
            pl.BlockSpec((1, tm, hq), lambda bi, i: (bi, i, 0)),
            pl.BlockSpec((1, tm, hv), lambda bi, i: (bi, i, 0)),
        ],
        out_shape=[
            jax.ShapeDtypeStruct((b, s, hq), BF16),
            jax.ShapeDtypeStruct((b, s, hq), BF16),
            jax.ShapeDtypeStruct((b, s, hv), BF16),
        ],
        compiler_params=_cparams(("parallel", "parallel"), est),
    )(lat, gq.reshape(1, -1), gkv.reshape(1, -1), wq, wk, wv, *tabs)


def _mla_attn_kernel(q_ref, k_ref, v_ref, o_ref, *, chain_rows):
    n_chains = q_ref.shape[1] // chain_rows

    def scores(c):
        rs = slice(c * chain_rows, (c + 1) * chain_rows)
        return lax.dot_general(q_ref[0, rs, :], k_ref[0], (((1,), (1,)), ((), ())), preferred_element_type=F32)

    s_next = scores(0)
    for c in range(n_chains):
        s = s_next
        if c + 1 < n_chains:
            s_next = scores(c + 1)
        m = jnp.max(s, axis=-1, keepdims=True)
        p = jnp.exp2(s - m).astype(BF16)
        oe = jnp.dot(p, v_ref[0], preferred_element_type=F32)
        o_ref[0, c * chain_rows:(c + 1) * chain_rows, :] = (oe[:, :MLA_V] / oe[:, MLA_V:]).astype(BF16)


def _mla_attn(q, k, v, tq=2048, chain_rows=1024):
    b, s, _ = q.shape
    est = (2 * _nbytes((tq, MLA_QK), BF16) + 2 * _nbytes((s, MLA_QK), BF16) + 2 * _nbytes((s, 2 * MLA_V), BF16)
           + 2 * _nbytes((tq, MLA_V), BF16) + 5 * _nbytes((chain_rows, s), F32))
    return pl.pallas_call(
        functools.partial(_mla_attn_kernel, chain_rows=chain_rows),
        name="mla_attn",
        grid=(b, MLA_HEADS, s // tq),
        in_specs=[
            pl.BlockSpec((1, tq, MLA_QK), lambda bi, h, i: (bi, i, h)),
            pl.BlockSpec((1, s, MLA_QK), lambda bi, h, i: (bi, 0, h)),
            pl.BlockSpec((1, s, 2 * MLA_V), lambda bi, h, i: (bi, 0, h)),
        ],
        out_specs=pl.BlockSpec((1, tq, MLA_V), lambda bi, h, i: (bi, i, h)),
        out_shape=jax.ShapeDtypeStruct((b, s, MLA_HEADS * MLA_V), BF16),
        compiler_params=_cparams(("parallel", "parallel", "parallel"), est),
    )(q, k, v)


def _gqa_kernel(q_ref, k_ref, v_ref, sink_ref, cos_ref, sin_ref, swap_ref, o_ref, k_scr, v_scr, *, scale):
    s_len = k_scr.shape[0]
    n_blocks = s_len // BAND
    rows = GQA_GROUP * BAND
    swap = swap_ref[...]

    def rot(x, cos_t, sin_t):
        return x.astype(F32) * cos_t + jnp.dot(x, swap, preferred_element_type=F32) * sin_t

    k_scr[...] = rot(k_ref[0], cos_ref[...], sin_ref[...]).astype(BF16)
    v_scr[:, :LANE] = v_ref[0]
    v_scr[:, LANE:] = jnp.ones((s_len, LANE), BF16)
    sink = sink_ref[0]
    rel = (lax.broadcasted_iota(jnp.int32, (rows, 3 * BAND), 1) - BAND
           - (lax.broadcasted_iota(jnp.int32, (rows, 3 * BAND), 0) & (BAND - 1)))
    bias = jnp.where(jnp.abs(rel) <= WINDOW, 0.0, NEG_INF)

    def window(n):
        r0 = n * BAND
        return max(0, r0 - BAND), min(s_len, r0 + 2 * BAND)

    def scores(n):
        r0 = n * BAND
        cos_t = jnp.concatenate([cos_ref[r0:r0 + BAND, :]] * GQA_GROUP, axis=0)
        sin_t = jnp.concatenate([sin_ref[r0:r0 + BAND, :]] * GQA_GROUP, axis=0)
        q_st = jnp.concatenate([q_ref[0, r0:r0 + BAND, g * LANE:(g + 1) * LANE] for g in range(GQA_GROUP)], axis=0)
        q_st = (rot(q_st, cos_t, sin_t) * scale).astype(BF16)
        lo, hi = window(n)
        c0 = lo - (r0 - BAND)
        s = lax.dot_general(q_st, k_scr[lo:hi, :], (((1,), (1,)), ((), ())), preferred_element_type=F32)
        return s + bias[:, c0:c0 + hi - lo]

    s_next = scores(0)
    for n in range(n_blocks):
        s = s_next
        if n + 1 < n_blocks:
            s_next = scores(n + 1)
        r0 = n * BAND
        lo, hi = window(n)
        m = jnp.maximum(jnp.max(s, axis=-1, keepdims=True), sink)
        p = jnp.exp(s - m).astype(BF16)
        oe = jnp.dot(p, v_scr[lo:hi, :], preferred_element_type=F32)
        o = oe[:, :LANE] / (oe[:, LANE:] + jnp.exp(sink - m))
        for g in range(GQA_GROUP):
            o_ref[0, r0:r0 + BAND, g * LANE:(g + 1) * LANE] = o[g * BAND:(g + 1) * BAND, :].astype(BF16)


def _gqa_attn(qkv, sinks, tabs):
    b, s, _ = qkv.shape
    gw = GQA_GROUP * GQA_HEAD_DIM
    sink_col = jnp.broadcast_to(sinks.astype(F32).reshape(GQA_KV_HEADS, GQA_GROUP, 1, 1),
                                (GQA_KV_HEADS, GQA_GROUP, BAND, 1)).reshape(GQA_KV_HEADS, GQA_GROUP * BAND, 1)
    est = (4 * _nbytes((s, gw), BF16) + 7 * _nbytes((s, LANE), BF16) + 6 * _nbytes((s, LANE), F32)
           + 8 * _nbytes((GQA_GROUP * BAND, 3 * BAND), F32))
    tab = pl.BlockSpec((s, LANE), lambda bi, h: (0, 0))
    cos_t, sin_t = tabs
    half = GQA_ROT // 2
    lane = jnp.arange(LANE)
    swap = (((lane[None, :] < half) & (lane[:, None] == lane[None, :] + half))
            | ((lane[None, :] >= half) & (lane[None, :] < 2 * half) & (lane[:, None] == lane[None, :] - half)))
    return pl.pallas_call(
        functools.partial(_gqa_kernel, scale=GQA_HEAD_DIM ** -0.5),
        name="gqa_window_attn",
        grid=(b, GQA_KV_HEADS),
        in_specs=[
            pl.BlockSpec((1, s, gw), lambda bi, h: (bi, 0, h)),
            pl.BlockSpec((1, s, LANE), lambda bi, h: (bi, 0, GQA_HEADS + h)),
            pl.BlockSpec((1, s, LANE), lambda bi, h: (bi, 0, GQA_HEADS + GQA_KV_HEADS + h)),
            pl.BlockSpec((1, GQA_GROUP * BAND, 1), lambda bi, h: (h, 0, 0)),
            tab, tab,
            pl.BlockSpec((LANE, LANE), lambda bi, h: (0, 0)),
        ],
        out_specs=pl.BlockSpec((1, s, gw), lambda bi, h: (bi, 0, h)),
        out_shape=jax.ShapeDtypeStruct((b, s, GQA_HEADS * GQA_HEAD_DIM), BF16),
        scratch_shapes=[pltpu.VMEM((s, LANE), BF16), pltpu.VMEM((s, 2 * LANE), BF16)],
        compiler_params=_cparams(("parallel", "parallel"), est),
    )(qkv, qkv, qkv, sink_col, cos_t, sin_t, swap.astype(BF16))


def _proj_short_conv_kernel(h_ref, w_ref, cw_ref, cb_ref, o_ref):
    x = jnp.dot(h_ref[0], w_ref[...].astype(BF16), preferred_element_type=F32)
    s_len = x.shape[0]
    row = lax.broadcasted_iota(jnp.int32, x.shape, 0)
    prev = jnp.where(row == 0, 0.0, pltpu.roll(x, 1, 0))
    nxt = jnp.where(row == s_len - 1, 0.0, pltpu.roll(x, s_len - 1, 0))
    y = cb_ref[...] + prev * cw_ref[0:1, :]
    y = y + x * cw_ref[1:2, :]
    y = y + nxt * cw_ref[2:3, :]
    o_ref[0] = y.astype(BF16)


def _proj_short_conv(h, w, cw, cb, tn=512):
    b, s, d = h.shape
    n = w.shape[1]
    est = (2 * _nbytes((s, d), BF16) + 2 * _nbytes((d, tn), F32) + _nbytes((d, tn), BF16)
           + 2 * _nbytes((s, tn), BF16) + 5 * _nbytes((s, tn), F32))
    return pl.pallas_call(
        _proj_short_conv_kernel,
        name="hyena_proj_short_conv",
        grid=(b, n // tn),
        in_specs=[
            pl.BlockSpec((1, s, d), lambda bi, j: (bi, 0, 0)),
            pl.BlockSpec((d, tn), lambda bi, j: (0, j)),
            pl.BlockSpec((HY_SHORT, tn), lambda bi, j: (0, j)),
            pl.BlockSpec((1, tn), lambda bi, j: (0, j)),
        ],
        out_specs=pl.BlockSpec((1, s, tn), lambda bi, j: (bi, 0, j)),
        out_shape=jax.ShapeDtypeStruct((b, s, n), BF16),
        compiler_params=_cparams(("parallel", "parallel"), est),
    )(h, w, cw, cb.reshape(1, n))


def _filter_mlp_kernel(z_ref, w1_ref, b1_ref, w2_ref, b2_ref, w3_ref, b3_ref, fr_ref, h_ref):
    dot = functools.partial(jnp.dot, preferred_element_type=F32, precision=HIGHEST)
    fr = fr_ref[...]
    h = jnp.sin(fr * (dot(z_ref[...], w1_ref[...]) + b1_ref[...]))
    h = jnp.sin(fr * (dot(h, w2_ref[...]) + b2_ref[...]))
    h_ref[...] = jnp.sin(fr * (dot(h, w3_ref[...]) + b3_ref[...]))


def _filter_mlp(z, w1, b1, w2, b2, w3, b3, freq):
    n_lag = z.shape[0]
    full = lambda shape: pl.BlockSpec(shape, lambda i: (0,) * len(shape))
    est = 12 * _nbytes((n_lag, LANE), F32)
    return pl.pallas_call(
        _filter_mlp_kernel,
        name="hyena_filter_mlp",
        grid=(1,),
        in_specs=[full(z.shape), full(w1.shape), full((1, HY_FFN)), full(w2.shape), full((1, HY_FFN)),
                  full(w3.shape), full((1, HY_FFN)), full((1, HY_FFN))],
        out_specs=full((n_lag, HY_FFN)),
        out_shape=jax.ShapeDtypeStruct((n_lag, HY_FFN), F32),
        compiler_params=_cparams(("arbitrary",), est),
    )(z, w1, b1.reshape(1, -1), w2, b2.reshape(1, -1), w3, b3.reshape(1, -1), freq.reshape(1, -1))


def _filter_gen_kernel(h_ref, w4f_ref, w4b_ref, ksum_ref, kdiff_ref, nyq_ref):
    h = h_ref[...].astype(BF16)
    hf = jnp.dot(h, w4f_ref[...].astype(BF16), preferred_element_type=F32)
    hb = jnp.dot(h, w4b_ref[...].astype(BF16), preferred_element_type=F32)
    n_lag, tn = hf.shape
    row = lax.broadcasted_iota(jnp.int32, (n_lag, tn), 0)
    col = pl.program_id(1) * tn + lax.broadcasted_iota(jnp.int32, (1, tn), 1)
    t = row.astype(F32) / (n_lag - 1)
    max_decay = math.log(HY_TARGET) / HY_DECAY_PCT_SHORT
    min_decay = math.log(HY_TARGET) / HY_DECAY_PCT_LONG
    delta = min_decay + (max_decay - min_decay) * (col.astype(F32) / (HY_D - 1))
    decay = jnp.exp(-t * jnp.abs(delta))
    kf = hf * decay
    kb = jnp.where(row == 0, 0.0, hb * decay)
    inv = 1.0 / (jnp.sum(jnp.abs(kf), axis=0, keepdims=True) + jnp.sum(jnp.abs(kb), axis=0, keepdims=True))
    ksum = (kf + kb) * inv
    ksum_ref[0] = ksum.astype(BF16)
    kdiff_ref[0] = ((kb - kf) * inv).astype(BF16)
    alt = (1 - 2 * (row & 1)).astype(F32)
    nyq_ref[0] = jnp.sum(ksum * alt, axis=0, keepdims=True)


def _filter_gen(h, w4, tn=256):
    n_lag = h.shape[0]
    nblk = HY_D // tn
    est = 10 * _nbytes((n_lag, tn), F32) + 4 * _nbytes((n_lag, LANE), F32) + 4 * _nbytes((n_lag, tn), BF16)
    out = jax.ShapeDtypeStruct((HY_ORDER, n_lag, HY_D), BF16)
    return pl.pallas_call(
        _filter_gen_kernel,
        name="hyena_filter_gen",
        grid=(HY_ORDER, nblk),
        in_specs=[
            pl.BlockSpec((n_lag, HY_FFN), lambda o, j: (0, 0)),
            pl.BlockSpec((HY_FFN, tn), lambda o, j: (0, 2 * o * nblk + j)),
            pl.BlockSpec((HY_FFN, tn), lambda o, j: (0, (2 * o + 1) * nblk + j)),
        ],
        out_specs=[pl.BlockSpec((1, n_lag, tn), lambda o, j: (o, 0, j))] * 2
        + [pl.BlockSpec((1, 1, tn), lambda o, j: (o, 0, j))],
        out_shape=[out, out, jax.ShapeDtypeStruct((HY_ORDER, 1, HY_D), F32)],
        compiler_params=_cparams(("parallel", "parallel"), est),
    )(h, w4, w4)


def _filter_dft_kernel(ct_ref, st_ref, ksum_ref, kdiff_ref, nyq_ref, a_ref, bq_ref, a2_ref):
    re = jnp.dot(ct_ref[...], ksum_ref[0], preferred_element_type=F32)
    im = jnp.dot(st_ref[...], kdiff_ref[0], preferred_element_type=F32)
    tf = re.shape[0]
    row = pl.program_id(2) * tf + lax.broadcasted_iota(jnp.int32, re.shape, 0)
    is0 = row == 0
    wgt = jnp.where(is0, 1.0 / DFT_N, 2.0 / DFT_N)
    a = re * wgt
    a_ref[0] = a
    bq_ref[0] = jnp.where(is0, 0.0, im * wgt)
    a2_ref[0] = jnp.where(is0, nyq_ref[0] * wgt, a)


def _filter_dft(ct, st, ksum, kdiff, nyq, tf=512, tn=512):
    n_f, n_s = ct.shape
    est = (4 * _nbytes((tf, n_s), BF16) + 4 * _nbytes((n_s, tn), BF16) + 6 * _nbytes((tf, tn), F32)
           + 8 * _nbytes((tf, tn), F32))
    out = jax.ShapeDtypeStruct((HY_ORDER, n_f, HY_D), F32)
    return pl.pallas_call(
        _filter_dft_kernel,
        name="hyena_filter_dft",
        grid=(HY_ORDER, HY_D // tn, n_f // tf),
        in_specs=[
            pl.BlockSpec((tf, n_s), lambda o, j, k: (k, 0)),
            pl.BlockSpec((tf, n_s), lambda o, j, k: (k, 0)),
            pl.BlockSpec((1, n_s, tn), lambda o, j, k: (o, 0, j)),
            pl.BlockSpec((1, n_s, tn), lambda o, j, k: (o, 0, j)),
            pl.BlockSpec((1, 1, tn), lambda o, j, k: (o, 0, j)),
        ],
        out_specs=[pl.BlockSpec((1, tf, tn), lambda o, j, k: (o, k, j))] * 3,
        out_shape=[out, out, out],
        compiler_params=_cparams(("parallel", "parallel", "parallel"), est),
    )(ct, st, ksum, kdiff, nyq)


def _alt_sign(idx):
    return (1 - 2 * (idx & 1)).astype(F32)


def _conv_fwd_kernel(ce_ref, se_ref, co_ref, so_ref, flip_ref, u_ref, ae_ref, bqe_ref, a2e_ref, ao_ref, bqo_ref,
                     yre_ref, yse_ref, yro_ref, yso_ref, e_scr, o_scr, row_scr):
    k = pl.program_id(2)
    hh = e_scr.shape[0]

    @pl.when(k == 0)
    def _():
        u_lo = u_ref[0, :hh, :].astype(F32)
        u_rev = jnp.dot(flip_ref[...], u_ref[0, hh:, :], preferred_element_type=F32)
        e = u_lo + u_rev
        e_scr[...] = e.astype(BF16)
        o_scr[...] = (u_lo - u_rev).astype(BF16)
        mid = u_ref[0, hh:hh + 16, :].astype(F32)[0:1]
        alt_s = _alt_sign(lax.broadcasted_iota(jnp.int32, (hh, 1), 0))
        row_scr[0:1, :] = mid
        row_scr[1:2, :] = jnp.sum(e * alt_s, axis=0, keepdims=True) + mid

    tf = ce_ref.shape[0]
    m = k * tf + lax.broadcasted_iota(jnp.int32, (tf, 1), 0)
    corr = _alt_sign(m) * row_scr[0:1, :]
    e, o = e_scr[...], o_scr[...]
    dot = functools.partial(jnp.dot, preferred_element_type=F32)
    uc_e = dot(ce_ref[...], e) + corr
    us_e = jnp.where(m == 0, row_scr[1:2, :], dot(se_ref[...], o))
    uc_o = dot(co_ref[...], o)
    us_o = dot(so_ref[...], e) + corr
    bqe, bqo, ao = bqe_ref[0], bqo_ref[0], ao_ref[0]
    yre_ref[0] = (uc_e * ae_ref[0] + us_e * bqe).astype(BF16)
    yse_ref[0] = (us_e * a2e_ref[0] - uc_e * bqe).astype(BF16)
    yro_ref[0] = (uc_o * ao + us_o * bqo).astype(BF16)
    yso_ref[0] = (us_o * ao - uc_o * bqo).astype(BF16)


def _conv_fwd(tabs, u, u_blk0, spec, order, tf=512, tn=512):
    a, bq, a2 = spec
    b, s, _ = u.shape
    hh = s // 2
    nk = hh // tf
    est = (8 * _nbytes((tf, hh), BF16) + 2 * _nbytes((hh, hh), BF16) + 2 * _nbytes((s, tn), BF16)
           + 10 * _nbytes((tf, tn), F32) + 8 * _nbytes((tf, tn), BF16) + 2 * _nbytes((hh, tn), BF16)
           + 4 * _nbytes((hh, tn), F32) + 8 * _nbytes((tf, tn), F32))
    tab = pl.BlockSpec((tf, hh), lambda bi, j, k: (k, 0))
    even = pl.BlockSpec((1, tf, tn), lambda bi, j, k: (order, k, j))
    odd = pl.BlockSpec((1, tf, tn), lambda bi, j, k: (order, nk + k, j))
    out = jax.ShapeDtypeStruct((b, hh, HY_D), BF16)
    return pl.pallas_call(
        _conv_fwd_kernel,
        name="hyena_conv_fwd",
        grid=(b, HY_D // tn, nk),
        in_specs=[
            tab, tab, tab, tab,
            pl.BlockSpec((hh, hh), lambda bi, j, k: (0, 0)),
            pl.BlockSpec((1, s, tn), lambda bi, j, k: (bi, 0, u_blk0 * (HY_D // tn) + j)),
            even, even, even, odd, odd,
        ],
        out_specs=[pl.BlockSpec((1, tf, tn), lambda bi, j, k: (bi, k, j))] * 4,
        out_shape=[out] * 4,
        scratch_shapes=[pltpu.VMEM((hh, tn), BF16), pltpu.VMEM((hh, tn), BF16), pltpu.VMEM((8, tn), F32)],
        compiler_params=_cparams(("parallel", "parallel", "arbitrary"), est),
    )(tabs["ce"], tabs["se"], tabs["co"], tabs["so"], tabs["flip"], u, a, bq, a2, a, bq)


def _conv_inv_kernel(ce_ref, se_ref, cot_ref, sot_ref, flip_ref, yre_ref, yse_ref, yro_ref, yso_ref,
                     g_ref, u_ref, skip_ref, o_ref, d_scr, row_scr):
    k = pl.program_id(2)
    hh = d_scr.shape[0]
    tt = ce_ref.shape[0]
    yre, yse, yro, yso = yre_ref[0], yse_ref[0], yro_ref[0], yso_ref[0]
    skip = skip_ref[0]
    nyq = yse_ref[0, 0:16, :].astype(F32)[0:1]

    @pl.when(k == 0)
    def _():
        alt_m = _alt_sign(lax.broadcasted_iota(jnp.int32, (hh, 1), 0))
        row_scr[0:1, :] = jnp.sum((yre.astype(F32) + yso.astype(F32)) * alt_m, axis=0, keepdims=True) + nyq

    dot = functools.partial(jnp.dot, preferred_element_type=F32)
    t = k * tt + lax.broadcasted_iota(jnp.int32, (tt, 1), 0)
    pa = dot(ce_ref[...], yre) + dot(sot_ref[...], yso) + _alt_sign(t) * nyq
    pb = dot(se_ref[...], yse) + dot(cot_ref[...], yro)
    rows = pl.ds(pl.multiple_of(k * tt, tt), tt)
    u_lo = u_ref[0, rows, :].astype(F32)
    o_ref[0, rows, :] = (g_ref[0, rows, :].astype(F32) * (pa + pb + u_lo * skip)).astype(BF16)
    d_scr[rows, :] = (pa - pb).astype(BF16)

    @pl.when(k == pl.num_programs(2) - 1)
    def _():
        y_hi = dot(flip_ref[...], d_scr[...])
        r = lax.broadcasted_iota(jnp.int32, (hh, 1), 0)
        y_hi = jnp.where(r == 0, row_scr[0:1, :], y_hi)
        u_hi = u_ref[0, hh:, :].astype(F32)
        o_ref[0, hh:, :] = (g_ref[0, hh:, :].astype(F32) * (y_hi + u_hi * skip)).astype(BF16)


def _conv_inv(tabs, ys4, gsrc, g_blk0, usrc, u_blk0, skip, order, tt=512, tn=512):
    b, hh, _ = ys4[0].shape
    s = 2 * hh
    nblk = HY_D // tn
    est = (8 * _nbytes((tt, hh), BF16) + 2 * _nbytes((hh, hh), BF16) + 8 * _nbytes((hh, tn), BF16)
           + 6 * _nbytes((s, tn), BF16) + _nbytes((hh, tn), BF16) + 8 * _nbytes((tt, tn), F32)
           + 6 * _nbytes((hh, tn), F32))
    tab = pl.BlockSpec((tt, hh), lambda bi, j, k: (k, 0))
    spec_y = pl.BlockSpec((1, hh, tn), lambda bi, j, k: (bi, 0, j))
    return pl.pallas_call(
        _conv_inv_kernel,
        name="hyena_conv_inv",
        grid=(b, nblk, hh // tt),
        in_specs=[
            tab, tab, tab, tab,
            pl.BlockSpec((hh, hh), lambda bi, j, k: (0, 0)),
            spec_y, spec_y, spec_y, spec_y,
            pl.BlockSpec((1, s, tn), lambda bi, j, k: (bi, 0, g_blk0 * nblk + j)),
            pl.BlockSpec((1, s, tn), lambda bi, j, k: (bi, 0, u_blk0 * nblk + j)),
            pl.BlockSpec((1, 1, tn), lambda bi, j, k: (order, 0, j)),
        ],
        out_specs=pl.BlockSpec((1, s, tn), lambda bi, j, k: (bi, 0, j)),
        out_shape=jax.ShapeDtypeStruct((b, s, HY_D), BF16),
        scratch_shapes=[pltpu.VMEM((hh, tn), BF16), pltpu.VMEM((8, tn), F32)],
        compiler_params=_cparams(("parallel", "parallel", "arbitrary"), est),
    )(tabs["ce"], tabs["se"], tabs["cot"], tabs["sot"], tabs["flip"], *ys4, gsrc, usrc,
      skip.reshape(HY_ORDER, 1, HY_D))


def _rope_tables(seq, rot_dim):
    half = rot_dim // 2
    pos = jnp.arange(seq, dtype=F32)
    inv = ROPE_THETA ** (-jnp.arange(0, rot_dim, 2, dtype=F32) / rot_dim)
    ang = pos[:, None] * inv[None, :]
    cos, sin = jnp.cos(ang), jnp.sin(ang)
    rest = LANE - rot_dim
    return cos, sin, half, rest


def _rope_lane_tables(seq, rot_dim, rest_passthrough):
    cos, sin, half, rest = _rope_tables(seq, rot_dim)
    fill = jnp.ones((seq, rest), F32) if rest_passthrough else jnp.zeros((seq, rest), F32)
    zero_h = jnp.zeros((seq, half), F32)
    zero_r = jnp.zeros((seq, rest), F32)
    cos_t = jnp.concatenate([cos, cos, fill], axis=1)
    sin_up = jnp.concatenate([zero_h, sin, zero_r], axis=1)
    sin_dn = jnp.concatenate([-sin, zero_h, zero_r], axis=1)
    return cos_t, sin_up, sin_dn


def _dft_tables(n_half):
    n = 2 * n_half
    blk = 64
    idx = jnp.arange(n_half, dtype=jnp.int32)
    unit = 2.0 * math.pi / n
    ang_a = ((blk * idx[:n_half // blk, None] * idx[None, :]) & (n - 1)).astype(F32) * unit
    ang_b = ((idx[:blk, None] * idx[None, :]) & (n - 1)).astype(F32) * unit
    ca, sa = jnp.cos(ang_a)[:, None, :], jnp.sin(ang_a)[:, None, :]
    cb, sb = jnp.cos(ang_b)[None], jnp.sin(ang_b)[None]
    ct = (ca * cb - sa * sb).reshape(n_half, n_half)
    sn = (sa * cb + ca * sb).reshape(n_half, n_half)
    alt = (1 - 2 * (idx & 1)).astype(F32)
    st = jnp.where(idx[:, None] == 0, alt[None, :], sn)
    hh = n_half // 2
    even_odd = lambda t: jnp.concatenate([t[0::2], t[1::2]], axis=0)
    co, so = ct[1::2, :hh], sn[1::2, :hh]
    flip = (idx[:hh, None] >= 1) & (idx[None, :hh] == hh - idx[:hh, None])
    tabs = dict(ct_eo=even_odd(ct), st_eo=even_odd(st), ce=ct[0::2, :hh], se=sn[0::2, :hh], co=co, so=so,
                cot=co.T, sot=so.T, flip=flip)
    return {name: t.astype(BF16) for name, t in tabs.items()}


def _filter_features(n_lag):
    t = jnp.linspace(0.0, 1.0, n_lag, dtype=F32)[:, None]
    bands = (HY_EMB - 1) // 2
    wpos = 2.0 * math.pi * jnp.arange(n_lag, dtype=F32) / n_lag
    fb = jnp.linspace(1e-4, bands - 1, bands, dtype=F32)
    fw = wpos[:, None] * fb[None, :]
    z = jnp.concatenate([t, jnp.cos(fw), -jnp.sin(fw)], axis=-1)
    return jnp.pad(z, ((0, 0), (0, LANE - HY_EMB)))


def kernel(x, c, ada_mix_w, ada_mix_b, norm_mix_g, ada_mlp_w, ada_mlp_b, norm_mlp_g, w_mlp_in, w_mlp_out, e_w_in, e_q_norm_g, e_kv_norm_g, e_w_uq, e_w_ukv, e_conv_w, e_conv_b, e_f_w1, e_f_b1, e_f_w2, e_f_b2, e_f_w3, e_f_b3, e_f_freq, e_f_w4, e_hy_skip, e_w_out, o_w_qkv, o_sinks, o_w_o, final_norm_g):
    b, s, d = x.shape
    c8 = jnp.pad(c, ((0, 8 - b), (0, 0)))
    mod_mix = _ada(c8, ada_mix_w, ada_mix_b)[:, :b].reshape(DEPTH, b, 1, 3 * d)
    mod_mlp = _ada(c8, ada_mlp_w, ada_mlp_b)[:, :b].reshape(DEPTH, b, 1, 3 * d)

    mla_tabs = _rope_lane_tables(s, MLA_ROPE, rest_passthrough=False)
    gqa_cos, gqa_sin_up, gqa_sin_dn = _rope_lane_tables(s, GQA_ROT, rest_passthrough=True)
    gqa_tabs = (gqa_cos, gqa_sin_up + gqa_sin_dn)
    dft = _dft_tables(s)
    z_feat = _filter_features(s)
    q_scale = (MLA_NOPE + MLA_ROPE) ** -0.5 * math.log2(math.e)

    for l in range(DEPTH):
        i = l // 2
        if l % 2 == 0:
            h = _normmod(x, mod_mix[l], norm_mix_g[l])
            lat = _matmul(h, e_w_in, i, F32, tn=LAT_W, n_cols=LAT_W)
            w_hy = e_w_in[i][:, Q_LORA + KV_LORA + MLA_ROPE:]
            u = _proj_short_conv(h, w_hy, e_conv_w[i], e_conv_b[i])

            w_uq = e_w_uq[i].reshape(Q_LORA, MLA_HEADS, MLA_NOPE + MLA_ROPE)
            wq = jnp.concatenate([
                w_uq[:, :, :MLA_NOPE].reshape(Q_LORA, -1),
                jnp.pad(w_uq[:, :, MLA_NOPE:], ((0, 0), (0, 0), (0, LANE - MLA_ROPE))).reshape(Q_LORA, -1),
            ], axis=1).astype(BF16)
            w_ukv = e_w_ukv[i].reshape(KV_LORA, MLA_HEADS, MLA_NOPE + MLA_V)
            wk = w_ukv[:, :, :MLA_NOPE].reshape(KV_LORA, -1).astype(BF16)
            wv = w_ukv[:, :, MLA_NOPE:].reshape(KV_LORA, -1).astype(BF16)
            q, k, v = _mla_proj(lat, e_q_norm_g[i], e_kv_norm_g[i], wq, wk, wv, mla_tabs, q_scale)
            a_mla = _mla_attn(q, k, v)

            w1 = jnp.pad(e_f_w1[i], ((0, LANE - HY_EMB), (0, 0)))
            h_filt = _filter_mlp(z_feat, w1, e_f_b1[i], e_f_w2[i], e_f_b2[i], e_f_w3[i], e_f_b3[i], e_f_freq[i])
            spec = _filter_dft(dft["ct_eo"], dft["st_eo"], *_filter_gen(h_filt, e_f_w4[i]))
            zc = _conv_inv(dft, _conv_fwd(dft, u, 0, spec, 0), u, 1, u, 0, e_hy_skip[i], 0)
            b_hy = _conv_inv(dft, _conv_fwd(dft, zc, 0, spec, 1), u, 2, zc, 0, e_hy_skip[i], 1)

            x = _proj_res([a_mla, b_hy], e_w_out, i, x, mod_mix[l])
        else:
            h = _normmod(x, mod_mix[l], norm_mix_g[l])
            qkv = _matmul(h, o_w_qkv, i, BF16)
            o = _gqa_attn(qkv, o_sinks[i], gqa_tabs)
            x = _proj_res([o], o_w_o, i, x, mod_mix[l])
        x = _mlp(x, mod_mlp[l], norm_mlp_g[l], w_mlp_in, w_mlp_out, l, final_norm_g, l == DEPTH - 1)
    return x
```

```python
import functools
import math

import jax
import jax.numpy as jnp
from jax import lax
from jax.experimental import pallas as pl
from jax.experimental.pallas import tpu as pltpu

F32 = jnp.float32
BF16 = jnp.bfloat16
HIGHEST = lax.Precision.HIGHEST

D_MODEL = 2048
BATCH = 4
SEQ = 2048
DEPTH = 4
RMS_EPS = 1e-6
ROPE_THETA = 500000.0
NEG_INF = -1e30
MLA_HEADS = 8
MLA_NOPE = 128
MLA_ROPE = 64
MLA_V = 128
Q_LORA = 512
KV_LORA = 256
HY_D = 1024
HY_ORDER = 2
HY_SHORT = 3
HY_EMB = 33
HY_FFN = 64
HY_DECAY_PCT_SHORT = 0.3
HY_DECAY_PCT_LONG = 1.5
HY_TARGET = 1e-2
GQA_HEADS = 16
GQA_KV_HEADS = 4
GQA_HEAD_DIM = 128
GQA_ROT = GQA_HEAD_DIM // 4
GQA_GROUP = GQA_HEADS // GQA_KV_HEADS
WINDOW = 128
BAND = 128
D_FF = 4 * D_MODEL
LAT_W = Q_LORA + KV_LORA + 128
MLA_QK = 256
DFT_N = 2 * SEQ
CONV_SUB_ROWS = 256

LANE = 128
V7X_VMEM_BYTES = 64 * 1024 * 1024
V7X_VMEM_BUDGET = 56 * 1024 * 1024


def _cparams(semantics, est_bytes):
    limit = int(min(V7X_VMEM_BUDGET, max(32 * 1024 * 1024, est_bytes * 3 // 2)))
    return pltpu.CompilerParams(dimension_semantics=semantics, vmem_limit_bytes=limit)


def _nbytes(shape, dtype):
    return math.prod(shape) * jnp.dtype(dtype).itemsize


def _rms(x, g):
    ms = jnp.mean(x * x, axis=-1, keepdims=True)
    return x * lax.rsqrt(ms + RMS_EPS) * g


def _ada_kernel(c_ref, w_ref, b_ref, o_ref):
    cv = c_ref[...]
    s = cv * (1.0 / (1.0 + jnp.exp(-cv)))
    o_ref[0] = jnp.dot(s.astype(BF16), w_ref[0].astype(BF16), preferred_element_type=F32) + b_ref[0]


def _ada(c8, w, b, tn=1536):
    n_l, d, n = w.shape
    est = 2 * _nbytes((d, tn), F32) + _nbytes((d, tn), BF16) + 4 * _nbytes((8, tn), F32) + 2 * _nbytes((8, d), F32)
    return pl.pallas_call(
        _ada_kernel,
        name="ada_modulation",
        grid=(n_l, n // tn),
        in_specs=[
            pl.BlockSpec((8, d), lambda l, j: (0, 0)),
            pl.BlockSpec((1, d, tn), lambda l, j: (l, 0, j)),
            pl.BlockSpec((1, 1, tn), lambda l, j: (l, 0, j)),
        ],
        out_specs=pl.BlockSpec((1, 8, tn), lambda l, j: (l, 0, j)),
        out_shape=jax.ShapeDtypeStruct((n_l, 8, n), F32),
        compiler_params=_cparams(("parallel", "parallel"), est),
    )(c8, w, b.reshape(n_l, 1, n))


def _normmod_rows(x_ref, shift_ref, scale_ref, g_ref, h_ref, rows):
    gain = g_ref[...] * (1.0 + scale_ref[0])
    sh = shift_ref[0]
    tm = h_ref.shape[0]

    def body(r, carry):
        sl = pl.ds(pl.multiple_of(r * rows, rows), rows)
        xv = x_ref[0, sl, :]
        inv = lax.rsqrt(jnp.mean(xv * xv, axis=-1, keepdims=True) + RMS_EPS)
        h_ref[sl, :] = (xv * inv * gain + sh).astype(BF16)
        return carry

    lax.fori_loop(0, tm // rows, body, 0)


def _normmod_kernel(x_ref, shift_ref, scale_ref, g_ref, h_ref):
    _normmod_rows(x_ref, shift_ref, scale_ref, g_ref, h_ref.at[0], 128)


def _normmod(x, mod, g, tm=512):
    b, s, d = x.shape
    est = 2 * _nbytes((tm, d), F32) + 2 * _nbytes((tm, d), BF16) + 8 * _nbytes((128, d), F32)
    return pl.pallas_call(
        _normmod_kernel,
        name="normmod",
        grid=(b, s // tm),
        in_specs=[
            pl.BlockSpec((1, tm, d), lambda bi, i: (bi, i, 0)),
            pl.BlockSpec((1, 1, d), lambda bi, i: (bi, 0, 0)),
            pl.BlockSpec((1, 1, d), lambda bi, i: (bi, 0, 1)),
            pl.BlockSpec((1, d), lambda bi, i: (0, 0)),
        ],
        out_specs=pl.BlockSpec((1, tm, d), lambda bi, i: (bi, i, 0)),
        out_shape=jax.ShapeDtypeStruct((b, s, d), BF16),
        compiler_params=_cparams(("parallel", "parallel"), est),
    )(x, mod, mod, g.reshape(1, d))


def _matmul_kernel(h_ref, w_ref, o_ref):
    o_ref[0] = jnp.dot(h_ref[0], w_ref[...].astype(BF16), preferred_element_type=F32).astype(o_ref.dtype)


def _matmul(h, w_all, layer, out_dtype, tm=2048, tn=512, n_cols=None):
    b, s, d = h.shape
    n = w_all.shape[2] if n_cols is None else n_cols
    tn = min(tn, n)
    est = (2 * _nbytes((tm, d), BF16) + 2 * _nbytes((d, tn), F32) + _nbytes((d, tn), BF16)
           + 3 * _nbytes((tm, tn), F32))
    return pl.pallas_call(
        _matmul_kernel,
        name="matmul",
        grid=(b, s // tm, n // tn),
        in_specs=[
            pl.BlockSpec((1, tm, d), lambda bi, i, j: (bi, i, 0)),
            pl.BlockSpec((None, d, tn), lambda bi, i, j: (layer, 0, j)),
        ],
        out_specs=pl.BlockSpec((1, tm, tn), lambda bi, i, j: (bi, i, j)),
        out_shape=jax.ShapeDtypeStruct((b, s, n), out_dtype),
        compiler_params=_cparams(("parallel", "parallel", "parallel"), est),
    )(h, w_all)


def _mlp_kernel(x_ref, shift_ref, scale_ref, gate_ref, g_ref, w1_ref, w2_ref, fg_ref, o_ref, h_ref, *, final_norm):
    f = pl.program_id(2)

    @pl.when(f == 0)
    def _():
        _normmod_rows(x_ref, shift_ref, scale_ref, g_ref, h_ref, 128)

        o_ref[...] = jnp.zeros_like(o_ref)

    a = jnp.dot(h_ref[...], w1_ref[...].astype(BF16), preferred_element_type=F32)
    a = jnp.square(jnp.maximum(a, 0.0)).astype(BF16)
    o_ref[0] += jnp.dot(a, w2_ref[...].astype(BF16), preferred_element_type=F32)

    @pl.when(f == pl.num_programs(2) - 1)
    def _():
        gate = gate_ref[0]
        fg = fg_ref[...]
        rows = 128

        def body(r, carry):
            sl = pl.ds(pl.multiple_of(r * rows, rows), rows)
            y = x_ref[0, sl, :] + gate * o_ref[0, sl, :]
            if final_norm:
                y = _rms(y, fg)
            o_ref[0, sl, :] = y
            return carry

        lax.fori_loop(0, o_ref.shape[1] // rows, body, 0)


def _mlp(x, mod, g, w1_all, w2_all, layer, final_g, final_norm, tm=1024, tf=512):
    b, s, d = x.shape
    ff = w1_all.shape[2]
    est = (4 * _nbytes((tm, d), F32) + _nbytes((tm, d), BF16) + 4 * _nbytes((d, tf), F32)
           + 2 * _nbytes((d, tf), BF16) + 2 * _nbytes((tm, tf), F32))
    return pl.pallas_call(
        functools.partial(_mlp_kernel, final_norm=final_norm),
        name="mlp_relu2",
        grid=(b, s // tm, ff // tf),
        in_specs=[
            pl.BlockSpec((1, tm, d), lambda bi, i, f: (bi, i, 0)),
            pl.BlockSpec((1, 1, d), lambda bi, i, f: (bi, 0, 0)),
            pl.BlockSpec((1, 1, d), lambda bi, i, f: (bi, 0, 1)),
            pl.BlockSpec((1, 1, d), lambda bi, i, f: (bi, 0, 2)),
            pl.BlockSpec((1, d), lambda bi, i, f: (0, 0)),
            pl.BlockSpec((None, d, tf), lambda bi, i, f: (layer, 0, f)),
            pl.BlockSpec((None, tf, d), lambda bi, i, f: (layer, f, 0)),
            pl.BlockSpec((1, d), lambda bi, i, f: (0, 0)),
        ],
        out_specs=pl.BlockSpec((1, tm, d), lambda bi, i, f: (bi, i, 0)),
        out_shape=jax.ShapeDtypeStruct((b, s, d), F32),
        scratch_shapes=[pltpu.VMEM((tm, d), BF16)],
        compiler_params=_cparams(("parallel", "parallel", "arbitrary"), est),
    )(x, mod, mod, mod, g.reshape(1, d), w1_all, w2_all, final_g.reshape(1, d))


def _proj_res_kernel(*refs, n_in):
    a_refs, w_refs = refs[:n_in], refs[n_in:2 * n_in]
    x_ref, gate_ref, o_ref = refs[2 * n_in:]
    acc = None
    for a_ref, w_ref in zip(a_refs, w_refs):
        p = jnp.dot(a_ref[0], w_ref[...].astype(BF16), preferred_element_type=F32)
        acc = p if acc is None else acc + p
    o_ref[0] = x_ref[0] + gate_ref[0] * acc


def _proj_res(a_list, w_all, layer, x, mod, tm=2048, tn=512):
    b, s, d = x.shape
    n_in = len(a_list)
    kk = a_list[0].shape[-1]
    assert all(a.shape[-1] == kk for a in a_list) and w_all.shape[1] == n_in * kk
    est = (n_in * (2 * _nbytes((tm, kk), BF16) + 2 * _nbytes((kk, tn), F32) + _nbytes((kk, tn), BF16))
           + 6 * _nbytes((tm, tn), F32))
    in_specs = [pl.BlockSpec((1, tm, kk), lambda bi, i, j: (bi, i, 0)) for _ in a_list]
    in_specs += [pl.BlockSpec((None, kk, tn), functools.partial(lambda bi, i, j, r: (layer, r, j), r=r))
                 for r in range(n_in)]
    in_specs += [
        pl.BlockSpec((1, tm, tn), lambda bi, i, j: (bi, i, j)),
        pl.BlockSpec((1, 1, tn), lambda bi, i, j: (bi, 0, 2 * (d // tn) + j)),
    ]
    return pl.pallas_call(
        functools.partial(_proj_res_kernel, n_in=n_in),
        name="proj_residual",
        grid=(b, s // tm, d // tn),
        in_specs=in_specs,
        out_specs=pl.BlockSpec((1, tm, tn), lambda bi, i, j: (bi, i, j)),
        out_shape=jax.ShapeDtypeStruct((b, s, d), F32),
        compiler_params=_cparams(("parallel", "parallel", "parallel"), est),
    )(*a_list, *([w_all] * n_in), x, mod)


def _rot_lanes(blk, cos_t, sin_up, sin_dn, half):
    return (blk * cos_t + pltpu.roll(blk, half, 1) * sin_up + pltpu.roll(blk, LANE - half, 1) * sin_dn)


def _mla_proj_kernel(lat_ref, gq_ref, gkv_ref, wq_ref, wk_ref, wv_ref, cos_ref, sup_ref, sdn_ref,
                     q_ref, k_ref, v_ref, *, q_scale):
    lat = lat_ref[0]
    qn = _rms(lat[:, :Q_LORA], gq_ref[...]).astype(BF16)
    kvn = _rms(lat[:, Q_LORA:Q_LORA + KV_LORA], gkv_ref[...]).astype(BF16)
    cos_t, sin_up, sin_dn = cos_ref[...], sup_ref[...], sdn_ref[...]
    half = MLA_ROPE // 2
    q = jnp.dot(qn, wq_ref[...], preferred_element_type=F32) * q_scale
    kn = jnp.dot(kvn, wk_ref[...], preferred_element_type=F32)
    v = jnp.dot(kvn, wv_ref[...], preferred_element_type=F32).astype(BF16)
    ones = jnp.ones((v.shape[0], MLA_V), BF16)
    kr = _rot_lanes(lat[:, Q_LORA + KV_LORA:], cos_t, sin_up, sin_dn, half).astype(BF16)
    nope_w = MLA_HEADS * MLA_NOPE
    for h in range(MLA_HEADS):
        c0 = h * MLA_QK
        v_ref[0, :, 2 * h * MLA_V:(2 * h + 1) * MLA_V] = v[:, h * MLA_V:(h + 1) * MLA_V]
        v_ref[0, :, (2 * h + 1) * MLA_V:(2 * h + 2) * MLA_V] = ones
        q_ref[0, :, c0:c0 + LANE] = q[:, h * LANE:(h + 1) * LANE].astype(BF16)
        qr = q[:, nope_w + h * LANE:nope_w + (h + 1) * LANE]
        q_ref[0, :, c0 + LANE:c0 + 2 * LANE] = _rot_lanes(qr, cos_t, sin_up, sin_dn, half).astype(BF16)
        k_ref[0, :, c0:c0 + LANE] = kn[:, h * LANE:(h + 1) * LANE].astype(BF16)
        k_ref[0, :, c0 + LANE:c0 + 2 * LANE] = kr


def _mla_proj(lat, gq, gkv, wq, wk, wv, tabs, q_scale, tm=512):
    b, s, _ = lat.shape
    hq = MLA_HEADS * MLA_QK
    hv = MLA_HEADS * 2 * MLA_V
    est = (2 * _nbytes((tm, LAT_W), F32) + 2 * _nbytes(wq.shape, BF16) + 2 * _nbytes(wk.shape, BF16)
           + 2 * _nbytes(wv.shape, BF16) + 4 * _nbytes((tm, hq), BF16) + 2 * _nbytes((tm, hv), BF16)
           + 3 * _nbytes((tm, hq), F32))
    full = lambda shape: pl.BlockSpec(shape, lambda bi, i: (0,) * len(shape))
    tab = pl.BlockSpec((tm, LANE), lambda bi, i: (i, 0))
    return pl.pallas_call(
        functools.partial(_mla_proj_kernel, q_scale=q_scale),
        name="mla_proj",
        grid=(b, s // tm),
        in_specs=[
            pl.BlockSpec((1, tm, LAT_W), lambda bi, i: (bi, i, 0)),
            full((1, Q_LORA)), full((1, KV_LORA)), full(wq.shape), full(wk.shape), full(wv.shape),
            tab, tab, tab,
        ],
        out_specs=[
            pl.BlockSpec((1, tm, hq), lambda bi, i: (bi, i, 0)),
            pl.BlockSpec((1, tm, hq), lambda bi, i: (bi, i, 0)),
            pl.BlockSpec((1, tm, hv), lambda bi, i: (bi, i, 0)),
        ],
        out_shape=[
            jax.ShapeDtypeStruct((b, s, hq), BF16),
            jax.ShapeDtypeStruct((b, s, hq), BF16),
            jax.ShapeDtypeStruct((b, s, hv), BF16),
        ],
        compiler_params=_cparams(("parallel", "parallel"), est),
    )(lat, gq.reshape(1, -1), gkv.reshape(1, -1), wq, wk, wv, *tabs)


def _mla_attn_kernel(q_ref, k_ref, v_ref, o_ref, *, chain_rows):
    n_chains = q_ref.shape[1] // chain_rows

    def scores(c):
        rs = slice(c * chain_rows, (c + 1) * chain_rows)
        return lax.dot_general(q_ref[0, rs, :], k_ref[0], (((1,), (1,)), ((), ())), preferred_element_type=F32)

    s_next = scores(0)
    for c in range(n_chains):
        s = s_next
        if c + 1 < n_chains:
            s_next = scores(c + 1)
        m = jnp.max(s, axis=-1, keepdims=True)
        p = jnp.exp2(s - m).astype(BF16)
        oe = jnp.dot(p, v_ref[0], preferred_element_type=F32)
        o_ref[0, c * chain_rows:(c + 1) * chain_rows, :] = (oe[:, :MLA_V] / oe[:, MLA_V:]).astype(BF16)


def _mla_attn(q, k, v, tq=2048, chain_rows=1024):
    b, s, _ = q.shape
    est = (2 * _nbytes((tq, MLA_QK), BF16) + 2 * _nbytes((s, MLA_QK), BF16) + 2 * _nbytes((s, 2 * MLA_V), BF16)
           + 2 * _nbytes((tq, MLA_V), BF16) + 5 * _nbytes((chain_rows, s), F32))
    return pl.pallas_call(
        functools.partial(_mla_attn_kernel, chain_rows=chain_rows),
        name="mla_attn",
        grid=(b, MLA_HEADS, s // tq),
        in_specs=[
            pl.BlockSpec((1, tq, MLA_QK), lambda bi, h, i: (bi, i, h)),
            pl.BlockSpec((1, s, MLA_QK), lambda bi, h, i: (bi, 0, h)),
            pl.BlockSpec((1, s, 2 * MLA_V), lambda bi, h, i: (bi, 0, h)),
        ],
        out_specs=pl.BlockSpec((1, tq, MLA_V), lambda bi, h, i: (bi, i, h)),
        out_shape=jax.ShapeDtypeStruct((b, s, MLA_HEADS * MLA_V), BF16),
        compiler_params=_cparams(("parallel", "parallel", "parallel"), est),
    )(q, k, v)


def _gqa_kernel(q_ref, k_ref, v_ref, sink_ref, cos_ref, sin_ref, swap_ref, o_ref, k_scr, v_scr, *, scale):
    s_len = k_scr.shape[0]
    n_blocks = s_len // BAND
    rows = GQA_GROUP * BAND
    swap = swap_ref[...]

    def rot(x, cos_t, sin_t):
        return x.astype(F32) * cos_t + jnp.dot(x, swap, preferred_element_type=F32) * sin_t

    k_scr[...] = rot(k_ref[0], cos_ref[...], sin_ref[...]).astype(BF16)
    v_scr[:, :LANE] = v_ref[0]
    v_scr[:, LANE:] = jnp.ones((s_len, LANE), BF16)
    sink = sink_ref[0]
    rel = (lax.broadcasted_iota(jnp.int32, (rows, 3 * BAND), 1) - BAND
           - (lax.broadcasted_iota(jnp.int32, (rows, 3 * BAND), 0) & (BAND - 1)))
    bias = jnp.where(jnp.abs(rel) <= WINDOW, 0.0, NEG_INF)

    def window(n):
        r0 = n * BAND
        return max(0, r0 - BAND), min(s_len, r0 + 2 * BAND)

    def scores(n):
        r0 = n * BAND
        cos_t = jnp.concatenate([cos_ref[r0:r0 + BAND, :]] * GQA_GROUP, axis=0)
        sin_t = jnp.concatenate([sin_ref[r0:r0 + BAND, :]] * GQA_GROUP, axis=0)
        q_st = jnp.concatenate([q_ref[0, r0:r0 + BAND, g * LANE:(g + 1) * LANE] for g in range(GQA_GROUP)], axis=0)
        q_st = (rot(q_st, cos_t, sin_t) * scale).astype(BF16)
        lo, hi = window(n)
        c0 = lo - (r0 - BAND)
        s = lax.dot_general(q_st, k_scr[lo:hi, :], (((1,), (1,)), ((), ())), preferred_element_type=F32)
        return s + bias[:, c0:c0 + hi - lo]

    s_next = scores(0)
    for n in range(n_blocks):
        s = s_next
        if n + 1 < n_blocks:
            s_next = scores(n + 1)
        r0 = n * BAND
        lo, hi = window(n)
        m = jnp.maximum(jnp.max(s, axis=-1, keepdims=True), sink)
        p = jnp.exp(s - m).astype(BF16)
        oe = jnp.dot(p, v_scr[lo:hi, :], preferred_element_type=F32)
        o = oe[:, :LANE] / (oe[:, LANE:] + jnp.exp(sink - m))
        for g in range(GQA_GROUP):
            o_ref[0, r0:r0 + BAND, g * LANE:(g + 1) * LANE] = o[g * BAND:(g + 1) * BAND, :].astype(BF16)


def _gqa_attn(qkv, sinks, tabs):
    b, s, _ = qkv.shape
    gw = GQA_GROUP * GQA_HEAD_DIM
    sink_col = jnp.broadcast_to(sinks.astype(F32).reshape(GQA_KV_HEADS, GQA_GROUP, 1, 1),
                                (GQA_KV_HEADS, GQA_GROUP, BAND, 1)).reshape(GQA_KV_HEADS, GQA_GROUP * BAND, 1)
    est = (4 * _nbytes((s, gw), BF16) + 7 * _nbytes((s, LANE), BF16) + 6 * _nbytes((s, LANE), F32)
           + 8 * _nbytes((GQA_GROUP * BAND, 3 * BAND), F32))
    tab = pl.BlockSpec((s, LANE), lambda bi, h: (0, 0))
    cos_t, sin_t = tabs
    half = GQA_ROT // 2
    lane = jnp.arange(LANE)
    swap = (((lane[None, :] < half) & (lane[:, None] == lane[None, :] + half))
            | ((lane[None, :] >= half) & (lane[None, :] < 2 * half) & (lane[:, None] == lane[None, :] - half)))
    return pl.pallas_call(
        functools.partial(_gqa_kernel, scale=GQA_HEAD_DIM ** -0.5),
        name="gqa_window_attn",
        grid=(b, GQA_KV_HEADS),
        in_specs=[
            pl.BlockSpec((1, s, gw), lambda bi, h: (bi, 0, h)),
            pl.BlockSpec((1, s, LANE), lambda bi, h: (bi, 0, GQA_HEADS + h)),
            pl.BlockSpec((1, s, LANE), lambda bi, h: (bi, 0, GQA_HEADS + GQA_KV_HEADS + h)),
            pl.BlockSpec((1, GQA_GROUP * BAND, 1), lambda bi, h: (h, 0, 0)),
            tab, tab,
            pl.BlockSpec((LANE, LANE), lambda bi, h: (0, 0)),
        ],
        out_specs=pl.BlockSpec((1, s, gw), lambda bi, h: (bi, 0, h)),
        out_shape=jax.ShapeDtypeStruct((b, s, GQA_HEADS * GQA_HEAD_DIM), BF16),
        scratch_shapes=[pltpu.VMEM((s, LANE), BF16), pltpu.VMEM((s, 2 * LANE), BF16)],
        compiler_params=_cparams(("parallel", "parallel"), est),
    )(qkv, qkv, qkv, sink_col, cos_t, sin_t, swap.astype(BF16))


def _proj_short_conv_kernel(h_ref, w_ref, cw_ref, cb_ref, o_ref):
    x = jnp.dot(h_ref[0], w_ref[...].astype(BF16), preferred_element_type=F32)
    s_len = x.shape[0]
    row = lax.broadcasted_iota(jnp.int32, x.shape, 0)
    prev = jnp.where(row == 0, 0.0, pltpu.roll(x, 1, 0))
    nxt = jnp.where(row == s_len - 1, 0.0, pltpu.roll(x, s_len - 1, 0))
    y = cb_ref[...] + prev * cw_ref[0:1, :]
    y = y + x * cw_ref[1:2, :]
    y = y + nxt * cw_ref[2:3, :]
    o_ref[0] = y.astype(BF16)


def _proj_short_conv(h, w, cw, cb, tn=512):
    b, s, d = h.shape
    n = w.shape[1]
    est = (2 * _nbytes((s, d), BF16) + 2 * _nbytes((d, tn), F32) + _nbytes((d, tn), BF16)
           + 2 * _nbytes((s, tn), BF16) + 5 * _nbytes((s, tn), F32))
    return pl.pallas_call(
        _proj_short_conv_kernel,
        name="hyena_proj_short_conv",
        grid=(b, n // tn),
        in_specs=[
            pl.BlockSpec((1, s, d), lambda bi, j: (bi, 0, 0)),
            pl.BlockSpec((d, tn), lambda bi, j: (0, j)),
            pl.BlockSpec((HY_SHORT, tn), lambda bi, j: (0, j)),
            pl.BlockSpec((1, tn), lambda bi, j: (0, j)),
        ],
        out_specs=pl.BlockSpec((1, s, tn), lambda bi, j: (bi, 0, j)),
        out_shape=jax.ShapeDtypeStruct((b, s, n), BF16),
        compiler_params=_cparams(("parallel", "parallel"), est),
    )(h, w, cw, cb.reshape(1, n))


def _filter_mlp_kernel(z_ref, w1_ref, b1_ref, w2_ref, b2_ref, w3_ref, b3_ref, fr_ref, h_ref):
    dot = functools.partial(jnp.dot, preferred_element_type=F32, precision=HIGHEST)
    fr = fr_ref[...]
    h = jnp.sin(fr * (dot(z_ref[...], w1_ref[...]) + b1_ref[...]))
    h = jnp.sin(fr * (dot(h, w2_ref[...]) + b2_ref[...]))
    h_ref[...] = jnp.sin(fr * (dot(h, w3_ref[...]) + b3_ref[...]))


def _filter_mlp(z, w1, b1, w2, b2, w3, b3, freq):
    n_lag = z.shape[0]
    full = lambda shape: pl.BlockSpec(shape, lambda i: (0,) * len(shape))
    est = 12 * _nbytes((n_lag, LANE), F32)
    return pl.pallas_call(
        _filter_mlp_kernel,
        name="hyena_filter_mlp",
        grid=(1,),
        in_specs=[full(z.shape), full(w1.shape), full((1, HY_FFN)), full(w2.shape), full((1, HY_FFN)),
                  full(w3.shape), full((1, HY_FFN)), full((1, HY_FFN))],
        out_specs=full((n_lag, HY_FFN)),
        out_shape=jax.ShapeDtypeStruct((n_lag, HY_FFN), F32),
        compiler_params=_cparams(("arbitrary",), est),
    )(z, w1, b1.reshape(1, -1), w2, b2.reshape(1, -1), w3, b3.reshape(1, -1), freq.reshape(1, -1))


def _filter_gen_kernel(h_ref, w4f_ref, w4b_ref, ksum_ref, kdiff_ref, nyq_ref):
    h = h_ref[...].astype(BF16)
    hf = jnp.dot(h, w4f_ref[...].astype(BF16), preferred_element_type=F32)
    hb = jnp.dot(h, w4b_ref[...].astype(BF16), preferred_element_type=F32)
    n_lag, tn = hf.shape
    row = lax.broadcasted_iota(jnp.int32, (n_lag, tn), 0)
    col = pl.program_id(1) * tn + lax.broadcasted_iota(jnp.int32, (1, tn), 1)
    t = row.astype(F32) / (n_lag - 1)
    max_decay = math.log(HY_TARGET) / HY_DECAY_PCT_SHORT
    min_decay = math.log(HY_TARGET) / HY_DECAY_PCT_LONG
    delta = min_decay + (max_decay - min_decay) * (col.astype(F32) / (HY_D - 1))
    decay = jnp.exp(-t * jnp.abs(delta))
    kf = hf * decay
    kb = jnp.where(row == 0, 0.0, hb * decay)
    inv = 1.0 / (jnp.sum(jnp.abs(kf), axis=0, keepdims=True) + jnp.sum(jnp.abs(kb), axis=0, keepdims=True))
    ksum = (kf + kb) * inv
    ksum_ref[0] = ksum.astype(BF16)
    kdiff_ref[0] = ((kb - kf) * inv).astype(BF16)
    alt = (1 - 2 * (row & 1)).astype(F32)
    nyq_ref[0] = jnp.sum(ksum * alt, axis=0, keepdims=True) * (1.0 / DFT_N)


def _filter_gen(h, w4, tn=256):
    n_lag = h.shape[0]
    nblk = HY_D // tn
    est = 10 * _nbytes((n_lag, tn), F32) + 4 * _nbytes((n_lag, LANE), F32) + 4 * _nbytes((n_lag, tn), BF16)
    out = jax.ShapeDtypeStruct((HY_ORDER, n_lag, HY_D), BF16)
    return pl.pallas_call(
        _filter_gen_kernel,
        name="hyena_filter_gen",
        grid=(HY_ORDER, nblk),
        in_specs=[
            pl.BlockSpec((n_lag, HY_FFN), lambda o, j: (0, 0)),
            pl.BlockSpec((HY_FFN, tn), lambda o, j: (0, 2 * o * nblk + j)),
            pl.BlockSpec((HY_FFN, tn), lambda o, j: (0, (2 * o + 1) * nblk + j)),
        ],
        out_specs=[pl.BlockSpec((1, n_lag, tn), lambda o, j: (o, 0, j))] * 2
        + [pl.BlockSpec((1, 1, tn), lambda o, j: (o, 0, j))],
        out_shape=[out, out, jax.ShapeDtypeStruct((HY_ORDER, 1, HY_D), F32)],
        compiler_params=_cparams(("parallel", "parallel"), est),
    )(h, w4, w4)


def _filter_dft_kernel(ct_ref, st_ref, ksum_ref, kdiff_ref, a_ref, bq_ref):
    re = jnp.dot(ct_ref[...], ksum_ref[0], preferred_element_type=F32)
    im = jnp.dot(st_ref[...], kdiff_ref[0], preferred_element_type=F32)
    tf = re.shape[0]
    row = pl.program_id(2) * tf + lax.broadcasted_iota(jnp.int32, re.shape, 0)
    is0 = row == 0
    wgt = jnp.where(is0, 1.0 / DFT_N, 2.0 / DFT_N)
    a_ref[0] = (re * wgt).astype(BF16)
    bq_ref[0] = jnp.where(is0, 0.0, im * wgt).astype(BF16)


def _filter_dft(ct, st, ksum, kdiff, tf=512, tn=512):
    n_f, n_s = ct.shape
    est = (4 * _nbytes((tf, n_s), BF16) + 4 * _nbytes((n_s, tn), BF16) + 4 * _nbytes((tf, tn), BF16)
           + 8 * _nbytes((tf, tn), F32))
    out = jax.ShapeDtypeStruct((HY_ORDER, n_f, HY_D), BF16)
    return pl.pallas_call(
        _filter_dft_kernel,
        name="hyena_filter_dft",
        grid=(HY_ORDER, HY_D // tn, n_f // tf),
        in_specs=[
            pl.BlockSpec((tf, n_s), lambda o, j, k: (k, 0)),
            pl.BlockSpec((tf, n_s), lambda o, j, k: (k, 0)),
            pl.BlockSpec((1, n_s, tn), lambda o, j, k: (o, 0, j)),
            pl.BlockSpec((1, n_s, tn), lambda o, j, k: (o, 0, j)),
        ],
        out_specs=[pl.BlockSpec((1, tf, tn), lambda o, j, k: (o, k, j))] * 2,
        out_shape=[out, out],
        compiler_params=_cparams(("parallel", "parallel", "parallel"), est),
    )(ct, st, ksum, kdiff)


def _alt_sign(idx):
    return (1 - 2 * (idx & 1)).astype(F32)


def _conv_fwd_kernel(ce_ref, se_ref, co_ref, so_ref, flip_ref, u_ref, ae_ref, bqe_ref, ao_ref, bqo_ref, knyq_ref,
                     yre_ref, yse_ref, yro_ref, yso_ref, e_scr, o_scr, row_scr):
    k = pl.program_id(2)
    hh = e_scr.shape[0]

    @pl.when(k == 0)
    def _():
        u_lo = u_ref[0, :hh, :].astype(F32)
        u_rev = jnp.dot(flip_ref[...], u_ref[0, hh:, :], preferred_element_type=F32)
        e = u_lo + u_rev
        e_scr[...] = e.astype(BF16)
        o_scr[...] = (u_lo - u_rev).astype(BF16)
        mid = u_ref[0, hh:hh + 16, :].astype(F32)[0:1]
        alt_s = _alt_sign(lax.broadcasted_iota(jnp.int32, (hh, 1), 0))
        row_scr[0:1, :] = mid
        row_scr[1:2, :] = jnp.sum(e * alt_s, axis=0, keepdims=True) + mid

    tf = yre_ref.shape[1]
    e, o = e_scr[...], o_scr[...]
    dot = functools.partial(jnp.dot, preferred_element_type=F32)
    for r0 in range(0, tf, CONV_SUB_ROWS):
        rs = slice(r0, r0 + CONV_SUB_ROWS)
        trows = pl.ds(pl.multiple_of(k * tf + r0, CONV_SUB_ROWS), CONV_SUB_ROWS)
        m = k * tf + r0 + lax.broadcasted_iota(jnp.int32, (CONV_SUB_ROWS, 1), 0)
        corr = _alt_sign(m) * row_scr[0:1, :]
        uc = dot(ce_ref[trows, :], e) + corr
        us = dot(se_ref[trows, :], o)
        a, bq = ae_ref[0, rs, :].astype(F32), bqe_ref[0, rs, :].astype(F32)
        yre_ref[0, rs, :] = (uc * a + us * bq).astype(BF16)
        ys = us * a - uc * bq
        if r0 == 0:
            ys = jnp.where(m == 0, row_scr[1:2, :] * knyq_ref[0], ys)
        yse_ref[0, rs, :] = ys.astype(BF16)
        uc = dot(co_ref[trows, :], o)
        us = dot(so_ref[trows, :], e) + corr
        a, bq = ao_ref[0, rs, :].astype(F32), bqo_ref[0, rs, :].astype(F32)
        yro_ref[0, rs, :] = (uc * a + us * bq).astype(BF16)
        yso_ref[0, rs, :] = (us * a - uc * bq).astype(BF16)


def _conv_fwd(tabs, u, u_blk0, spec, order, tf=1024, tn=512):
    a, bq, knyq = spec
    b, s, _ = u.shape
    hh = s // 2
    nk = hh // tf
    est = (10 * _nbytes((hh, hh), BF16) + 2 * _nbytes((s, tn), BF16) + 16 * _nbytes((tf, tn), BF16)
           + 2 * _nbytes((hh, tn), BF16) + 4 * _nbytes((hh, tn), F32) + 12 * _nbytes((CONV_SUB_ROWS, tn), F32))
    tab = pl.BlockSpec((hh, hh), lambda bi, j, k: (0, 0))
    even = pl.BlockSpec((1, tf, tn), lambda bi, j, k: (order, k, j))
    odd = pl.BlockSpec((1, tf, tn), lambda bi, j, k: (order, nk + k, j))
    out = jax.ShapeDtypeStruct((b, hh, HY_D), BF16)
    return pl.pallas_call(
        _conv_fwd_kernel,
        name="hyena_conv_fwd",
        grid=(b, HY_D // tn, nk),
        in_specs=[
            tab, tab, tab, tab, tab,
            pl.BlockSpec((1, s, tn), lambda bi, j, k: (bi, 0, u_blk0 * (HY_D // tn) + j)),
            even, even, odd, odd,
            pl.BlockSpec((1, 1, tn), lambda bi, j, k: (order, 0, j)),
        ],
        out_specs=[pl.BlockSpec((1, tf, tn), lambda bi, j, k: (bi, k, j))] * 4,
        out_shape=[out] * 4,
        scratch_shapes=[pltpu.VMEM((hh, tn), BF16), pltpu.VMEM((hh, tn), BF16), pltpu.VMEM((8, tn), F32)],
        compiler_params=_cparams(("parallel", "parallel", "arbitrary"), est),
    )(tabs["ce"], tabs["se"], tabs["co"], tabs["so"], tabs["flip"], u, a, bq, a, bq, knyq)


def _conv_inv_kernel(ce_ref, se_ref, cot_ref, sot_ref, flip_ref, yre_ref, yse_ref, yro_ref, yso_ref,
                     g_ref, u_ref, skip_ref, o_ref, d_scr, row_scr):
    k = pl.program_id(2)
    hh = d_scr.shape[0]
    tt = ce_ref.shape[0]
    yre, yse, yro, yso = yre_ref[0], yse_ref[0], yro_ref[0], yso_ref[0]
    skip = skip_ref[0]
    nyq = yse_ref[0, 0:16, :].astype(F32)[0:1]

    @pl.when(k == 0)
    def _():
        alt_m = _alt_sign(lax.broadcasted_iota(jnp.int32, (hh, 1), 0))
        row_scr[0:1, :] = jnp.sum((yre.astype(F32) + yso.astype(F32)) * alt_m, axis=0, keepdims=True) + nyq

    dot = functools.partial(jnp.dot, preferred_element_type=F32)
    for r0 in range(0, tt, CONV_SUB_ROWS):
        rs = slice(r0, r0 + CONV_SUB_ROWS)
        t = k * tt + r0 + lax.broadcasted_iota(jnp.int32, (CONV_SUB_ROWS, 1), 0)
        pa = dot(ce_ref[rs, :], yre) + dot(sot_ref[rs, :], yso) + _alt_sign(t) * nyq
        pb = dot(se_ref[rs, :], yse) + dot(cot_ref[rs, :], yro)
        rows = pl.ds(pl.multiple_of(k * tt + r0, CONV_SUB_ROWS), CONV_SUB_ROWS)
        u_lo = u_ref[0, rows, :].astype(F32)
        o_ref[0, rows, :] = (g_ref[0, rows, :].astype(F32) * (pa + pb + u_lo * skip)).astype(BF16)
        d_scr[rows, :] = (pa - pb).astype(BF16)

    @pl.when(k == pl.num_programs(2) - 1)
    def _():
        y_hi = dot(flip_ref[...], d_scr[...])
        r = lax.broadcasted_iota(jnp.int32, (hh, 1), 0)
        y_hi = jnp.where(r == 0, row_scr[0:1, :], y_hi)
        u_hi = u_ref[0, hh:, :].astype(F32)
        o_ref[0, hh:, :] = (g_ref[0, hh:, :].astype(F32) * (y_hi + u_hi * skip)).astype(BF16)


def _conv_inv(tabs, ys4, gsrc, g_blk0, usrc, u_blk0, skip, order, tt=1024, tn=512):
    b, hh, _ = ys4[0].shape
    s = 2 * hh
    nblk = HY_D // tn
    est = (8 * _nbytes((tt, hh), BF16) + 2 * _nbytes((hh, hh), BF16) + 8 * _nbytes((hh, tn), BF16)
           + 6 * _nbytes((s, tn), BF16) + _nbytes((hh, tn), BF16) + 12 * _nbytes((CONV_SUB_ROWS, tn), F32)
           + 6 * _nbytes((hh, tn), F32))
    tab = pl.BlockSpec((tt, hh), lambda bi, j, k: (k, 0))
    spec_y = pl.BlockSpec((1, hh, tn), lambda bi, j, k: (bi, 0, j))
    return pl.pallas_call(
        _conv_inv_kernel,
        name="hyena_conv_inv",
        grid=(b, nblk, hh // tt),
        in_specs=[
            tab, tab, tab, tab,
            pl.BlockSpec((hh, hh), lambda bi, j, k: (0, 0)),
            spec_y, spec_y, spec_y, spec_y,
            pl.BlockSpec((1, s, tn), lambda bi, j, k: (bi, 0, g_blk0 * nblk + j)),
            pl.BlockSpec((1, s, tn), lambda bi, j, k: (bi, 0, u_blk0 * nblk + j)),
            pl.BlockSpec((1, 1, tn), lambda bi, j, k: (order, 0, j)),
        ],
        out_specs=pl.BlockSpec((1, s, tn), lambda bi, j, k: (bi, 0, j)),
        out_shape=jax.ShapeDtypeStruct((b, s, HY_D), BF16),
        scratch_shapes=[pltpu.VMEM((hh, tn), BF16), pltpu.VMEM((8, tn), F32)],
        compiler_params=_cparams(("parallel", "parallel", "arbitrary"), est),
    )(tabs["ce"], tabs["se"], tabs["cot"], tabs["sot"], tabs["flip"], *ys4, gsrc, usrc,
      skip.reshape(HY_ORDER, 1, HY_D))


def _rope_tables(seq, rot_dim):
    half = rot_dim // 2
    pos = jnp.arange(seq, dtype=F32)
    inv = ROPE_THETA ** (-jnp.arange(0, rot_dim, 2, dtype=F32) / rot_dim)
    ang = pos[:, None] * inv[None, :]
    cos, sin = jnp.cos(ang), jnp.sin(ang)
    rest = LANE - rot_dim
    return cos, sin, half, rest


def _rope_lane_tables(seq, rot_dim, rest_passthrough):
    cos, sin, half, rest = _rope_tables(seq, rot_dim)
    fill = jnp.ones((seq, rest), F32) if rest_passthrough else jnp.zeros((seq, rest), F32)
    zero_h = jnp.zeros((seq, half), F32)
    zero_r = jnp.zeros((seq, rest), F32)
    cos_t = jnp.concatenate([cos, cos, fill], axis=1)
    sin_up = jnp.concatenate([zero_h, sin, zero_r], axis=1)
    sin_dn = jnp.concatenate([-sin, zero_h, zero_r], axis=1)
    return cos_t, sin_up, sin_dn


def _dft_tables(n_half):
    hh = n_half // 2
    blk = 32
    idx = jnp.arange(n_half, dtype=jnp.int32)
    unit = 2.0 * math.pi / n_half
    ang_a = ((blk * idx[:hh // blk, None] * idx[None, :]) & (n_half - 1)).astype(F32) * unit
    ang_b = ((idx[:blk, None] * idx[None, :]) & (n_half - 1)).astype(F32) * unit
    ca, sa = jnp.cos(ang_a)[:, None, :], jnp.sin(ang_a)[:, None, :]
    cb, sb = jnp.cos(ang_b)[None], jnp.sin(ang_b)[None]
    ce_f = (ca * cb - sa * sb).reshape(hh, n_half)
    se_f = (sa * cb + ca * sb).reshape(hh, n_half)
    turn = idx.astype(F32) * (0.5 * unit)
    c_row, s_row = jnp.cos(turn)[None, :], jnp.sin(turn)[None, :]
    co_f = ce_f * c_row - se_f * s_row
    so_f = se_f * c_row + ce_f * s_row
    alt = (1 - 2 * (idx & 1)).astype(F32)
    ce, se = ce_f[:, :hh], se_f[:, :hh]
    c_col, s_col = jnp.cos(turn[:hh])[:, None], jnp.sin(turn[:hh])[:, None]
    flip = (idx[:hh, None] >= 1) & (idx[None, :hh] == hh - idx[:hh, None])
    tabs = dict(
        ct_eo=jnp.concatenate([ce_f, co_f], axis=0),
        st_eo=jnp.concatenate([jnp.where(idx[:hh, None] == 0, alt[None, :], se_f), so_f], axis=0),
        ce=ce, se=se, co=co_f[:, :hh], so=so_f[:, :hh],
        cot=ce * c_col - se * s_col, sot=se * c_col + ce * s_col, flip=flip)
    return {name: t.astype(BF16) for name, t in tabs.items()}


def _filter_features(n_lag):
    t = jnp.linspace(0.0, 1.0, n_lag, dtype=F32)[:, None]
    bands = (HY_EMB - 1) // 2
    wpos = 2.0 * math.pi * jnp.arange(n_lag, dtype=F32) / n_lag
    fb = jnp.linspace(1e-4, bands - 1, bands, dtype=F32)
    fw = wpos[:, None] * fb[None, :]
    z = jnp.concatenate([t, jnp.cos(fw), -jnp.sin(fw)], axis=-1)
    return jnp.pad(z, ((0, 0), (0, LANE - HY_EMB)))


def kernel(x, c, ada_mix_w, ada_mix_b, norm_mix_g, ada_mlp_w, ada_mlp_b, norm_mlp_g, w_mlp_in, w_mlp_out, e_w_in, e_q_norm_g, e_kv_norm_g, e_w_uq, e_w_ukv, e_conv_w, e_conv_b, e_f_w1, e_f_b1, e_f_w2, e_f_b2, e_f_w3, e_f_b3, e_f_freq, e_f_w4, e_hy_skip, e_w_out, o_w_qkv, o_sinks, o_w_o, final_norm_g):
    b, s, d = x.shape
    c8 = jnp.pad(c, ((0, 8 - b), (0, 0)))
    mod_mix = _ada(c8, ada_mix_w, ada_mix_b)[:, :b].reshape(DEPTH, b, 1, 3 * d)
    mod_mlp = _ada(c8, ada_mlp_w, ada_mlp_b)[:, :b].reshape(DEPTH, b, 1, 3 * d)

    mla_tabs = _rope_lane_tables(s, MLA_ROPE, rest_passthrough=False)
    gqa_cos, gqa_sin_up, gqa_sin_dn = _rope_lane_tables(s, GQA_ROT, rest_passthrough=True)
    gqa_tabs = (gqa_cos, gqa_sin_up + gqa_sin_dn)
    dft = _dft_tables(s)
    z_feat = _filter_features(s)
    q_scale = (MLA_NOPE + MLA_ROPE) ** -0.5 * math.log2(math.e)

    for l in range(DEPTH):
        i = l // 2
        if l % 2 == 0:
            h = _normmod(x, mod_mix[l], norm_mix_g[l])
            lat = _matmul(h, e_w_in, i, F32, tn=LAT_W, n_cols=LAT_W)
            w_hy = e_w_in[i][:, Q_LORA + KV_LORA + MLA_ROPE:]
            u = _proj_short_conv(h, w_hy, e_conv_w[i], e_conv_b[i])

            w_uq = e_w_uq[i].reshape(Q_LORA, MLA_HEADS, MLA_NOPE + MLA_ROPE)
            wq = jnp.concatenate([
                w_uq[:, :, :MLA_NOPE].reshape(Q_LORA, -1),
                jnp.pad(w_uq[:, :, MLA_NOPE:], ((0, 0), (0, 0), (0, LANE - MLA_ROPE))).reshape(Q_LORA, -1),
            ], axis=1).astype(BF16)
            w_ukv = e_w_ukv[i].reshape(KV_LORA, MLA_HEADS, MLA_NOPE + MLA_V)
            wk = w_ukv[:, :, :MLA_NOPE].reshape(KV_LORA, -1).astype(BF16)
            wv = w_ukv[:, :, MLA_NOPE:].reshape(KV_LORA, -1).astype(BF16)
            q, k, v = _mla_proj(lat, e_q_norm_g[i], e_kv_norm_g[i], wq, wk, wv, mla_tabs, q_scale)
            a_mla = _mla_attn(q, k, v)

            w1 = jnp.pad(e_f_w1[i], ((0, LANE - HY_EMB), (0, 0)))
            h_filt = _filter_mlp(z_feat, w1, e_f_b1[i], e_f_w2[i], e_f_b2[i], e_f_w3[i], e_f_b3[i], e_f_freq[i])
            ksum, kdiff, knyq = _filter_gen(h_filt, e_f_w4[i])
            spec = (*_filter_dft(dft["ct_eo"], dft["st_eo"], ksum, kdiff), knyq)
            zc = _conv_inv(dft, _conv_fwd(dft, u, 0, spec, 0), u, 1, u, 0, e_hy_skip[i], 0)
            b_hy = _conv_inv(dft, _conv_fwd(dft, zc, 0, spec, 1), u, 2, zc, 0, e_hy_skip[i], 1)

            x = _proj_res([a_mla, b_hy], e_w_out, i, x, mod_mix[l])
        else:
            h = _normmod(x, mod_mix[l], norm_mix_g[l])
            qkv = _matmul(h, o_w_qkv, i, BF16)
            o = _gqa_attn(qkv, o_sinks[i], gqa_tabs)
            x = _proj_res([o], o_w_o, i, x, mod_mix[l])
        x = _mlp(x, mod_mlp[l], norm_mlp_g[l], w_mlp_in, w_mlp_out, l, final_norm_g, l == DEPTH - 1)
    return x
```

```python
import functools
import math

import jax
import jax.numpy as jnp
from jax import lax
from jax.experimental import pallas as pl
from jax.experimental.pallas import tpu as pltpu

F32 = jnp.float32
BF16 = jnp.bfloat16
HIGHEST = lax.Precision.HIGHEST

D_MODEL = 2048
BATCH = 4
SEQ = 2048
DEPTH = 4
RMS_EPS = 1e-6
ROPE_THETA = 500000.0
NEG_INF = -1e30
MLA_HEADS = 8
MLA_NOPE = 128
MLA_ROPE = 64
MLA_V = 128
Q_LORA = 512
KV_LORA = 256
HY_D = 1024
HY_ORDER = 2
HY_SHORT = 3
HY_EMB = 33
HY_FFN = 64
HY_DECAY_PCT_SHORT = 0.3
HY_DECAY_PCT_LONG = 1.5
HY_TARGET = 1e-2
GQA_HEADS = 16
GQA_KV_HEADS = 4
GQA_HEAD_DIM = 128
GQA_ROT = GQA_HEAD_DIM // 4
GQA_GROUP = GQA_HEADS // GQA_KV_HEADS
WINDOW = 128
BAND = 128
D_FF = 4 * D_MODEL
LAT_W = Q_LORA + KV_LORA + 128
MLA_QK = 256
DFT_N = 2 * SEQ
CONV_SUB_ROWS = 256

LANE = 128
V7X_VMEM_BYTES = 64 * 1024 * 1024
V7X_VMEM_BUDGET = 56 * 1024 * 1024


def _cparams(semantics, est_bytes):
    limit = int(min(V7X_VMEM_BUDGET, max(32 * 1024 * 1024, est_bytes * 3 // 2)))
    return pltpu.CompilerParams(dimension_semantics=semantics, vmem_limit_bytes=limit)


def _nbytes(shape, dtype):
    return math.prod(shape) * jnp.dtype(dtype).itemsize


def _rms(x, g):
    ms = jnp.mean(x * x, axis=-1, keepdims=True)
    return x * lax.rsqrt(ms + RMS_EPS) * g


def _ada_kernel(c_ref, w_ref, b_ref, o_ref):
    cv = c_ref[...]
    s = cv * (1.0 / (1.0 + jnp.exp(-cv)))
    o_ref[0] = jnp.dot(s.astype(BF16), w_ref[0].astype(BF16), preferred_element_type=F32) + b_ref[0]


def _ada(c8, w, b, tn=1536):
    n_l, d, n = w.shape
    est = 2 * _nbytes((d, tn), F32) + _nbytes((d, tn), BF16) + 4 * _nbytes((8, tn), F32) + 2 * _nbytes((8, d), F32)
    return pl.pallas_call(
        _ada_kernel,
        name="ada_modulation",
        grid=(n_l, n // tn),
        in_specs=[
            pl.BlockSpec((8, d), lambda l, j: (0, 0)),
            pl.BlockSpec((1, d, tn), lambda l, j: (l, 0, j)),
            pl.BlockSpec((1, 1, tn), lambda l, j: (l, 0, j)),
        ],
        out_specs=pl.BlockSpec((1, 8, tn), lambda l, j: (l, 0, j)),
        out_shape=jax.ShapeDtypeStruct((n_l, 8, n), F32),
        compiler_params=_cparams(("parallel", "parallel"), est),
    )(c8, w, b.reshape(n_l, 1, n))


def _normmod_rows(x_ref, shift_ref, scale_ref, g_ref, h_ref, rows):
    gain = g_ref[...] * (1.0 + scale_ref[0])
    sh = shift_ref[0]
    tm = h_ref.shape[0]

    def body(r, carry):
        sl = pl.ds(pl.multiple_of(r * rows, rows), rows)
        xv = x_ref[0, sl, :]
        inv = lax.rsqrt(jnp.mean(xv * xv, axis=-1, keepdims=True) + RMS_EPS)
        h_ref[sl, :] = (xv * inv * gain + sh).astype(BF16)
        return carry

    lax.fori_loop(0, tm // rows, body, 0)


def _normmod_kernel(x_ref, shift_ref, scale_ref, g_ref, h_ref):
    _normmod_rows(x_ref, shift_ref, scale_ref, g_ref, h_ref.at[0], 128)


def _normmod(x, mod, g, tm=512):
    b, s, d = x.shape
    est = 2 * _nbytes((tm, d), F32) + 2 * _nbytes((tm, d), BF16) + 8 * _nbytes((128, d), F32)
    return pl.pallas_call(
        _normmod_kernel,
        name="normmod",
        grid=(b, s // tm),
        in_specs=[
            pl.BlockSpec((1, tm, d), lambda bi, i: (bi, i, 0)),
            pl.BlockSpec((1, 1, d), lambda bi, i: (bi, 0, 0)),
            pl.BlockSpec((1, 1, d), lambda bi, i: (bi, 0, 1)),
            pl.BlockSpec((1, d), lambda bi, i: (0, 0)),
        ],
        out_specs=pl.BlockSpec((1, tm, d), lambda bi, i: (bi, i, 0)),
        out_shape=jax.ShapeDtypeStruct((b, s, d), BF16),
        compiler_params=_cparams(("parallel", "parallel"), est),
    )(x, mod, mod, g.reshape(1, d))


def _matmul_kernel(h_ref, w_ref, o_ref):
    o_ref[0] = jnp.dot(h_ref[0], w_ref[...].astype(BF16), preferred_element_type=F32).astype(o_ref.dtype)


def _matmul(h, w_all, layer, out_dtype, tm=2048, tn=512, n_cols=None):
    b, s, d = h.shape
    n = w_all.shape[2] if n_cols is None else n_cols
    tn = min(tn, n)
    est = (2 * _nbytes((tm, d), BF16) + 2 * _nbytes((d, tn), F32) + _nbytes((d, tn), BF16)
           + 3 * _nbytes((tm, tn), F32))
    return pl.pallas_call(
        _matmul_kernel,
        name="matmul",
        grid=(b, s // tm, n // tn),
        in_specs=[
            pl.BlockSpec((1, tm, d), lambda bi, i, j: (bi, i, 0)),
            pl.BlockSpec((None, d, tn), lambda bi, i, j: (layer, 0, j)),
        ],
        out_specs=pl.BlockSpec((1, tm, tn), lambda bi, i, j: (bi, i, j)),
        out_shape=jax.ShapeDtypeStruct((b, s, n), out_dtype),
        compiler_params=_cparams(("parallel", "parallel", "parallel"), est),
    )(h, w_all)


def _mlp_kernel(x_ref, shift_ref, scale_ref, gate_ref, g_ref, w1_ref, w2_ref, fg_ref, o_ref, h_ref, *, final_norm):
    f = pl.program_id(2)

    @pl.when(f == 0)
    def _():
        _normmod_rows(x_ref, shift_ref, scale_ref, g_ref, h_ref, 128)

        o_ref[...] = jnp.zeros_like(o_ref)

    a = jnp.dot(h_ref[...], w1_ref[...].astype(BF16), preferred_element_type=F32)
    a = jnp.square(jnp.maximum(a, 0.0)).astype(BF16)
    o_ref[0] += jnp.dot(a, w2_ref[...].astype(BF16), preferred_element_type=F32)

    @pl.when(f == pl.num_programs(2) - 1)
    def _():
        gate = gate_ref[0]
        fg = fg_ref[...]
        rows = 128

        def body(r, carry):
            sl = pl.ds(pl.multiple_of(r * rows, rows), rows)
            y = x_ref[0, sl, :] + gate * o_ref[0, sl, :]
            if final_norm:
                y = _rms(y, fg)
            o_ref[0, sl, :] = y
            return carry

        lax.fori_loop(0, o_ref.shape[1] // rows, body, 0)


def _mlp(x, mod, g, w1_all, w2_all, layer, final_g, final_norm, tm=1024, tf=512):
    b, s, d = x.shape
    ff = w1_all.shape[2]
    est = (4 * _nbytes((tm, d), F32) + _nbytes((tm, d), BF16) + 4 * _nbytes((d, tf), F32)
           + 2 * _nbytes((d, tf), BF16) + 2 * _nbytes((tm, tf), F32))
    return pl.pallas_call(
        functools.partial(_mlp_kernel, final_norm=final_norm),
        name="mlp_relu2",
        grid=(b, s // tm, ff // tf),
        in_specs=[
            pl.BlockSpec((1, tm, d), lambda bi, i, f: (bi, i, 0)),
            pl.BlockSpec((1, 1, d), lambda bi, i, f: (bi, 0, 0)),
            pl.BlockSpec((1, 1, d), lambda bi, i, f: (bi, 0, 1)),
            pl.BlockSpec((1, 1, d), lambda bi, i, f: (bi, 0, 2)),
            pl.BlockSpec((1, d), lambda bi, i, f: (0, 0)),
            pl.BlockSpec((None, d, tf), lambda bi, i, f: (layer, 0, f)),
            pl.BlockSpec((None, tf, d), lambda bi, i, f: (layer, f, 0)),
            pl.BlockSpec((1, d), lambda bi, i, f: (0, 0)),
        ],
        out_specs=pl.BlockSpec((1, tm, d), lambda bi, i, f: (bi, i, 0)),
        out_shape=jax.ShapeDtypeStruct((b, s, d), F32),
        scratch_shapes=[pltpu.VMEM((tm, d), BF16)],
        compiler_params=_cparams(("parallel", "parallel", "arbitrary"), est),
    )(x, mod, mod, mod, g.reshape(1, d), w1_all, w2_all, final_g.reshape(1, d))


def _proj_res_kernel(*refs, n_in):
    a_refs, w_refs = refs[:n_in], refs[n_in:2 * n_in]
    x_ref, gate_ref, o_ref = refs[2 * n_in:]
    acc = None
    for a_ref, w_ref in zip(a_refs, w_refs):
        p = jnp.dot(a_ref[0], w_ref[...].astype(BF16), preferred_element_type=F32)
        acc = p if acc is None else acc + p
    o_ref[0] = x_ref[0] + gate_ref[0] * acc


def _proj_res(a_list, w_all, layer, x, mod, tm=2048, tn=512):
    b, s, d = x.shape
    n_in = len(a_list)
    kk = a_list[0].shape[-1]
    assert all(a.shape[-1] == kk for a in a_list) and w_all.shape[1] == n_in * kk
    est = (n_in * (2 * _nbytes((tm, kk), BF16) + 2 * _nbytes((kk, tn), F32) + _nbytes((kk, tn), BF16))
           + 6 * _nbytes((tm, tn), F32))
    in_specs = [pl.BlockSpec((1, tm, kk), lambda bi, i, j: (bi, i, 0)) for _ in a_list]
    in_specs += [pl.BlockSpec((None, kk, tn), functools.partial(lambda bi, i, j, r: (layer, r, j), r=r))
                 for r in range(n_in)]
    in_specs += [
        pl.BlockSpec((1, tm, tn), lambda bi, i, j: (bi, i, j)),
        pl.BlockSpec((1, 1, tn), lambda bi, i, j: (bi, 0, 2 * (d // tn) + j)),
    ]
    return pl.pallas_call(
        functools.partial(_proj_res_kernel, n_in=n_in),
        name="proj_residual",
        grid=(b, s // tm, d // tn),
        in_specs=in_specs,
        out_specs=pl.BlockSpec((1, tm, tn), lambda bi, i, j: (bi, i, j)),
        out_shape=jax.ShapeDtypeStruct((b, s, d), F32),
        compiler_params=_cparams(("parallel", "parallel", "parallel"), est),
    )(*a_list, *([w_all] * n_in), x, mod)


def _rot_lanes(blk, cos_t, sin_up, sin_dn, half):
    return (blk * cos_t + pltpu.roll(blk, half, 1) * sin_up + pltpu.roll(blk, LANE - half, 1) * sin_dn)


def _mla_proj_kernel(lat_ref, gq_ref, gkv_ref, wq_ref, wk_ref, wv_ref, cos_ref, sup_ref, sdn_ref,
                     q_ref, k_ref, v_ref, *, q_scale):
    lat = lat_ref[0]
    qn = _rms(lat[:, :Q_LORA], gq_ref[...]).astype(BF16)
    kvn = _rms(lat[:, Q_LORA:Q_LORA + KV_LORA], gkv_ref[...]).astype(BF16)
    cos_t, sin_up, sin_dn = cos_ref[...], sup_ref[...], sdn_ref[...]
    half = MLA_ROPE // 2
    q = jnp.dot(qn, wq_ref[...], preferred_element_type=F32) * q_scale
    kn = jnp.dot(kvn, wk_ref[...], preferred_element_type=F32)
    v = jnp.dot(kvn, wv_ref[...], preferred_element_type=F32).astype(BF16)
    ones = jnp.ones((v.shape[0], MLA_V), BF16)
    kr = _rot_lanes(lat[:, Q_LORA + KV_LORA:], cos_t, sin_up, sin_dn, half).astype(BF16)
    nope_w = MLA_HEADS * MLA_NOPE
    for h in range(MLA_HEADS):
        c0 = h * MLA_QK
        v_ref[0, :, 2 * h * MLA_V:(2 * h + 1) * MLA_V] = v[:, h * MLA_V:(h + 1) * MLA_V]
        v_ref[0, :, (2 * h + 1) * MLA_V:(2 * h + 2) * MLA_V] = ones
        q_ref[0, :, c0:c0 + LANE] = q[:, h * LANE:(h + 1) * LANE].astype(BF16)
        qr = q[:, nope_w + h * LANE:nope_w + (h + 1) * LANE]
        q_ref[0, :, c0 + LANE:c0 + 2 * LANE] = _rot_lanes(qr, cos_t, sin_up, sin_dn, half).astype(BF16)
        k_ref[0, :, c0:c0 + LANE] = kn[:, h * LANE:(h + 1) * LANE].astype(BF16)
        k_ref[0, :, c0 + LANE:c0 + 2 * LANE] = kr


def _mla_proj(lat, gq, gkv, wq, wk, wv, tabs, q_scale, tm=512):
    b, s, _ = lat.shape
    hq = MLA_HEADS * MLA_QK
    hv = MLA_HEADS * 2 * MLA_V
    est = (2 * _nbytes((tm, LAT_W), F32) + 2 * _nbytes(wq.shape, BF16) + 2 * _nbytes(wk.shape, BF16)
           + 2 * _nbytes(wv.shape, BF16) + 4 * _nbytes((tm, hq), BF16) + 2 * _nbytes((tm, hv), BF16)
           + 3 * _nbytes((tm, hq), F32))
    full = lambda shape: pl.BlockSpec(shape, lambda bi, i: (0,) * len(shape))
    tab = pl.BlockSpec((tm, LANE), lambda bi, i: (i, 0))
    return pl.pallas_call(
        functools.partial(_mla_proj_kernel, q_scale=q_scale),
        name="mla_proj",
        grid=(b, s // tm),
        in_specs=[
            pl.BlockSpec((1, tm, LAT_W), lambda bi, i: (bi, i, 0)),
            full((1, Q_LORA)), full((1, KV_LORA)), full(wq.shape), full(wk.shape), full(wv.shape),
            tab, tab, tab,
        ],
        out_specs=[
            pl.BlockSpec((1, tm, hq), lambda bi, i: (bi, i, 0)),
            pl.BlockSpec((1, tm, hq), lambda bi, i: (bi, i, 0)),
            pl.BlockSpec((1, tm, hv), lambda bi, i: (bi, i, 0)),
        ],
        out_shape=[
            jax.ShapeDtypeStruct((b, s, hq), BF16),
            jax.ShapeDtypeStruct((b, s, hq), BF16),
            jax.ShapeDtypeStruct((b, s, hv), BF16),
        ],
        compiler_params=_cparams(("parallel", "parallel"), est),
    )(lat, gq.reshape(1, -1), gkv.reshape(1, -1), wq, wk, wv, *tabs)


def _mla_attn_kernel(q_ref, k_ref, v_ref, o_ref, *, chain_rows):
    n_chains = q_ref.shape[1] // chain_rows

    def scores(c):
        rs = slice(c * chain_rows, (c + 1) * chain_rows)
        return lax.dot_general(q_ref[0, rs, :], k_ref[0], (((1,), (1,)), ((), ())), preferred_element_type=F32)

    s_next = scores(0)
    for c in range(n_chains):
        s = s_next
        if c + 1 < n_chains:
            s_next = scores(c + 1)
        m = jnp.max(s, axis=-1, keepdims=True)
        p = jnp.exp2(s - m).astype(BF16)
        oe = jnp.dot(p, v_ref[0], preferred_element_type=F32)
        o_ref[0, c * chain_rows:(c + 1) * chain_rows, :] = (oe[:, :MLA_V] / oe[:, MLA_V:]).astype(BF16)


def _mla_attn(q, k, v, tq=2048, chain_rows=1024):
    b, s, _ = q.shape
    est = (2 * _nbytes((tq, MLA_QK), BF16) + 2 * _nbytes((s, MLA_QK), BF16) + 2 * _nbytes((s, 2 * MLA_V), BF16)
           + 2 * _nbytes((tq, MLA_V), BF16) + 5 * _nbytes((chain_rows, s), F32))
    return pl.pallas_call(
        functools.partial(_mla_attn_kernel, chain_rows=chain_rows),
        name="mla_attn",
        grid=(b, MLA_HEADS, s // tq),
        in_specs=[
            pl.BlockSpec((1, tq, MLA_QK), lambda bi, h, i: (bi, i, h)),
            pl.BlockSpec((1, s, MLA_QK), lambda bi, h, i: (bi, 0, h)),
            pl.BlockSpec((1, s, 2 * MLA_V), lambda bi, h, i: (bi, 0, h)),
        ],
        out_specs=pl.BlockSpec((1, tq, MLA_V), lambda bi, h, i: (bi, i, h)),
        out_shape=jax.ShapeDtypeStruct((b, s, MLA_HEADS * MLA_V), BF16),
        compiler_params=_cparams(("parallel", "parallel", "parallel"), est),
    )(q, k, v)


def _gqa_kernel(q_ref, k_ref, v_ref, sink_ref, cos_ref, sin_ref, swap_ref, o_ref, k_scr, v_scr, *, scale):
    s_len = k_scr.shape[0]
    n_blocks = s_len // BAND
    rows = GQA_GROUP * BAND
    swap = swap_ref[...]

    def rot(x, cos_t, sin_t):
        return x.astype(F32) * cos_t + jnp.dot(x, swap, preferred_element_type=F32) * sin_t

    k_scr[...] = rot(k_ref[0], cos_ref[...], sin_ref[...]).astype(BF16)
    v_scr[:, :LANE] = v_ref[0]
    v_scr[:, LANE:] = jnp.ones((s_len, LANE), BF16)
    sink = sink_ref[0]
    rel = (lax.broadcasted_iota(jnp.int32, (rows, 3 * BAND), 1) - BAND
           - (lax.broadcasted_iota(jnp.int32, (rows, 3 * BAND), 0) & (BAND - 1)))
    bias = jnp.where(jnp.abs(rel) <= WINDOW, 0.0, NEG_INF)

    def window(n):
        r0 = n * BAND
        return max(0, r0 - BAND), min(s_len, r0 + 2 * BAND)

    def scores(n):
        r0 = n * BAND
        cos_t = jnp.concatenate([cos_ref[r0:r0 + BAND, :]] * GQA_GROUP, axis=0)
        sin_t = jnp.concatenate([sin_ref[r0:r0 + BAND, :]] * GQA_GROUP, axis=0)
        q_st = jnp.concatenate([q_ref[0, r0:r0 + BAND, g * LANE:(g + 1) * LANE] for g in range(GQA_GROUP)], axis=0)
        q_st = (rot(q_st, cos_t, sin_t) * scale).astype(BF16)
        lo, hi = window(n)
        c0 = lo - (r0 - BAND)
        s = lax.dot_general(q_st, k_scr[lo:hi, :], (((1,), (1,)), ((), ())), preferred_element_type=F32)
        return s + bias[:, c0:c0 + hi - lo]

    s_next = scores(0)
    for n in range(n_blocks):
        s = s_next
        if n + 1 < n_blocks:
            s_next = scores(n + 1)
        r0 = n * BAND
        lo, hi = window(n)
        m = jnp.maximum(jnp.max(s, axis=-1, keepdims=True), sink)
        p = jnp.exp(s - m).astype(BF16)
        oe = jnp.dot(p, v_scr[lo:hi, :], preferred_element_type=F32)
        o = oe[:, :LANE] / (oe[:, LANE:] + jnp.exp(sink - m))
        for g in range(GQA_GROUP):
            o_ref[0, r0:r0 + BAND, g * LANE:(g + 1) * LANE] = o[g * BAND:(g + 1) * BAND, :].astype(BF16)


def _gqa_attn(qkv, sinks, tabs):
    b, s, _ = qkv.shape
    gw = GQA_GROUP * GQA_HEAD_DIM
    sink_col = jnp.broadcast_to(sinks.astype(F32).reshape(GQA_KV_HEADS, GQA_GROUP, 1, 1),
                                (GQA_KV_HEADS, GQA_GROUP, BAND, 1)).reshape(GQA_KV_HEADS, GQA_GROUP * BAND, 1)
    est = (4 * _nbytes((s, gw), BF16) + 7 * _nbytes((s, LANE), BF16) + 6 * _nbytes((s, LANE), F32)
           + 8 * _nbytes((GQA_GROUP * BAND, 3 * BAND), F32))
    tab = pl.BlockSpec((s, LANE), lambda bi, h: (0, 0))
    cos_t, sin_t = tabs
    half = GQA_ROT // 2
    lane = jnp.arange(LANE)
    swap = (((lane[None, :] < half) & (lane[:, None] == lane[None, :] + half))
            | ((lane[None, :] >= half) & (lane[None, :] < 2 * half) & (lane[:, None] == lane[None, :] - half)))
    return pl.pallas_call(
        functools.partial(_gqa_kernel, scale=GQA_HEAD_DIM ** -0.5),
        name="gqa_window_attn",
        grid=(b, GQA_KV_HEADS),
        in_specs=[
            pl.BlockSpec((1, s, gw), lambda bi, h: (bi, 0, h)),
            pl.BlockSpec((1, s, LANE), lambda bi, h: (bi, 0, GQA_HEADS + h)),
            pl.BlockSpec((1, s, LANE), lambda bi, h: (bi, 0, GQA_HEADS + GQA_KV_HEADS + h)),
            pl.BlockSpec((1, GQA_GROUP * BAND, 1), lambda bi, h: (h, 0, 0)),
            tab, tab,
            pl.BlockSpec((LANE, LANE), lambda bi, h: (0, 0)),
        ],
        out_specs=pl.BlockSpec((1, s, gw), lambda bi, h: (bi, 0, h)),
        out_shape=jax.ShapeDtypeStruct((b, s, GQA_HEADS * GQA_HEAD_DIM), BF16),
        scratch_shapes=[pltpu.VMEM((s, LANE), BF16), pltpu.VMEM((s, 2 * LANE), BF16)],
        compiler_params=_cparams(("parallel", "parallel"), est),
    )(qkv, qkv, qkv, sink_col, cos_t, sin_t, swap.astype(BF16))


def _proj_short_conv_kernel(h_ref, w_ref, cw_ref, cb_ref, o_ref, *, sub_cols):
    tn = o_ref.shape[-1]
    h = h_ref[0]

    def project(c0):
        return jnp.dot(h, w_ref[:, c0:c0 + sub_cols].astype(BF16), preferred_element_type=F32)

    x_next = project(0)
    for c0 in range(0, tn, sub_cols):
        x = x_next
        if c0 + sub_cols < tn:
            x_next = project(c0 + sub_cols)
        cols = slice(c0, c0 + sub_cols)
        s_len = x.shape[0]
        row = lax.broadcasted_iota(jnp.int32, x.shape, 0)
        prev = jnp.where(row == 0, 0.0, pltpu.roll(x, 1, 0))
        nxt = jnp.where(row == s_len - 1, 0.0, pltpu.roll(x, s_len - 1, 0))
        y = cb_ref[:, cols] + prev * cw_ref[0:1, cols]
        y = y + x * cw_ref[1:2, cols]
        y = y + nxt * cw_ref[2:3, cols]
        o_ref[0, :, cols] = y.astype(BF16)


def _proj_short_conv(h, w_all, layer, cw, cb, tn=1024, sub_cols=512):
    b, s, d = h.shape
    n = w_all.shape[2]
    est = (2 * _nbytes((s, d), BF16) + 2 * _nbytes((d, tn), F32) + 2 * _nbytes((d, sub_cols), BF16)
           + 2 * _nbytes((s, tn), BF16) + 8 * _nbytes((s, sub_cols), F32))
    return pl.pallas_call(
        functools.partial(_proj_short_conv_kernel, sub_cols=sub_cols),
        name="hyena_proj_short_conv",
        grid=(b, n // tn),
        in_specs=[
            pl.BlockSpec((1, s, d), lambda bi, j: (bi, 0, 0)),
            pl.BlockSpec((None, d, tn), lambda bi, j: (layer, 0, j)),
            pl.BlockSpec((HY_SHORT, tn), lambda bi, j: (0, j)),
            pl.BlockSpec((1, tn), lambda bi, j: (0, j)),
        ],
        out_specs=pl.BlockSpec((1, s, tn), lambda bi, j: (bi, 0, j)),
        out_shape=jax.ShapeDtypeStruct((b, s, n), BF16),
        compiler_params=_cparams(("parallel", "parallel"), est),
    )(h, w_all, cw, cb.reshape(1, n))


def _filter_mlp_kernel(z_ref, w1_ref, b1_ref, w2_ref, b2_ref, w3_ref, b3_ref, fr_ref, h_ref):
    dot = functools.partial(jnp.dot, preferred_element_type=F32, precision=HIGHEST)
    fr = fr_ref[...]
    h = jnp.sin(fr * (dot(z_ref[...], w1_ref[...]) + b1_ref[...]))
    h = jnp.sin(fr * (dot(h, w2_ref[...]) + b2_ref[...]))
    h_ref[...] = jnp.sin(fr * (dot(h, w3_ref[...]) + b3_ref[...]))


def _filter_mlp(z, w1, b1, w2, b2, w3, b3, freq):
    n_lag = z.shape[0]
    full = lambda shape: pl.BlockSpec(shape, lambda i: (0,) * len(shape))
    est = 12 * _nbytes((n_lag, LANE), F32)
    return pl.pallas_call(
        _filter_mlp_kernel,
        name="hyena_filter_mlp",
        grid=(1,),
        in_specs=[full(z.shape), full(w1.shape), full((1, HY_FFN)), full(w2.shape), full((1, HY_FFN)),
                  full(w3.shape), full((1, HY_FFN)), full((1, HY_FFN))],
        out_specs=full((n_lag, HY_FFN)),
        out_shape=jax.ShapeDtypeStruct((n_lag, HY_FFN), F32),
        compiler_params=_cparams(("arbitrary",), est),
    )(z, w1, b1.reshape(1, -1), w2, b2.reshape(1, -1), w3, b3.reshape(1, -1), freq.reshape(1, -1))


def _filter_gen_kernel(h_ref, w4f_ref, w4b_ref, ksum_ref, kdiff_ref, nyq_ref):
    h = h_ref[...].astype(BF16)
    hf = jnp.dot(h, w4f_ref[...].astype(BF16), preferred_element_type=F32)
    hb = jnp.dot(h, w4b_ref[...].astype(BF16), preferred_element_type=F32)
    n_lag, tn = hf.shape
    row = lax.broadcasted_iota(jnp.int32, (n_lag, tn), 0)
    col = pl.program_id(1) * tn + lax.broadcasted_iota(jnp.int32, (1, tn), 1)
    t = row.astype(F32) / (n_lag - 1)
    max_decay = math.log(HY_TARGET) / HY_DECAY_PCT_SHORT
    min_decay = math.log(HY_TARGET) / HY_DECAY_PCT_LONG
    delta = min_decay + (max_decay - min_decay) * (col.astype(F32) / (HY_D - 1))
    decay = jnp.exp(-t * jnp.abs(delta))
    kf = hf * decay
    kb = jnp.where(row == 0, 0.0, hb * decay)
    inv = 1.0 / (jnp.sum(jnp.abs(kf), axis=0, keepdims=True) + jnp.sum(jnp.abs(kb), axis=0, keepdims=True))
    ksum = (kf + kb) * inv
    ksum_ref[0] = ksum.astype(BF16)
    kdiff_ref[0] = ((kb - kf) * inv).astype(BF16)
    alt = (1 - 2 * (row & 1)).astype(F32)
    nyq_ref[0] = jnp.sum(ksum * alt, axis=0, keepdims=True) * (1.0 / DFT_N)


def _filter_gen(h, w4, tn=256):
    n_lag = h.shape[0]
    nblk = HY_D // tn
    est = 10 * _nbytes((n_lag, tn), F32) + 4 * _nbytes((n_lag, LANE), F32) + 4 * _nbytes((n_lag, tn), BF16)
    out = jax.ShapeDtypeStruct((HY_ORDER, n_lag, HY_D), BF16)
    return pl.pallas_call(
        _filter_gen_kernel,
        name="hyena_filter_gen",
        grid=(HY_ORDER, nblk),
        in_specs=[
            pl.BlockSpec((n_lag, HY_FFN), lambda o, j: (0, 0)),
            pl.BlockSpec((HY_FFN, tn), lambda o, j: (0, 2 * o * nblk + j)),
            pl.BlockSpec((HY_FFN, tn), lambda o, j: (0, (2 * o + 1) * nblk + j)),
        ],
        out_specs=[pl.BlockSpec((1, n_lag, tn), lambda o, j: (o, 0, j))] * 2
        + [pl.BlockSpec((1, 1, tn), lambda o, j: (o, 0, j))],
        out_shape=[out, out, jax.ShapeDtypeStruct((HY_ORDER, 1, HY_D), F32)],
        compiler_params=_cparams(("parallel", "parallel"), est),
    )(h, w4, w4)


def _filter_dft_kernel(ct_ref, st_ref, ksum_ref, kdiff_ref, a_ref, bq_ref):
    re = jnp.dot(ct_ref[...], ksum_ref[0], preferred_element_type=F32)
    im = jnp.dot(st_ref[...], kdiff_ref[0], preferred_element_type=F32)
    tf = re.shape[0]
    row = pl.program_id(2) * tf + lax.broadcasted_iota(jnp.int32, re.shape, 0)
    is0 = row == 0
    wgt = jnp.where(is0, 1.0 / DFT_N, 2.0 / DFT_N)
    a_ref[0] = (re * wgt).astype(BF16)
    bq_ref[0] = jnp.where(is0, 0.0, im * wgt).astype(BF16)


def _filter_dft(ct, st, ksum, kdiff, tf=512, tn=512):
    n_f, n_s = ct.shape
    est = (4 * _nbytes((tf, n_s), BF16) + 4 * _nbytes((n_s, tn), BF16) + 4 * _nbytes((tf, tn), BF16)
           + 8 * _nbytes((tf, tn), F32))
    out = jax.ShapeDtypeStruct((HY_ORDER, n_f, HY_D), BF16)
    return pl.pallas_call(
        _filter_dft_kernel,
        name="hyena_filter_dft",
        grid=(HY_ORDER, HY_D // tn, n_f // tf),
        in_specs=[
            pl.BlockSpec((tf, n_s), lambda o, j, k: (k, 0)),
            pl.BlockSpec((tf, n_s), lambda o, j, k: (k, 0)),
            pl.BlockSpec((1, n_s, tn), lambda o, j, k: (o, 0, j)),
            pl.BlockSpec((1, n_s, tn), lambda o, j, k: (o, 0, j)),
        ],
        out_specs=[pl.BlockSpec((1, tf, tn), lambda o, j, k: (o, k, j))] * 2,
        out_shape=[out, out],
        compiler_params=_cparams(("parallel", "parallel", "parallel"), est),
    )(ct, st, ksum, kdiff)


def _alt_sign(idx):
    return (1 - 2 * (idx & 1)).astype(F32)


def _long_conv_kernel(ce_ref, se_ref, co_ref, so_ref, cot_ref, sot_ref, flip_ref, u_ref, g_ref,
                      ae_ref, bqe_ref, ao_ref, bqo_ref, knyq_ref, skip_ref, o_ref,
                      e_scr, o_scr, yre_scr, yse_scr, yro_scr, yso_scr, d_scr):
    hh = e_scr.shape[0]
    sub = CONV_SUB_ROWS
    dot = functools.partial(jnp.dot, preferred_element_type=F32)
    alt_h = _alt_sign(lax.broadcasted_iota(jnp.int32, (hh, 1), 0))

    u_lo = u_ref[0, :hh, :].astype(F32)
    u_rev = dot(flip_ref[...], u_ref[0, hh:, :])
    e32 = u_lo + u_rev
    e_scr[...] = e32.astype(BF16)
    o_scr[...] = (u_lo - u_rev).astype(BF16)
    mid = u_ref[0, hh:hh + 16, :].astype(F32)[0:1]
    nyq_u = jnp.sum(e32 * alt_h, axis=0, keepdims=True) + mid
    e, o = e_scr[...], o_scr[...]

    for r0 in range(0, hh, sub):
        rs = slice(r0, r0 + sub)
        m = r0 + lax.broadcasted_iota(jnp.int32, (sub, 1), 0)
        corr = _alt_sign(m) * mid
        uc = dot(ce_ref[rs, :], e) + corr
        us = dot(se_ref[rs, :], o)
        a, bq = ae_ref[0, rs, :].astype(F32), bqe_ref[0, rs, :].astype(F32)
        yre_scr[rs, :] = (uc * a + us * bq).astype(BF16)
        ys = us * a - uc * bq
        if r0 == 0:
            ys = jnp.where(m == 0, nyq_u * knyq_ref[0], ys)
        yse_scr[rs, :] = ys.astype(BF16)
        uc = dot(co_ref[rs, :], o)
        us = dot(so_ref[rs, :], e) + corr
        a, bq = ao_ref[0, rs, :].astype(F32), bqo_ref[0, rs, :].astype(F32)
        yro_scr[rs, :] = (uc * a + us * bq).astype(BF16)
        yso_scr[rs, :] = (us * a - uc * bq).astype(BF16)

    yre, yse, yro, yso = yre_scr[...], yse_scr[...], yro_scr[...], yso_scr[...]
    nyq = yse_scr[0:16, :].astype(F32)[0:1]
    y_mid = jnp.sum((yre.astype(F32) + yso.astype(F32)) * alt_h, axis=0, keepdims=True) + nyq
    skip = skip_ref[0]
    for r0 in range(0, hh, sub):
        rs = slice(r0, r0 + sub)
        t = r0 + lax.broadcasted_iota(jnp.int32, (sub, 1), 0)
        pa = dot(ce_ref[rs, :], yre) + dot(sot_ref[rs, :], yso) + _alt_sign(t) * nyq
        pb = dot(se_ref[rs, :], yse) + dot(cot_ref[rs, :], yro)
        u_t = u_ref[0, rs, :].astype(F32)
        o_ref[0, rs, :] = (g_ref[0, rs, :].astype(F32) * (pa + pb + u_t * skip)).astype(BF16)
        d_scr[rs, :] = (pa - pb).astype(BF16)
    y_hi = dot(flip_ref[...], d_scr[...])
    y_hi = jnp.where(lax.broadcasted_iota(jnp.int32, (hh, 1), 0) == 0, y_mid, y_hi)
    u_hi = u_ref[0, hh:, :].astype(F32)
    o_ref[0, hh:, :] = (g_ref[0, hh:, :].astype(F32) * (y_hi + u_hi * skip)).astype(BF16)


def _long_conv(tabs, spec, order, usrc, u_blk0, gsrc, g_blk0, skip, tn=512):
    a, bq, knyq = spec
    b, s, _ = usrc.shape
    hh = s // 2
    nblk = HY_D // tn
    est = (7 * _nbytes((hh, hh), BF16) + 10 * _nbytes((s, tn), BF16) + 8 * _nbytes((hh, tn), BF16)
           + 7 * _nbytes((hh, tn), BF16) + 8 * _nbytes((hh, tn), F32) + 12 * _nbytes((CONV_SUB_ROWS, tn), F32))
    tab = pl.BlockSpec((hh, hh), lambda bi, j: (0, 0), pipeline_mode=pl.Buffered(1))
    even = pl.BlockSpec((1, hh, tn), lambda bi, j: (order, 0, j))
    odd = pl.BlockSpec((1, hh, tn), lambda bi, j: (order, 1, j))
    row = pl.BlockSpec((1, 1, tn), lambda bi, j: (order, 0, j))
    return pl.pallas_call(
        _long_conv_kernel,
        name="hyena_long_conv",
        grid=(b, nblk),
        in_specs=[
            tab, tab, tab, tab, tab, tab, tab,
            pl.BlockSpec((1, s, tn), lambda bi, j: (bi, 0, u_blk0 * nblk + j)),
            pl.BlockSpec((1, s, tn), lambda bi, j: (bi, 0, g_blk0 * nblk + j)),
            even, even, odd, odd, row, row,
        ],
        out_specs=pl.BlockSpec((1, s, tn), lambda bi, j: (bi, 0, j)),
        out_shape=jax.ShapeDtypeStruct((b, s, HY_D), BF16),
        scratch_shapes=[pltpu.VMEM((hh, tn), BF16)] * 7,
        compiler_params=_cparams(("parallel", "parallel"), est),
    )(tabs["ce"], tabs["se"], tabs["co"], tabs["so"], tabs["cot"], tabs["sot"], tabs["flip"], usrc, gsrc,
      a, bq, a, bq, knyq, skip.reshape(HY_ORDER, 1, HY_D))


def _rope_tables(seq, rot_dim):
    half = rot_dim // 2
    pos = jnp.arange(seq, dtype=F32)
    inv = ROPE_THETA ** (-jnp.arange(0, rot_dim, 2, dtype=F32) / rot_dim)
    ang = pos[:, None] * inv[None, :]
    cos, sin = jnp.cos(ang), jnp.sin(ang)
    rest = LANE - rot_dim
    return cos, sin, half, rest


def _rope_lane_tables(seq, rot_dim, rest_passthrough):
    cos, sin, half, rest = _rope_tables(seq, rot_dim)
    fill = jnp.ones((seq, rest), F32) if rest_passthrough else jnp.zeros((seq, rest), F32)
    zero_h = jnp.zeros((seq, half), F32)
    zero_r = jnp.zeros((seq, rest), F32)
    cos_t = jnp.concatenate([cos, cos, fill], axis=1)
    sin_up = jnp.concatenate([zero_h, sin, zero_r], axis=1)
    sin_dn = jnp.concatenate([-sin, zero_h, zero_r], axis=1)
    return cos_t, sin_up, sin_dn


def _dft_tables(n_half):
    hh = n_half // 2
    blk = 32
    idx = jnp.arange(n_half, dtype=jnp.int32)
    unit = 2.0 * math.pi / n_half
    ang_a = ((blk * idx[:hh // blk, None] * idx[None, :]) & (n_half - 1)).astype(F32) * unit
    ang_b = ((idx[:blk, None] * idx[None, :]) & (n_half - 1)).astype(F32) * unit
    ca, sa = jnp.cos(ang_a)[:, None, :], jnp.sin(ang_a)[:, None, :]
    cb, sb = jnp.cos(ang_b)[None], jnp.sin(ang_b)[None]
    ce_f = (ca * cb - sa * sb).reshape(hh, n_half)
    se_f = (sa * cb + ca * sb).reshape(hh, n_half)
    turn = idx.astype(F32) * (0.5 * unit)
    c_row, s_row = jnp.cos(turn)[None, :], jnp.sin(turn)[None, :]
    co_f = ce_f * c_row - se_f * s_row
    so_f = se_f * c_row + ce_f * s_row
    alt = (1 - 2 * (idx & 1)).astype(F32)
    ce, se = ce_f[:, :hh], se_f[:, :hh]
    c_col, s_col = jnp.cos(turn[:hh])[:, None], jnp.sin(turn[:hh])[:, None]
    flip = (idx[:hh, None] >= 1) & (idx[None, :hh] == hh - idx[:hh, None])
    tabs = dict(
        ct_eo=jnp.concatenate([ce_f, co_f], axis=0),
        st_eo=jnp.concatenate([jnp.where(idx[:hh, None] == 0, alt[None, :], se_f), so_f], axis=0),
        ce=ce, se=se, co=co_f[:, :hh], so=so_f[:, :hh],
        cot=ce * c_col - se * s_col, sot=se * c_col + ce * s_col, flip=flip)
    return {name: t.astype(BF16) for name, t in tabs.items()}


def _filter_features(n_lag):
    t = jnp.linspace(0.0, 1.0, n_lag, dtype=F32)[:, None]
    bands = (HY_EMB - 1) // 2
    wpos = 2.0 * math.pi * jnp.arange(n_lag, dtype=F32) / n_lag
    fb = jnp.linspace(1e-4, bands - 1, bands, dtype=F32)
    fw = wpos[:, None] * fb[None, :]
    z = jnp.concatenate([t, jnp.cos(fw), -jnp.sin(fw)], axis=-1)
    return jnp.pad(z, ((0, 0), (0, LANE - HY_EMB)))


def kernel(x, c, ada_mix_w, ada_mix_b, norm_mix_g, ada_mlp_w, ada_mlp_b, norm_mlp_g, w_mlp_in, w_mlp_out, e_w_in, e_q_norm_g, e_kv_norm_g, e_w_uq, e_w_ukv, e_conv_w, e_conv_b, e_f_w1, e_f_b1, e_f_w2, e_f_b2, e_f_w3, e_f_b3, e_f_freq, e_f_w4, e_hy_skip, e_w_out, o_w_qkv, o_sinks, o_w_o, final_norm_g):
    b, s, d = x.shape
    c8 = jnp.pad(c, ((0, 8 - b), (0, 0)))
    mod_mix = _ada(c8, ada_mix_w, ada_mix_b)[:, :b].reshape(DEPTH, b, 1, 3 * d)
    mod_mlp = _ada(c8, ada_mlp_w, ada_mlp_b)[:, :b].reshape(DEPTH, b, 1, 3 * d)

    mla_tabs = _rope_lane_tables(s, MLA_ROPE, rest_passthrough=False)
    gqa_cos, gqa_sin_up, gqa_sin_dn = _rope_lane_tables(s, GQA_ROT, rest_passthrough=True)
    gqa_tabs = (gqa_cos, gqa_sin_up + gqa_sin_dn)
    dft = _dft_tables(s)
    z_feat = _filter_features(s)
    q_scale = (MLA_NOPE + MLA_ROPE) ** -0.5 * math.log2(math.e)
    w_lat_all = e_w_in[:, :, :LAT_W]
    w_hy_all = e_w_in[:, :, Q_LORA + KV_LORA + MLA_ROPE:]

    for l in range(DEPTH):
        i = l // 2
        if l % 2 == 0:
            h = _normmod(x, mod_mix[l], norm_mix_g[l])
            lat = _matmul(h, w_lat_all, i, F32, tn=LAT_W)
            u = _proj_short_conv(h, w_hy_all, i, e_conv_w[i], e_conv_b[i])

            w_uq = e_w_uq[i].reshape(Q_LORA, MLA_HEADS, MLA_NOPE + MLA_ROPE)
            wq = jnp.concatenate([
                w_uq[:, :, :MLA_NOPE].reshape(Q_LORA, -1),
                jnp.pad(w_uq[:, :, MLA_NOPE:], ((0, 0), (0, 0), (0, LANE - MLA_ROPE))).reshape(Q_LORA, -1),
            ], axis=1).astype(BF16)
            w_ukv = e_w_ukv[i].reshape(KV_LORA, MLA_HEADS, MLA_NOPE + MLA_V)
            wk = w_ukv[:, :, :MLA_NOPE].reshape(KV_LORA, -1).astype(BF16)
            wv = w_ukv[:, :, MLA_NOPE:].reshape(KV_LORA, -1).astype(BF16)
            q, k, v = _mla_proj(lat, e_q_norm_g[i], e_kv_norm_g[i], wq, wk, wv, mla_tabs, q_scale)
            a_mla = _mla_attn(q, k, v)

            w1 = jnp.pad(e_f_w1[i], ((0, LANE - HY_EMB), (0, 0)))
            h_filt = _filter_mlp(z_feat, w1, e_f_b1[i], e_f_w2[i], e_f_b2[i], e_f_w3[i], e_f_b3[i], e_f_freq[i])
            ksum, kdiff, knyq = _filter_gen(h_filt, e_f_w4[i])
            spec = (*_filter_dft(dft["ct_eo"], dft["st_eo"], ksum, kdiff), knyq)
            zc = _long_conv(dft, spec, 0, u, 0, u, 1, e_hy_skip[i])
            b_hy = _long_conv(dft, spec, 1, zc, 0, u, 2, e_hy_skip[i])

            x = _proj_res([a_mla, b_hy], e_w_out, i, x, mod_mix[l])
        else:
            h = _normmod(x, mod_mix[l], norm_mix_g[l])
            qkv = _matmul(h, o_w_qkv, i, BF16)
            o = _gqa_attn(qkv, o_sinks[i], gqa_tabs)
            x = _proj_res([o], o_w_o, i, x, mod_mix[l])
        x = _mlp(x, mod_mlp[l], norm_mlp_g[l], w_mlp_in, w_mlp_out, l, final_norm_g, l == DEPTH - 1)
    return x
```

```python
import functools
import math

import jax
import jax.numpy as jnp
from jax import lax
from jax.experimental import pallas as pl
from jax.experimental.pallas import tpu as pltpu

F32 = jnp.float32
BF16 = jnp.bfloat16
HIGHEST = lax.Precision.HIGHEST

D_MODEL = 2048
BATCH = 4
SEQ = 2048
DEPTH = 4
RMS_EPS = 1e-6
ROPE_THETA = 500000.0
NEG_INF = -1e30
MLA_HEADS = 8
MLA_NOPE = 128
MLA_ROPE = 64
MLA_V = 128
Q_LORA = 512
KV_LORA = 256
HY_D = 1024
HY_ORDER = 2
HY_SHORT = 3
HY_EMB = 33
HY_FFN = 64
HY_DECAY_PCT_SHORT = 0.3
HY_DECAY_PCT_LONG = 1.5
HY_TARGET = 1e-2
GQA_HEADS = 16
GQA_KV_HEADS = 4
GQA_HEAD_DIM = 128
GQA_ROT = GQA_HEAD_DIM // 4
GQA_GROUP = GQA_HEADS // GQA_KV_HEADS
WINDOW = 128
BAND = 128
D_FF = 4 * D_MODEL
LAT_W = Q_LORA + KV_LORA + 128
MLA_QK = 256
DFT_N = 2 * SEQ
CONV_SUB_ROWS = 256

LANE = 128
V7X_VMEM_BYTES = 64 * 1024 * 1024
V7X_VMEM_BUDGET = 56 * 1024 * 1024


def _cparams(semantics, est_bytes):
    limit = int(min(V7X_VMEM_BUDGET, max(32 * 1024 * 1024, est_bytes * 3 // 2)))
    return pltpu.CompilerParams(dimension_semantics=semantics, vmem_limit_bytes=limit)


def _nbytes(shape, dtype):
    return math.prod(shape) * jnp.dtype(dtype).itemsize


def _rms(x, g):
    ms = jnp.mean(x * x, axis=-1, keepdims=True)
    return x * lax.rsqrt(ms + RMS_EPS) * g


def _ada_kernel(c_ref, w_ref, b_ref, o_ref):
    cv = c_ref[...]
    s = cv * (1.0 / (1.0 + jnp.exp(-cv)))
    o_ref[0] = jnp.dot(s.astype(BF16), w_ref[0].astype(BF16), preferred_element_type=F32) + b_ref[0]


def _ada(c8, w, b, tn=1536):
    n_l, d, n = w.shape
    est = 2 * _nbytes((d, tn), F32) + _nbytes((d, tn), BF16) + 4 * _nbytes((8, tn), F32) + 2 * _nbytes((8, d), F32)
    return pl.pallas_call(
        _ada_kernel,
        name="ada_modulation",
        grid=(n_l, n // tn),
        in_specs=[
            pl.BlockSpec((8, d), lambda l, j: (0, 0)),
            pl.BlockSpec((1, d, tn), lambda l, j: (l, 0, j)),
            pl.BlockSpec((1, 1, tn), lambda l, j: (l, 0, j)),
        ],
        out_specs=pl.BlockSpec((1, 8, tn), lambda l, j: (l, 0, j)),
        out_shape=jax.ShapeDtypeStruct((n_l, 8, n), F32),
        compiler_params=_cparams(("parallel", "parallel"), est),
    )(c8, w, b.reshape(n_l, 1, n))


def _normmod_rows(x_ref, shift_ref, scale_ref, g_ref, h_ref, rows):
    gain = g_ref[...] * (1.0 + scale_ref[0])
    sh = shift_ref[0]
    tm = h_ref.shape[0]

    def body(r, carry):
        sl = pl.ds(pl.multiple_of(r * rows, rows), rows)
        xv = x_ref[0, sl, :]
        inv = lax.rsqrt(jnp.mean(xv * xv, axis=-1, keepdims=True) + RMS_EPS)
        h_ref[sl, :] = (xv * inv * gain + sh).astype(BF16)
        return carry

    lax.fori_loop(0, tm // rows, body, 0)


def _normmod_kernel(x_ref, shift_ref, scale_ref, g_ref, h_ref):
    _normmod_rows(x_ref, shift_ref, scale_ref, g_ref, h_ref.at[0], 128)


def _normmod(x, mod, g, tm=512):
    b, s, d = x.shape
    est = 2 * _nbytes((tm, d), F32) + 2 * _nbytes((tm, d), BF16) + 8 * _nbytes((128, d), F32)
    return pl.pallas_call(
        _normmod_kernel,
        name="normmod",
        grid=(b, s // tm),
        in_specs=[
            pl.BlockSpec((1, tm, d), lambda bi, i: (bi, i, 0)),
            pl.BlockSpec((1, 1, d), lambda bi, i: (bi, 0, 0)),
            pl.BlockSpec((1, 1, d), lambda bi, i: (bi, 0, 1)),
            pl.BlockSpec((1, d), lambda bi, i: (0, 0)),
        ],
        out_specs=pl.BlockSpec((1, tm, d), lambda bi, i: (bi, i, 0)),
        out_shape=jax.ShapeDtypeStruct((b, s, d), BF16),
        compiler_params=_cparams(("parallel", "parallel"), est),
    )(x, mod, mod, g.reshape(1, d))


_NT_DIMS = (((1,), (1,)), ((), ()))


def _matmul_kernel(h_ref, w_ref, o_ref, *, w_transposed):
    w = w_ref[...].astype(BF16)
    if w_transposed:
        acc = lax.dot_general(h_ref[0], w, _NT_DIMS, preferred_element_type=F32)
    else:
        acc = jnp.dot(h_ref[0], w, preferred_element_type=F32)
    o_ref[0] = acc.astype(o_ref.dtype)


def _matmul(h, w_all, layer, out_dtype, tm=2048, tn=512, n_cols=None, w_transposed=False):
    b, s, d = h.shape
    n = (w_all.shape[1] if w_transposed else w_all.shape[2]) if n_cols is None else n_cols
    tn = min(tn, n)
    est = (2 * _nbytes((tm, d), BF16) + 2 * _nbytes((d, tn), F32) + _nbytes((d, tn), BF16)
           + 3 * _nbytes((tm, tn), F32))
    if w_transposed:
        w_spec = pl.BlockSpec((None, tn, d), lambda bi, i, j: (layer, j, 0))
    else:
        w_spec = pl.BlockSpec((None, d, tn), lambda bi, i, j: (layer, 0, j))
    return pl.pallas_call(
        functools.partial(_matmul_kernel, w_transposed=w_transposed),
        name="matmul",
        grid=(b, s // tm, n // tn),
        in_specs=[
            pl.BlockSpec((1, tm, d), lambda bi, i, j: (bi, i, 0)),
            w_spec,
        ],
        out_specs=pl.BlockSpec((1, tm, tn), lambda bi, i, j: (bi, i, j)),
        out_shape=jax.ShapeDtypeStruct((b, s, n), out_dtype),
        compiler_params=_cparams(("parallel", "parallel", "parallel"), est),
    )(h, w_all)


def _mlp_kernel(x_ref, shift_ref, scale_ref, gate_ref, g_ref, w1_ref, w2_ref, fg_ref, o_ref, h_ref, *, final_norm):
    f = pl.program_id(2)

    @pl.when(f == 0)
    def _():
        _normmod_rows(x_ref, shift_ref, scale_ref, g_ref, h_ref, 128)

        o_ref[...] = jnp.zeros_like(o_ref)

    a = jnp.dot(h_ref[...], w1_ref[...].astype(BF16), preferred_element_type=F32)
    a = jnp.square(jnp.maximum(a, 0.0)).astype(BF16)
    o_ref[0] += jnp.dot(a, w2_ref[...].astype(BF16), preferred_element_type=F32)

    @pl.when(f == pl.num_programs(2) - 1)
    def _():
        gate = gate_ref[0]
        fg = fg_ref[...]
        rows = 128

        def body(r, carry):
            sl = pl.ds(pl.multiple_of(r * rows, rows), rows)
            y = x_ref[0, sl, :] + gate * o_ref[0, sl, :]
            if final_norm:
                y = _rms(y, fg)
            o_ref[0, sl, :] = y
            return carry

        lax.fori_loop(0, o_ref.shape[1] // rows, body, 0)


def _mlp(x, mod, g, w1_all, w2_all, layer, final_g, final_norm, tm=1024, tf=512):
    b, s, d = x.shape
    ff = w1_all.shape[2]
    est = (4 * _nbytes((tm, d), F32) + _nbytes((tm, d), BF16) + 4 * _nbytes((d, tf), F32)
           + 2 * _nbytes((d, tf), BF16) + 2 * _nbytes((tm, tf), F32))
    return pl.pallas_call(
        functools.partial(_mlp_kernel, final_norm=final_norm),
        name="mlp_relu2",
        grid=(b, s // tm, ff // tf),
        in_specs=[
            pl.BlockSpec((1, tm, d), lambda bi, i, f: (bi, i, 0)),
            pl.BlockSpec((1, 1, d), lambda bi, i, f: (bi, 0, 0)),
            pl.BlockSpec((1, 1, d), lambda bi, i, f: (bi, 0, 1)),
            pl.BlockSpec((1, 1, d), lambda bi, i, f: (bi, 0, 2)),
            pl.BlockSpec((1, d), lambda bi, i, f: (0, 0)),
            pl.BlockSpec((None, d, tf), lambda bi, i, f: (layer, 0, f)),
            pl.BlockSpec((None, tf, d), lambda bi, i, f: (layer, f, 0)),
            pl.BlockSpec((1, d), lambda bi, i, f: (0, 0)),
        ],
        out_specs=pl.BlockSpec((1, tm, d), lambda bi, i, f: (bi, i, 0)),
        out_shape=jax.ShapeDtypeStruct((b, s, d), F32),
        scratch_shapes=[pltpu.VMEM((tm, d), BF16)],
        compiler_params=_cparams(("parallel", "parallel", "arbitrary"), est),
    )(x, mod, mod, mod, g.reshape(1, d), w1_all, w2_all, final_g.reshape(1, d))


def _proj_res_kernel(*refs, n_in):
    a_refs, w_refs = refs[:n_in], refs[n_in:2 * n_in]
    x_ref, gate_ref, o_ref = refs[2 * n_in:]
    acc = None
    for a_ref, w_ref in zip(a_refs, w_refs):
        p = jnp.dot(a_ref[0], w_ref[...].astype(BF16), preferred_element_type=F32)
        acc = p if acc is None else acc + p
    o_ref[0] = x_ref[0] + gate_ref[0] * acc


def _proj_res(a_list, w_all, layer, x, mod, tm=2048, tn=512):
    b, s, d = x.shape
    n_in = len(a_list)
    kk = a_list[0].shape[-1]
    assert all(a.shape[-1] == kk for a in a_list) and w_all.shape[1] == n_in * kk
    est = (n_in * (2 * _nbytes((tm, kk), BF16) + 2 * _nbytes((kk, tn), F32) + _nbytes((kk, tn), BF16))
           + 6 * _nbytes((tm, tn), F32))
    in_specs = [pl.BlockSpec((1, tm, kk), lambda bi, i, j: (bi, i, 0)) for _ in a_list]
    in_specs += [pl.BlockSpec((None, kk, tn), functools.partial(lambda bi, i, j, r: (layer, r, j), r=r))
                 for r in range(n_in)]
    in_specs += [
        pl.BlockSpec((1, tm, tn), lambda bi, i, j: (bi, i, j)),
        pl.BlockSpec((1, 1, tn), lambda bi, i, j: (bi, 0, 2 * (d // tn) + j)),
    ]
    return pl.pallas_call(
        functools.partial(_proj_res_kernel, n_in=n_in),
        name="proj_residual",
        grid=(b, s // tm, d // tn),
        in_specs=in_specs,
        out_specs=pl.BlockSpec((1, tm, tn), lambda bi, i, j: (bi, i, j)),
        out_shape=jax.ShapeDtypeStruct((b, s, d), F32),
        compiler_params=_cparams(("parallel", "parallel", "parallel"), est),
    )(*a_list, *([w_all] * n_in), x, mod)


def _rot_lanes(blk, cos_t, sin_up, sin_dn, half):
    return (blk * cos_t + pltpu.roll(blk, half, 1) * sin_up + pltpu.roll(blk, LANE - half, 1) * sin_dn)


def _mla_proj_kernel(lat_ref, gq_ref, gkv_ref, wq_ref, wk_ref, wv_ref, cos_ref, sup_ref, sdn_ref,
                     q_ref, k_ref, v_ref, *, q_scale):
    lat = lat_ref[0]
    qn = _rms(lat[:, :Q_LORA], gq_ref[...]).astype(BF16)
    kvn = _rms(lat[:, Q_LORA:Q_LORA + KV_LORA], gkv_ref[...]).astype(BF16)
    cos_t, sin_up, sin_dn = cos_ref[...], sup_ref[...], sdn_ref[...]
    half = MLA_ROPE // 2
    q = jnp.dot(qn, wq_ref[...], preferred_element_type=F32) * q_scale
    kn = jnp.dot(kvn, wk_ref[...], preferred_element_type=F32)
    v = jnp.dot(kvn, wv_ref[...], preferred_element_type=F32).astype(BF16)
    ones = jnp.ones((v.shape[0], MLA_V), BF16)
    kr = _rot_lanes(lat[:, Q_LORA + KV_LORA:], cos_t, sin_up, sin_dn, half).astype(BF16)
    nope_w = MLA_HEADS * MLA_NOPE
    for h in range(MLA_HEADS):
        c0 = h * MLA_QK
        v_ref[0, :, 2 * h * MLA_V:(2 * h + 1) * MLA_V] = v[:, h * MLA_V:(h + 1) * MLA_V]
        v_ref[0, :, (2 * h + 1) * MLA_V:(2 * h + 2) * MLA_V] = ones
        q_ref[0, :, c0:c0 + LANE] = q[:, h * LANE:(h + 1) * LANE].astype(BF16)
        qr = q[:, nope_w + h * LANE:nope_w + (h + 1) * LANE]
        q_ref[0, :, c0 + LANE:c0 + 2 * LANE] = _rot_lanes(qr, cos_t, sin_up, sin_dn, half).astype(BF16)
        k_ref[0, :, c0:c0 + LANE] = kn[:, h * LANE:(h + 1) * LANE].astype(BF16)
        k_ref[0, :, c0 + LANE:c0 + 2 * LANE] = kr


def _mla_proj(lat, gq, gkv, wq_all, wk_all, wv_all, layer, tabs, q_scale, tm=512):
    b, s, _ = lat.shape
    hq = MLA_HEADS * MLA_QK
    hv = MLA_HEADS * 2 * MLA_V
    est = (2 * _nbytes((tm, LAT_W), F32) + 2 * _nbytes(wq_all.shape[1:], BF16) + 2 * _nbytes(wk_all.shape[1:], BF16)
           + 2 * _nbytes(wv_all.shape[1:], BF16) + 4 * _nbytes((tm, hq), BF16) + 2 * _nbytes((tm, hv), BF16)
           + 3 * _nbytes((tm, hq), F32))
    full = lambda shape: pl.BlockSpec(shape, lambda bi, i: (0,) * len(shape))
    stacked = lambda w: pl.BlockSpec((None,) + w.shape[1:], lambda bi, i: (layer, 0, 0))
    tab = pl.BlockSpec((tm, LANE), lambda bi, i: (i, 0))
    return pl.pallas_call(
        functools.partial(_mla_proj_kernel, q_scale=q_scale),
        name="mla_proj",
        grid=(b, s // tm),
        in_specs=[
            pl.BlockSpec((1, tm, LAT_W), lambda bi, i: (bi, i, 0)),
            full((1, Q_LORA)), full((1, KV_LORA)), stacked(wq_all), stacked(wk_all), stacked(wv_all),
            tab, tab, tab,
        ],
        out_specs=[
            pl.BlockSpec((1, tm, hq), lambda bi, i: (bi, i, 0)),
            pl.BlockSpec((1, tm, hq), lambda bi, i: (bi, i, 0)),
            pl.BlockSpec((1, tm, hv), lambda bi, i: (bi, i, 0)),
        ],
        out_shape=[
            jax.ShapeDtypeStruct((b, s, hq), BF16),
            jax.ShapeDtypeStruct((b, s, hq), BF16),
            jax.ShapeDtypeStruct((b, s, hv), BF16),
        ],
        compiler_params=_cparams(("parallel", "parallel"), est),
    )(lat, gq.reshape(1, -1), gkv.reshape(1, -1), wq_all, wk_all, wv_all, *tabs)


def _mla_attn_kernel(q_ref, k_ref, v_ref, o_ref, *, chain_rows):
    n_chains = q_ref.shape[1] // chain_rows

    def scores(c):
        rs = slice(c * chain_rows, (c + 1) * chain_rows)
        return lax.dot_general(q_ref[0, rs, :], k_ref[0], (((1,), (1,)), ((), ())), preferred_element_type=F32)

    s_next = scores(0)
    for c in range(n_chains):
        s = s_next
        if c + 1 < n_chains:
            s_next = scores(c + 1)
        m = jnp.max(s, axis=-1, keepdims=True)
        p = jnp.exp2(s - m).astype(BF16)
        oe = jnp.dot(p, v_ref[0], preferred_element_type=F32)
        o_ref[0, c * chain_rows:(c + 1) * chain_rows, :] = (oe[:, :MLA_V] / oe[:, MLA_V:]).astype(BF16)


def _mla_attn(q, k, v, tq=2048, chain_rows=1024):
    b, s, _ = q.shape
    est = (2 * _nbytes((tq, MLA_QK), BF16) + 2 * _nbytes((s, MLA_QK), BF16) + 2 * _nbytes((s, 2 * MLA_V), BF16)
           + 2 * _nbytes((tq, MLA_V), BF16) + 5 * _nbytes((chain_rows, s), F32))
    return pl.pallas_call(
        functools.partial(_mla_attn_kernel, chain_rows=chain_rows),
        name="mla_attn",
        grid=(b, MLA_HEADS, s // tq),
        in_specs=[
            pl.BlockSpec((1, tq, MLA_QK), lambda bi, h, i: (bi, i, h)),
            pl.BlockSpec((1, s, MLA_QK), lambda bi, h, i: (bi, 0, h)),
            pl.BlockSpec((1, s, 2 * MLA_V), lambda bi, h, i: (bi, 0, h)),
        ],
        out_specs=pl.BlockSpec((1, tq, MLA_V), lambda bi, h, i: (bi, i, h)),
        out_shape=jax.ShapeDtypeStruct((b, s, MLA_HEADS * MLA_V), BF16),
        compiler_params=_cparams(("parallel", "parallel", "parallel"), est),
    )(q, k, v)


def _gqa_kernel(q_ref, k_ref, v_ref, sink_ref, cos_ref, sin_ref, swap_ref, o_ref, k_scr, v_scr, *, scale):
    s_len = k_scr.shape[0]
    n_blocks = s_len // BAND
    rows = GQA_GROUP * BAND
    swap = swap_ref[...]

    def rot(x, cos_t, sin_t):
        return x.astype(F32) * cos_t + jnp.dot(x, swap, preferred_element_type=F32) * sin_t

    k_scr[...] = rot(k_ref[0], cos_ref[...], sin_ref[...]).astype(BF16)
    v_scr[:, :LANE] = v_ref[0]
    v_scr[:, LANE:] = jnp.ones((s_len, LANE), BF16)
    sink = sink_ref[0]
    rel = (lax.broadcasted_iota(jnp.int32, (rows, 3 * BAND), 1) - BAND
           - (lax.broadcasted_iota(jnp.int32, (rows, 3 * BAND), 0) & (BAND - 1)))
    bias = jnp.where(jnp.abs(rel) <= WINDOW, 0.0, NEG_INF)

    def window(n):
        r0 = n * BAND
        return max(0, r0 - BAND), min(s_len, r0 + 2 * BAND)

    def scores(n):
        r0 = n * BAND
        cos_t = jnp.concatenate([cos_ref[r0:r0 + BAND, :]] * GQA_GROUP, axis=0)
        sin_t = jnp.concatenate([sin_ref[r0:r0 + BAND, :]] * GQA_GROUP, axis=0)
        q_st = jnp.concatenate([q_ref[0, r0:r0 + BAND, g * LANE:(g + 1) * LANE] for g in range(GQA_GROUP)], axis=0)
        q_st = (rot(q_st, cos_t, sin_t) * scale).astype(BF16)
        lo, hi = window(n)
        c0 = lo - (r0 - BAND)
        s = lax.dot_general(q_st, k_scr[lo:hi, :], (((1,), (1,)), ((), ())), preferred_element_type=F32)
        return s + bias[:, c0:c0 + hi - lo]

    s_next = scores(0)
    for n in range(n_blocks):
        s = s_next
        if n + 1 < n_blocks:
            s_next = scores(n + 1)
        r0 = n * BAND
        lo, hi = window(n)
        m = jnp.maximum(jnp.max(s, axis=-1, keepdims=True), sink)
        p = jnp.exp(s - m).astype(BF16)
        oe = jnp.dot(p, v_scr[lo:hi, :], preferred_element_type=F32)
        o = oe[:, :LANE] / (oe[:, LANE:] + jnp.exp(sink - m))
        for g in range(GQA_GROUP):
            o_ref[0, r0:r0 + BAND, g * LANE:(g + 1) * LANE] = o[g * BAND:(g + 1) * BAND, :].astype(BF16)


def _gqa_attn(qkv, sinks, tabs):
    b, s, _ = qkv.shape
    gw = GQA_GROUP * GQA_HEAD_DIM
    sink_col = jnp.broadcast_to(sinks.astype(F32).reshape(GQA_KV_HEADS, GQA_GROUP, 1, 1),
                                (GQA_KV_HEADS, GQA_GROUP, BAND, 1)).reshape(GQA_KV_HEADS, GQA_GROUP * BAND, 1)
    est = (4 * _nbytes((s, gw), BF16) + 7 * _nbytes((s, LANE), BF16) + 6 * _nbytes((s, LANE), F32)
           + 8 * _nbytes((GQA_GROUP * BAND, 3 * BAND), F32))
    tab = pl.BlockSpec((s, LANE), lambda bi, h: (0, 0))
    cos_t, sin_t = tabs
    half = GQA_ROT // 2
    lane = jnp.arange(LANE)
    swap = (((lane[None, :] < half) & (lane[:, None] == lane[None, :] + half))
            | ((lane[None, :] >= half) & (lane[None, :] < 2 * half) & (lane[:, None] == lane[None, :] - half)))
    return pl.pallas_call(
        functools.partial(_gqa_kernel, scale=GQA_HEAD_DIM ** -0.5),
        name="gqa_window_attn",
        grid=(b, GQA_KV_HEADS),
        in_specs=[
            pl.BlockSpec((1, s, gw), lambda bi, h: (bi, 0, h)),
            pl.BlockSpec((1, s, LANE), lambda bi, h: (bi, 0, GQA_HEADS + h)),
            pl.BlockSpec((1, s, LANE), lambda bi, h: (bi, 0, GQA_HEADS + GQA_KV_HEADS + h)),
            pl.BlockSpec((1, GQA_GROUP * BAND, 1), lambda bi, h: (h, 0, 0)),
            tab, tab,
            pl.BlockSpec((LANE, LANE), lambda bi, h: (0, 0)),
        ],
        out_specs=pl.BlockSpec((1, s, gw), lambda bi, h: (bi, 0, h)),
        out_shape=jax.ShapeDtypeStruct((b, s, GQA_HEADS * GQA_HEAD_DIM), BF16),
        scratch_shapes=[pltpu.VMEM((s, LANE), BF16), pltpu.VMEM((s, 2 * LANE), BF16)],
        compiler_params=_cparams(("parallel", "parallel"), est),
    )(qkv, qkv, qkv, sink_col, cos_t, sin_t, swap.astype(BF16))


def _proj_short_conv_kernel(h_ref, w_ref, cw_ref, cb_ref, o_ref, *, sub_cols):
    tn = o_ref.shape[-1]
    h = h_ref[0]

    def project(c0):
        return lax.dot_general(h, w_ref[0, c0:c0 + sub_cols, :].astype(BF16), _NT_DIMS, preferred_element_type=F32)

    x_next = project(0)
    for c0 in range(0, tn, sub_cols):
        x = x_next
        if c0 + sub_cols < tn:
            x_next = project(c0 + sub_cols)
        cols = slice(c0, c0 + sub_cols)
        s_len = x.shape[0]
        row = lax.broadcasted_iota(jnp.int32, x.shape, 0)
        prev = jnp.where(row == 0, 0.0, pltpu.roll(x, 1, 0))
        nxt = jnp.where(row == s_len - 1, 0.0, pltpu.roll(x, s_len - 1, 0))
        y = cb_ref[:, cols] + prev * cw_ref[0:1, cols]
        y = y + x * cw_ref[1:2, cols]
        y = y + nxt * cw_ref[2:3, cols]
        o_ref[0, :, cols] = y.astype(BF16)


def _proj_short_conv(h, wt_all, layer, row0, n, cw, cb, tn=1024, sub_cols=512):
    b, s, d = h.shape
    assert row0 % 8 == 0 and n % tn == 0
    est = (2 * _nbytes((s, d), BF16) + 2 * _nbytes((d, tn), F32) + 2 * _nbytes((d, sub_cols), BF16)
           + 2 * _nbytes((s, tn), BF16) + 8 * _nbytes((s, sub_cols), F32))
    return pl.pallas_call(
        functools.partial(_proj_short_conv_kernel, sub_cols=sub_cols),
        name="hyena_proj_short_conv",
        grid=(b, n // tn),
        in_specs=[
            pl.BlockSpec((1, s, d), lambda bi, j: (bi, 0, 0)),
            pl.BlockSpec((pl.Element(1), pl.Element(tn), pl.Element(d)),
                         lambda bi, j: (layer, pl.multiple_of(row0 + j * tn, 8), 0)),
            pl.BlockSpec((HY_SHORT, tn), lambda bi, j: (0, j)),
            pl.BlockSpec((1, tn), lambda bi, j: (0, j)),
        ],
        out_specs=pl.BlockSpec((1, s, tn), lambda bi, j: (bi, 0, j)),
        out_shape=jax.ShapeDtypeStruct((b, s, n), BF16),
        compiler_params=_cparams(("parallel", "parallel"), est),
    )(h, wt_all, cw, cb.reshape(1, n))


def _filter_mlp_kernel(z_ref, w1_ref, b1_ref, w2_ref, b2_ref, w3_ref, b3_ref, fr_ref, h_ref):
    dot = functools.partial(jnp.dot, preferred_element_type=F32, precision=HIGHEST)
    fr = fr_ref[...]
    h = jnp.sin(fr * (dot(z_ref[...], w1_ref[...]) + b1_ref[...]))
    h = jnp.sin(fr * (dot(h, w2_ref[...]) + b2_ref[...]))
    h_ref[...] = jnp.sin(fr * (dot(h, w3_ref[...]) + b3_ref[...]))


def _filter_mlp(z, w1, b1, w2, b2, w3, b3, freq):
    n_lag = z.shape[0]
    full = lambda shape: pl.BlockSpec(shape, lambda i: (0,) * len(shape))
    est = 12 * _nbytes((n_lag, LANE), F32)
    return pl.pallas_call(
        _filter_mlp_kernel,
        name="hyena_filter_mlp",
        grid=(1,),
        in_specs=[full(z.shape), full(w1.shape), full((1, HY_FFN)), full(w2.shape), full((1, HY_FFN)),
                  full(w3.shape), full((1, HY_FFN)), full((1, HY_FFN))],
        out_specs=full((n_lag, HY_FFN)),
        out_shape=jax.ShapeDtypeStruct((n_lag, HY_FFN), F32),
        compiler_params=_cparams(("arbitrary",), est),
    )(z, w1, b1.reshape(1, -1), w2, b2.reshape(1, -1), w3, b3.reshape(1, -1), freq.reshape(1, -1))


def _filter_gen_kernel(h_ref, w4f_ref, w4b_ref, ksum_ref, kdiff_ref, nyq_ref):
    h = h_ref[...].astype(BF16)
    hf = jnp.dot(h, w4f_ref[...].astype(BF16), preferred_element_type=F32)
    hb = jnp.dot(h, w4b_ref[...].astype(BF16), preferred_element_type=F32)
    n_lag, tn = hf.shape
    row = lax.broadcasted_iota(jnp.int32, (n_lag, tn), 0)
    col = pl.program_id(1) * tn + lax.broadcasted_iota(jnp.int32, (1, tn), 1)
    t = row.astype(F32) / (n_lag - 1)
    max_decay = math.log(HY_TARGET) / HY_DECAY_PCT_SHORT
    min_decay = math.log(HY_TARGET) / HY_DECAY_PCT_LONG
    delta = min_decay + (max_decay - min_decay) * (col.astype(F32) / (HY_D - 1))
    decay = jnp.exp(-t * jnp.abs(delta))
    kf = hf * decay
    kb = jnp.where(row == 0, 0.0, hb * decay)
    inv = 1.0 / (jnp.sum(jnp.abs(kf), axis=0, keepdims=True) + jnp.sum(jnp.abs(kb), axis=0, keepdims=True))
    ksum = (kf + kb) * inv
    ksum_ref[0] = ksum.astype(BF16)
    kdiff_ref[0] = ((kb - kf) * inv).astype(BF16)
    alt = (1 - 2 * (row & 1)).astype(F32)
    nyq_ref[0] = jnp.sum(ksum * alt, axis=0, keepdims=True) * (1.0 / DFT_N)


def _filter_gen(h, w4, tn=256):
    n_lag = h.shape[0]
    nblk = HY_D // tn
    est = 10 * _nbytes((n_lag, tn), F32) + 4 * _nbytes((n_lag, LANE), F32) + 4 * _nbytes((n_lag, tn), BF16)
    out = jax.ShapeDtypeStruct((HY_ORDER, n_lag, HY_D), BF16)
    return pl.pallas_call(
        _filter_gen_kernel,
        name="hyena_filter_gen",
        grid=(HY_ORDER, nblk),
        in_specs=[
            pl.BlockSpec((n_lag, HY_FFN), lambda o, j: (0, 0)),
            pl.BlockSpec((HY_FFN, tn), lambda o, j: (0, 2 * o * nblk + j)),
            pl.BlockSpec((HY_FFN, tn), lambda o, j: (0, (2 * o + 1) * nblk + j)),
        ],
        out_specs=[pl.BlockSpec((1, n_lag, tn), lambda o, j: (o, 0, j))] * 2
        + [pl.BlockSpec((1, 1, tn), lambda o, j: (o, 0, j))],
        out_shape=[out, out, jax.ShapeDtypeStruct((HY_ORDER, 1, HY_D), F32)],
        compiler_params=_cparams(("parallel", "parallel"), est),
    )(h, w4, w4)


def _filter_dft_kernel(ct_ref, st_ref, ksum_ref, kdiff_ref, a_ref, bq_ref):
    re = jnp.dot(ct_ref[...], ksum_ref[0], preferred_element_type=F32)
    im = jnp.dot(st_ref[...], kdiff_ref[0], preferred_element_type=F32)
    tf = re.shape[0]
    row = pl.program_id(2) * tf + lax.broadcasted_iota(jnp.int32, re.shape, 0)
    is0 = row == 0
    wgt = jnp.where(is0, 1.0 / DFT_N, 2.0 / DFT_N)
    a_ref[0] = (re * wgt).astype(BF16)
    bq_ref[0] = jnp.where(is0, 0.0, im * wgt).astype(BF16)


def _filter_dft(ct, st, ksum, kdiff, tf=512, tn=512):
    n_f, n_s = ct.shape
    est = (4 * _nbytes((tf, n_s), BF16) + 4 * _nbytes((n_s, tn), BF16) + 4 * _nbytes((tf, tn), BF16)
           + 8 * _nbytes((tf, tn), F32))
    out = jax.ShapeDtypeStruct((HY_ORDER, n_f, HY_D), BF16)
    return pl.pallas_call(
        _filter_dft_kernel,
        name="hyena_filter_dft",
        grid=(HY_ORDER, HY_D // tn, n_f // tf),
        in_specs=[
            pl.BlockSpec((tf, n_s), lambda o, j, k: (k, 0)),
            pl.BlockSpec((tf, n_s), lambda o, j, k: (k, 0)),
            pl.BlockSpec((1, n_s, tn), lambda o, j, k: (o, 0, j)),
            pl.BlockSpec((1, n_s, tn), lambda o, j, k: (o, 0, j)),
        ],
        out_specs=[pl.BlockSpec((1, tf, tn), lambda o, j, k: (o, k, j))] * 2,
        out_shape=[out, out],
        compiler_params=_cparams(("parallel", "parallel", "parallel"), est),
    )(ct, st, ksum, kdiff)


def _alt_sign(idx):
    return (1 - 2 * (idx & 1)).astype(F32)


def _long_conv_kernel(ce_ref, se_ref, co_ref, so_ref, cot_ref, sot_ref, flip_ref, u_ref, g_ref,
                      ae_ref, bqe_ref, ao_ref, bqo_ref, knyq_ref, skip_ref, o_ref,
                      e_scr, o_scr, yre_scr, yse_scr, yro_scr, yso_scr, d_scr):
    hh = e_scr.shape[0]
    sub = CONV_SUB_ROWS
    dot = functools.partial(jnp.dot, preferred_element_type=F32)
    alt_h = _alt_sign(lax.broadcasted_iota(jnp.int32, (hh, 1), 0))

    u_lo = u_ref[0, :hh, :].astype(F32)
    u_rev = dot(flip_ref[...], u_ref[0, hh:, :])
    e32 = u_lo + u_rev
    e_scr[...] = e32.astype(BF16)
    o_scr[...] = (u_lo - u_rev).astype(BF16)
    mid = u_ref[0, hh:hh + 16, :].astype(F32)[0:1]
    nyq_u = jnp.sum(e32 * alt_h, axis=0, keepdims=True) + mid
    e, o = e_scr[...], o_scr[...]

    for r0 in range(0, hh, sub):
        rs = slice(r0, r0 + sub)
        m = r0 + lax.broadcasted_iota(jnp.int32, (sub, 1), 0)
        corr = _alt_sign(m) * mid
        uc = dot(ce_ref[rs, :], e) + corr
        us = dot(se_ref[rs, :], o)
        a, bq = ae_ref[0, rs, :].astype(F32), bqe_ref[0, rs, :].astype(F32)
        yre_scr[rs, :] = (uc * a + us * bq).astype(BF16)
        ys = us * a - uc * bq
        if r0 == 0:
            ys = jnp.where(m == 0, nyq_u * knyq_ref[0], ys)
        yse_scr[rs, :] = ys.astype(BF16)
        uc = dot(co_ref[rs, :], o)
        us = dot(so_ref[rs, :], e) + corr
        a, bq = ao_ref[0, rs, :].astype(F32), bqo_ref[0, rs, :].astype(F32)
        yro_scr[rs, :] = (uc * a + us * bq).astype(BF16)
        yso_scr[rs, :] = (us * a - uc * bq).astype(BF16)

    yre, yse, yro, yso = yre_scr[...], yse_scr[...], yro_scr[...], yso_scr[...]
    nyq = yse_scr[0:16, :].astype(F32)[0:1]
    y_mid = jnp.sum((yre.astype(F32) + yso.astype(F32)) * alt_h, axis=0, keepdims=True) + nyq
    skip = skip_ref[0]
    for r0 in range(0, hh, sub):
        rs = slice(r0, r0 + sub)
        t = r0 + lax.broadcasted_iota(jnp.int32, (sub, 1), 0)
        pa = dot(ce_ref[rs, :], yre) + dot(sot_ref[rs, :], yso) + _alt_sign(t) * nyq
        pb = dot(se_ref[rs, :], yse) + dot(cot_ref[rs, :], yro)
        u_t = u_ref[0, rs, :].astype(F32)
        o_ref[0, rs, :] = (g_ref[0, rs, :].astype(F32) * (pa + pb + u_t * skip)).astype(BF16)
        d_scr[rs, :] = (pa - pb).astype(BF16)
    y_hi = dot(flip_ref[...], d_scr[...])
    y_hi = jnp.where(lax.broadcasted_iota(jnp.int32, (hh, 1), 0) == 0, y_mid, y_hi)
    u_hi = u_ref[0, hh:, :].astype(F32)
    o_ref[0, hh:, :] = (g_ref[0, hh:, :].astype(F32) * (y_hi + u_hi * skip)).astype(BF16)


def _long_conv(tabs, spec, order, usrc, u_blk0, gsrc, g_blk0, skip, tn=512):
    a, bq, knyq = spec
    b, s, _ = usrc.shape
    hh = s // 2
    nblk = HY_D // tn
    est = (7 * _nbytes((hh, hh), BF16) + 10 * _nbytes((s, tn), BF16) + 8 * _nbytes((hh, tn), BF16)
           + 7 * _nbytes((hh, tn), BF16) + 8 * _nbytes((hh, tn), F32) + 12 * _nbytes((CONV_SUB_ROWS, tn), F32))
    tab = pl.BlockSpec((hh, hh), lambda bi, j: (0, 0), pipeline_mode=pl.Buffered(1))
    even = pl.BlockSpec((1, hh, tn), lambda bi, j: (order, 0, j))
    odd = pl.BlockSpec((1, hh, tn), lambda bi, j: (order, 1, j))
    row = pl.BlockSpec((1, 1, tn), lambda bi, j: (order, 0, j))
    return pl.pallas_call(
        _long_conv_kernel,
        name="hyena_long_conv",
        grid=(b, nblk),
        in_specs=[
            tab, tab, tab, tab, tab, tab, tab,
            pl.BlockSpec((1, s, tn), lambda bi, j: (bi, 0, u_blk0 * nblk + j)),
            pl.BlockSpec((1, s, tn), lambda bi, j: (bi, 0, g_blk0 * nblk + j)),
            even, even, odd, odd, row, row,
        ],
        out_specs=pl.BlockSpec((1, s, tn), lambda bi, j: (bi, 0, j)),
        out_shape=jax.ShapeDtypeStruct((b, s, HY_D), BF16),
        scratch_shapes=[pltpu.VMEM((hh, tn), BF16)] * 7,
        compiler_params=_cparams(("parallel", "parallel"), est),
    )(tabs["ce"], tabs["se"], tabs["co"], tabs["so"], tabs["cot"], tabs["sot"], tabs["flip"], usrc, gsrc,
      a, bq, a, bq, knyq, skip.reshape(HY_ORDER, 1, HY_D))


def _rope_tables(seq, rot_dim):
    half = rot_dim // 2
    pos = jnp.arange(seq, dtype=F32)
    inv = ROPE_THETA ** (-jnp.arange(0, rot_dim, 2, dtype=F32) / rot_dim)
    ang = pos[:, None] * inv[None, :]
    cos, sin = jnp.cos(ang), jnp.sin(ang)
    rest = LANE - rot_dim
    return cos, sin, half, rest


def _rope_lane_tables(seq, rot_dim, rest_passthrough):
    cos, sin, half, rest = _rope_tables(seq, rot_dim)
    fill = jnp.ones((seq, rest), F32) if rest_passthrough else jnp.zeros((seq, rest), F32)
    zero_h = jnp.zeros((seq, half), F32)
    zero_r = jnp.zeros((seq, rest), F32)
    cos_t = jnp.concatenate([cos, cos, fill], axis=1)
    sin_up = jnp.concatenate([zero_h, sin, zero_r], axis=1)
    sin_dn = jnp.concatenate([-sin, zero_h, zero_r], axis=1)
    return cos_t, sin_up, sin_dn


def _dft_tables(n_half):
    hh = n_half // 2
    blk = 32
    idx = jnp.arange(n_half, dtype=jnp.int32)
    unit = 2.0 * math.pi / n_half
    ang_a = ((blk * idx[:hh // blk, None] * idx[None, :]) & (n_half - 1)).astype(F32) * unit
    ang_b = ((idx[:blk, None] * idx[None, :]) & (n_half - 1)).astype(F32) * unit
    ca, sa = jnp.cos(ang_a)[:, None, :], jnp.sin(ang_a)[:, None, :]
    cb, sb = jnp.cos(ang_b)[None], jnp.sin(ang_b)[None]
    ce_f = (ca * cb - sa * sb).reshape(hh, n_half)
    se_f = (sa * cb + ca * sb).reshape(hh, n_half)
    turn = idx.astype(F32) * (0.5 * unit)
    c_row, s_row = jnp.cos(turn)[None, :], jnp.sin(turn)[None, :]
    co_f = ce_f * c_row - se_f * s_row
    so_f = se_f * c_row + ce_f * s_row
    alt = (1 - 2 * (idx & 1)).astype(F32)
    ce, se = ce_f[:, :hh], se_f[:, :hh]
    c_col, s_col = jnp.cos(turn[:hh])[:, None], jnp.sin(turn[:hh])[:, None]
    flip = (idx[:hh, None] >= 1) & (idx[None, :hh] == hh - idx[:hh, None])
    tabs = dict(
        ct_eo=jnp.concatenate([ce_f, co_f], axis=0),
        st_eo=jnp.concatenate([jnp.where(idx[:hh, None] == 0, alt[None, :], se_f), so_f], axis=0),
        ce=ce, se=se, co=co_f[:, :hh], so=so_f[:, :hh],
        cot=ce * c_col - se * s_col, sot=se * c_col + ce * s_col, flip=flip)
    return {name: t.astype(BF16) for name, t in tabs.items()}


def _filter_features(n_lag):
    t = jnp.linspace(0.0, 1.0, n_lag, dtype=F32)[:, None]
    bands = (HY_EMB - 1) // 2
    wpos = 2.0 * math.pi * jnp.arange(n_lag, dtype=F32) / n_lag
    fb = jnp.linspace(1e-4, bands - 1, bands, dtype=F32)
    fw = wpos[:, None] * fb[None, :]
    z = jnp.concatenate([t, jnp.cos(fw), -jnp.sin(fw)], axis=-1)
    return jnp.pad(z, ((0, 0), (0, LANE - HY_EMB)))


def kernel(x, c, ada_mix_w, ada_mix_b, norm_mix_g, ada_mlp_w, ada_mlp_b, norm_mlp_g, w_mlp_in, w_mlp_out, e_w_in, e_q_norm_g, e_kv_norm_g, e_w_uq, e_w_ukv, e_conv_w, e_conv_b, e_f_w1, e_f_b1, e_f_w2, e_f_b2, e_f_w3, e_f_b3, e_f_freq, e_f_w4, e_hy_skip, e_w_out, o_w_qkv, o_sinks, o_w_o, final_norm_g):
    b, s, d = x.shape
    c8 = jnp.pad(c, ((0, 8 - b), (0, 0)))
    mod_mix = _ada(c8, ada_mix_w, ada_mix_b)[:, :b].reshape(DEPTH, b, 1, 3 * d)
    mod_mlp = _ada(c8, ada_mlp_w, ada_mlp_b)[:, :b].reshape(DEPTH, b, 1, 3 * d)

    mla_tabs = _rope_lane_tables(s, MLA_ROPE, rest_passthrough=False)
    gqa_cos, gqa_sin_up, gqa_sin_dn = _rope_lane_tables(s, GQA_ROT, rest_passthrough=True)
    gqa_tabs = (gqa_cos, gqa_sin_up + gqa_sin_dn)
    dft = _dft_tables(s)
    z_feat = _filter_features(s)
    q_scale = (MLA_NOPE + MLA_ROPE) ** -0.5 * math.log2(math.e)
    w_in_t = jnp.swapaxes(e_w_in, 1, 2)
    hy_row0 = Q_LORA + KV_LORA + MLA_ROPE

    w_uq = e_w_uq.reshape(-1, Q_LORA, MLA_HEADS, MLA_NOPE + MLA_ROPE)
    wq_all = jnp.concatenate([
        w_uq[..., :MLA_NOPE].reshape(-1, Q_LORA, MLA_HEADS * MLA_NOPE),
        jnp.pad(w_uq[..., MLA_NOPE:], ((0, 0), (0, 0), (0, 0), (0, LANE - MLA_ROPE))).reshape(
            -1, Q_LORA, MLA_HEADS * LANE),
    ], axis=2).astype(BF16)
    w_ukv = e_w_ukv.reshape(-1, KV_LORA, MLA_HEADS, MLA_NOPE + MLA_V)
    wk_all = w_ukv[..., :MLA_NOPE].reshape(-1, KV_LORA, MLA_HEADS * MLA_NOPE).astype(BF16)
    wv_all = w_ukv[..., MLA_NOPE:].reshape(-1, KV_LORA, MLA_HEADS * MLA_V).astype(BF16)

    for l in range(DEPTH):
        i = l // 2
        if l % 2 == 0:
            h = _normmod(x, mod_mix[l], norm_mix_g[l])
            lat = _matmul(h, w_in_t, i, F32, tn=LAT_W, n_cols=LAT_W, w_transposed=True)
            u = _proj_short_conv(h, w_in_t, i, hy_row0, 3 * HY_D, e_conv_w[i], e_conv_b[i])

            q, k, v = _mla_proj(lat, e_q_norm_g[i], e_kv_norm_g[i], wq_all, wk_all, wv_all, i, mla_tabs, q_scale)
            a_mla = _mla_attn(q, k, v)

            w1 = jnp.pad(e_f_w1[i], ((0, LANE - HY_EMB), (0, 0)))
            h_filt = _filter_mlp(z_feat, w1, e_f_b1[i], e_f_w2[i], e_f_b2[i], e_f_w3[i], e_f_b3[i], e_f_freq[i])
            ksum, kdiff, knyq = _filter_gen(h_filt, e_f_w4[i])
            spec = (*_filter_dft(dft["ct_eo"], dft["st_eo"], ksum, kdiff), knyq)
            zc = _long_conv(dft, spec, 0, u, 0, u, 1, e_hy_skip[i])
            b_hy = _long_conv(dft, spec, 1, zc, 0, u, 2, e_hy_skip[i])

            x = _proj_res([a_mla, b_hy], e_w_out, i, x, mod_mix[l])
        else:
            h = _normmod(x, mod_mix[l], norm_mix_g[l])
            qkv = _matmul(h, o_w_qkv, i, BF16)
            o = _gqa_attn(qkv, o_sinks[i], gqa_tabs)
            x = _proj_res([o], o_w_o, i, x, mod_mix[l])
        x = _mlp(x, mod_mlp[l], norm_mlp_g[l], w_mlp_in, w_mlp_out, l, final_norm_g, l == DEPTH - 1)
    return x
```

```python
import functools
import math

import jax
import jax.numpy as jnp
from jax import lax
from jax.experimental import pallas as pl
from jax.experimental.pallas import tpu as pltpu

F32 = jnp.float32
BF16 = jnp.bfloat16
HIGHEST = lax.Precision.HIGHEST

D_MODEL = 2048
BATCH = 4
SEQ = 2048
DEPTH = 4
RMS_EPS = 1e-6
ROPE_THETA = 500000.0
NEG_INF = -1e30
MLA_HEADS = 8
MLA_NOPE = 128
MLA_ROPE = 64
MLA_V = 128
Q_LORA = 512
KV_LORA = 256
HY_D = 1024
HY_ORDER = 2
HY_SHORT = 3
HY_EMB = 33
HY_FFN = 64
HY_DECAY_PCT_SHORT = 0.3
HY_DECAY_PCT_LONG = 1.5
HY_TARGET = 1e-2
GQA_HEADS = 16
GQA_KV_HEADS = 4
GQA_HEAD_DIM = 128
GQA_ROT = GQA_HEAD_DIM // 4
GQA_GROUP = GQA_HEADS // GQA_KV_HEADS
WINDOW = 128
BAND = 128
D_FF = 4 * D_MODEL
LAT_W = Q_LORA + KV_LORA + 128
MLA_QK = 256
DFT_N = 2 * SEQ
CONV_SUB_ROWS = 256

LANE = 128
V7X_VMEM_BYTES = 64 * 1024 * 1024
V7X_VMEM_BUDGET = 56 * 1024 * 1024


def _cparams(semantics, est_bytes):
    limit = int(min(V7X_VMEM_BUDGET, max(32 * 1024 * 1024, est_bytes * 3 // 2)))
    return pltpu.CompilerParams(dimension_semantics=semantics, vmem_limit_bytes=limit)


def _nbytes(shape, dtype):
    return math.prod(shape) * jnp.dtype(dtype).itemsize


def _rms(x, g):
    ms = jnp.mean(x * x, axis=-1, keepdims=True)
    return x * lax.rsqrt(ms + RMS_EPS) * g


def _ada_kernel(c_ref, w_ref, b_ref, o_ref):
    cv = c_ref[...]
    s = cv * (1.0 / (1.0 + jnp.exp(-cv)))
    o_ref[0] = jnp.dot(s.astype(BF16), w_ref[0].astype(BF16), preferred_element_type=F32) + b_ref[0]


def _ada(c8, w, b, tn=1536):
    n_l, d, n = w.shape
    est = 2 * _nbytes((d, tn), F32) + _nbytes((d, tn), BF16) + 4 * _nbytes((8, tn), F32) + 2 * _nbytes((8, d), F32)
    return pl.pallas_call(
        _ada_kernel,
        name="ada_modulation",
        grid=(n_l, n // tn),
        in_specs=[
            pl.BlockSpec((8, d), lambda l, j: (0, 0)),
            pl.BlockSpec((1, d, tn), lambda l, j: (l, 0, j)),
            pl.BlockSpec((1, 1, tn), lambda l, j: (l, 0, j)),
        ],
        out_specs=pl.BlockSpec((1, 8, tn), lambda l, j: (l, 0, j)),
        out_shape=jax.ShapeDtypeStruct((n_l, 8, n), F32),
        compiler_params=_cparams(("parallel", "parallel"), est),
    )(c8, w, b.reshape(n_l, 1, n))


def _normmod_rows(x_ref, shift_ref, scale_ref, g_ref, h_ref, rows):
    gain = g_ref[...] * (1.0 + scale_ref[0])
    sh = shift_ref[0]
    tm = h_ref.shape[0]

    def body(r, carry):
        sl = pl.ds(pl.multiple_of(r * rows, rows), rows)
        xv = x_ref[0, sl, :]
        inv = lax.rsqrt(jnp.mean(xv * xv, axis=-1, keepdims=True) + RMS_EPS)
        h_ref[sl, :] = (xv * inv * gain + sh).astype(BF16)
        return carry

    lax.fori_loop(0, tm // rows, body, 0)


def _normmod_kernel(x_ref, shift_ref, scale_ref, g_ref, h_ref):
    _normmod_rows(x_ref, shift_ref, scale_ref, g_ref, h_ref.at[0], 128)


def _normmod(x, mod, g, tm=1024):
    b, s, d = x.shape
    est = 2 * _nbytes((tm, d), F32) + 2 * _nbytes((tm, d), BF16) + 8 * _nbytes((128, d), F32)
    return pl.pallas_call(
        _normmod_kernel,
        name="normmod",
        grid=(b, s // tm),
        in_specs=[
            pl.BlockSpec((1, tm, d), lambda bi, i: (bi, i, 0)),
            pl.BlockSpec((1, 1, d), lambda bi, i: (bi, 0, 0)),
            pl.BlockSpec((1, 1, d), lambda bi, i: (bi, 0, 1)),
            pl.BlockSpec((1, d), lambda bi, i: (0, 0)),
        ],
        out_specs=pl.BlockSpec((1, tm, d), lambda bi, i: (bi, i, 0)),
        out_shape=jax.ShapeDtypeStruct((b, s, d), BF16),
        compiler_params=_cparams(("parallel", "parallel"), est),
    )(x, mod, mod, g.reshape(1, d))


_NT_DIMS = (((1,), (1,)), ((), ()))


def _matmul_kernel(h_ref, w_ref, o_ref):
    o_ref[0] = jnp.dot(h_ref[0], w_ref[...].astype(BF16), preferred_element_type=F32).astype(o_ref.dtype)


def _matmul(h, w_all, layer, out_dtype, tm=2048, tn=1024):
    b, s, d = h.shape
    n = w_all.shape[2]
    est = (2 * _nbytes((tm, d), BF16) + 2 * _nbytes((d, tn), F32) + _nbytes((d, tn), BF16)
           + 3 * _nbytes((tm, tn), F32))
    return pl.pallas_call(
        _matmul_kernel,
        name="matmul",
        grid=(b, s // tm, n // tn),
        in_specs=[
            pl.BlockSpec((1, tm, d), lambda bi, i, j: (bi, i, 0)),
            pl.BlockSpec((None, d, tn), lambda bi, i, j: (layer, 0, j)),
        ],
        out_specs=pl.BlockSpec((1, tm, tn), lambda bi, i, j: (bi, i, j)),
        out_shape=jax.ShapeDtypeStruct((b, s, n), out_dtype),
        compiler_params=_cparams(("parallel", "parallel", "parallel"), est),
    )(h, w_all)


def _mlp_kernel(x_ref, shift_ref, scale_ref, gate_ref, g_ref, w1_ref, w2_ref, fg_ref, o_ref, h_ref, *, final_norm):
    f = pl.program_id(2)

    @pl.when(f == 0)
    def _():
        _normmod_rows(x_ref, shift_ref, scale_ref, g_ref, h_ref, 128)

        o_ref[...] = jnp.zeros_like(o_ref)

    a = jnp.dot(h_ref[...], w1_ref[...].astype(BF16), preferred_element_type=F32)
    a = jnp.square(jnp.maximum(a, 0.0)).astype(BF16)
    o_ref[0] += jnp.dot(a, w2_ref[...].astype(BF16), preferred_element_type=F32)

    @pl.when(f == pl.num_programs(2) - 1)
    def _():
        gate = gate_ref[0]
        fg = fg_ref[...]
        rows = 128

        def body(r, carry):
            sl = pl.ds(pl.multiple_of(r * rows, rows), rows)
            y = x_ref[0, sl, :] + gate * o_ref[0, sl, :]
            if final_norm:
                y = _rms(y, fg)
            o_ref[0, sl, :] = y
            return carry

        lax.fori_loop(0, o_ref.shape[1] // rows, body, 0)


def _mlp(x, mod, g, w1_all, w2_all, layer, final_g, final_norm, tm=1024, tf=512):
    b, s, d = x.shape
    ff = w1_all.shape[2]
    est = (4 * _nbytes((tm, d), F32) + _nbytes((tm, d), BF16) + 4 * _nbytes((d, tf), F32)
           + 2 * _nbytes((d, tf), BF16) + 2 * _nbytes((tm, tf), F32))
    return pl.pallas_call(
        functools.partial(_mlp_kernel, final_norm=final_norm),
        name="mlp_relu2",
        grid=(b, s // tm, ff // tf),
        in_specs=[
            pl.BlockSpec((1, tm, d), lambda bi, i, f: (bi, i, 0)),
            pl.BlockSpec((1, 1, d), lambda bi, i, f: (bi, 0, 0)),
            pl.BlockSpec((1, 1, d), lambda bi, i, f: (bi, 0, 1)),
            pl.BlockSpec((1, 1, d), lambda bi, i, f: (bi, 0, 2)),
            pl.BlockSpec((1, d), lambda bi, i, f: (0, 0)),
            pl.BlockSpec((None, d, tf), lambda bi, i, f: (layer, 0, f)),
            pl.BlockSpec((None, tf, d), lambda bi, i, f: (layer, f, 0)),
            pl.BlockSpec((1, d), lambda bi, i, f: (0, 0)),
        ],
        out_specs=pl.BlockSpec((1, tm, d), lambda bi, i, f: (bi, i, 0)),
        out_shape=jax.ShapeDtypeStruct((b, s, d), F32),
        scratch_shapes=[pltpu.VMEM((tm, d), BF16)],
        compiler_params=_cparams(("parallel", "parallel", "arbitrary"), est),
    )(x, mod, mod, mod, g.reshape(1, d), w1_all, w2_all, final_g.reshape(1, d))


def _proj_res_kernel(*refs, n_in):
    a_refs, w_refs = refs[:n_in], refs[n_in:2 * n_in]
    x_ref, gate_ref, o_ref = refs[2 * n_in:]
    acc = None
    for a_ref, w_ref in zip(a_refs, w_refs):
        p = jnp.dot(a_ref[0], w_ref[...].astype(BF16), preferred_element_type=F32)
        acc = p if acc is None else acc + p
    o_ref[0] = x_ref[0] + gate_ref[0] * acc


def _proj_res(a_list, w_all, layer, x, mod, tm=2048, tn=512):
    b, s, d = x.shape
    n_in = len(a_list)
    kk = a_list[0].shape[-1]
    assert all(a.shape[-1] == kk for a in a_list) and w_all.shape[1] == n_in * kk
    est = (n_in * (2 * _nbytes((tm, kk), BF16) + 2 * _nbytes((kk, tn), F32) + _nbytes((kk, tn), BF16))
           + 6 * _nbytes((tm, tn), F32))
    in_specs = [pl.BlockSpec((1, tm, kk), lambda bi, i, j: (bi, i, 0)) for _ in a_list]
    in_specs += [pl.BlockSpec((None, kk, tn), functools.partial(lambda bi, i, j, r: (layer, r, j), r=r))
                 for r in range(n_in)]
    in_specs += [
        pl.BlockSpec((1, tm, tn), lambda bi, i, j: (bi, i, j)),
        pl.BlockSpec((1, 1, tn), lambda bi, i, j: (bi, 0, 2 * (d // tn) + j)),
    ]
    return pl.pallas_call(
        functools.partial(_proj_res_kernel, n_in=n_in),
        name="proj_residual",
        grid=(b, s // tm, d // tn),
        in_specs=in_specs,
        out_specs=pl.BlockSpec((1, tm, tn), lambda bi, i, j: (bi, i, j)),
        out_shape=jax.ShapeDtypeStruct((b, s, d), F32),
        compiler_params=_cparams(("parallel", "parallel", "parallel"), est),
    )(*a_list, *([w_all] * n_in), x, mod)


def _rot_lanes(blk, cos_t, sin_up, sin_dn, half):
    return (blk * cos_t + pltpu.roll(blk, half, 1) * sin_up + pltpu.roll(blk, LANE - half, 1) * sin_dn)


def _mla_proj_kernel(h_ref, wlat_ref, gq_ref, gkv_ref, wq_ref, wk_ref, wv_ref, cos_ref, sup_ref, sdn_ref,
                     q_ref, k_ref, v_ref, *, q_scale):
    lat = lax.dot_general(h_ref[0], wlat_ref[...].astype(BF16), _NT_DIMS, preferred_element_type=F32)
    qn = _rms(lat[:, :Q_LORA], gq_ref[...]).astype(BF16)
    kvn = _rms(lat[:, Q_LORA:Q_LORA + KV_LORA], gkv_ref[...]).astype(BF16)
    cos_t, sin_up, sin_dn = cos_ref[...], sup_ref[...], sdn_ref[...]
    half = MLA_ROPE // 2
    q = jnp.dot(qn, wq_ref[...], preferred_element_type=F32) * q_scale
    kn = jnp.dot(kvn, wk_ref[...], preferred_element_type=F32)
    v = jnp.dot(kvn, wv_ref[...], preferred_element_type=F32).astype(BF16)
    ones = jnp.ones((v.shape[0], MLA_V), BF16)
    kr = _rot_lanes(lat[:, Q_LORA + KV_LORA:], cos_t, sin_up, sin_dn, half).astype(BF16)
    nope_w = MLA_HEADS * MLA_NOPE
    for h in range(MLA_HEADS):
        c0 = h * MLA_QK
        v_ref[0, :, 2 * h * MLA_V:(2 * h + 1) * MLA_V] = v[:, h * MLA_V:(h + 1) * MLA_V]
        v_ref[0, :, (2 * h + 1) * MLA_V:(2 * h + 2) * MLA_V] = ones
        q_ref[0, :, c0:c0 + LANE] = q[:, h * LANE:(h + 1) * LANE].astype(BF16)
        qr = q[:, nope_w + h * LANE:nope_w + (h + 1) * LANE]
        q_ref[0, :, c0 + LANE:c0 + 2 * LANE] = _rot_lanes(qr, cos_t, sin_up, sin_dn, half).astype(BF16)
        k_ref[0, :, c0:c0 + LANE] = kn[:, h * LANE:(h + 1) * LANE].astype(BF16)
        k_ref[0, :, c0 + LANE:c0 + 2 * LANE] = kr


def _mla_proj(h, w_in_t, gq, gkv, wq_all, wk_all, wv_all, layer, tabs, q_scale, tm=512):
    b, s, d = h.shape
    hq = MLA_HEADS * MLA_QK
    hv = MLA_HEADS * 2 * MLA_V
    est = (2 * _nbytes((tm, d), BF16) + _nbytes((LAT_W, d), F32) + _nbytes((LAT_W, d), BF16)
           + 2 * _nbytes((tm, LAT_W), F32) + 2 * _nbytes(wq_all.shape[1:], BF16) + 2 * _nbytes(wk_all.shape[1:], BF16)
           + 2 * _nbytes(wv_all.shape[1:], BF16) + 4 * _nbytes((tm, hq), BF16) + 2 * _nbytes((tm, hv), BF16)
           + 3 * _nbytes((tm, hq), F32))
    full = lambda shape: pl.BlockSpec(shape, lambda bi, i: (0,) * len(shape))
    stacked = lambda w: pl.BlockSpec((None,) + w.shape[1:], lambda bi, i: (layer, 0, 0))
    tab = pl.BlockSpec((tm, LANE), lambda bi, i: (i, 0))
    return pl.pallas_call(
        functools.partial(_mla_proj_kernel, q_scale=q_scale),
        name="mla_proj",
        grid=(b, s // tm),
        in_specs=[
            pl.BlockSpec((1, tm, d), lambda bi, i: (bi, i, 0)),
            pl.BlockSpec((None, LAT_W, d), lambda bi, i: (layer, 0, 0), pipeline_mode=pl.Buffered(1)),
            full((1, Q_LORA)), full((1, KV_LORA)), stacked(wq_all), stacked(wk_all), stacked(wv_all),
            tab, tab, tab,
        ],
        out_specs=[
            pl.BlockSpec((1, tm, hq), lambda bi, i: (bi, i, 0)),
            pl.BlockSpec((1, tm, hq), lambda bi, i: (bi, i, 0)),
            pl.BlockSpec((1, tm, hv), lambda bi, i: (bi, i, 0)),
        ],
        out_shape=[
            jax.ShapeDtypeStruct((b, s, hq), BF16),
            jax.ShapeDtypeStruct((b, s, hq), BF16),
            jax.ShapeDtypeStruct((b, s, hv), BF16),
        ],
        compiler_params=_cparams(("parallel", "parallel"), est),
    )(h, w_in_t, gq.reshape(1, -1), gkv.reshape(1, -1), wq_all, wk_all, wv_all, *tabs)


def _mla_attn_kernel(q_ref, k_ref, v_ref, o_ref, *, chain_rows):
    n_chains = q_ref.shape[1] // chain_rows

    def scores(c):
        rs = slice(c * chain_rows, (c + 1) * chain_rows)
        return lax.dot_general(q_ref[0, rs, :], k_ref[0], (((1,), (1,)), ((), ())), preferred_element_type=F32)

    s_next = scores(0)
    for c in range(n_chains):
        s = s_next
        if c + 1 < n_chains:
            s_next = scores(c + 1)
        m = jnp.max(s, axis=-1, keepdims=True)
        p = jnp.exp2(s - m).astype(BF16)
        oe = jnp.dot(p, v_ref[0], preferred_element_type=F32)
        o_ref[0, c * chain_rows:(c + 1) * chain_rows, :] = (oe[:, :MLA_V] / oe[:, MLA_V:]).astype(BF16)


def _mla_attn(q, k, v, tq=2048, chain_rows=1024):
    b, s, _ = q.shape
    est = (2 * _nbytes((tq, MLA_QK), BF16) + 2 * _nbytes((s, MLA_QK), BF16) + 2 * _nbytes((s, 2 * MLA_V), BF16)
           + 2 * _nbytes((tq, MLA_V), BF16) + 5 * _nbytes((chain_rows, s), F32))
    return pl.pallas_call(
        functools.partial(_mla_attn_kernel, chain_rows=chain_rows),
        name="mla_attn",
        grid=(b, MLA_HEADS, s // tq),
        in_specs=[
            pl.BlockSpec((1, tq, MLA_QK), lambda bi, h, i: (bi, i, h)),
            pl.BlockSpec((1, s, MLA_QK), lambda bi, h, i: (bi, 0, h)),
            pl.BlockSpec((1, s, 2 * MLA_V), lambda bi, h, i: (bi, 0, h)),
        ],
        out_specs=pl.BlockSpec((1, tq, MLA_V), lambda bi, h, i: (bi, i, h)),
        out_shape=jax.ShapeDtypeStruct((b, s, MLA_HEADS * MLA_V), BF16),
        compiler_params=_cparams(("parallel", "parallel", "parallel"), est),
    )(q, k, v)


def _gqa_kernel(q_ref, k_ref, v_ref, sink_ref, cos_ref, sin_ref, swap_ref, o_ref, k_scr, v_scr, *, scale):
    s_len = k_scr.shape[0]
    n_blocks = s_len // BAND
    rows = GQA_GROUP * BAND
    swap = swap_ref[...]

    def rot(x, cos_t, sin_t):
        return x.astype(F32) * cos_t + jnp.dot(x, swap, preferred_element_type=F32) * sin_t

    k_scr[...] = rot(k_ref[0], cos_ref[...], sin_ref[...]).astype(BF16)
    v_scr[:, :LANE] = v_ref[0]
    v_scr[:, LANE:] = jnp.ones((s_len, LANE), BF16)
    sink = sink_ref[0]
    rel = (lax.broadcasted_iota(jnp.int32, (rows, 3 * BAND), 1) - BAND
           - (lax.broadcasted_iota(jnp.int32, (rows, 3 * BAND), 0) & (BAND - 1)))
    bias = jnp.where(jnp.abs(rel) <= WINDOW, 0.0, NEG_INF)

    def window(n):
        r0 = n * BAND
        return max(0, r0 - BAND), min(s_len, r0 + 2 * BAND)

    def scores(n):
        r0 = n * BAND
        cos_t = jnp.concatenate([cos_ref[r0:r0 + BAND, :]] * GQA_GROUP, axis=0)
        sin_t = jnp.concatenate([sin_ref[r0:r0 + BAND, :]] * GQA_GROUP, axis=0)
        q_st = jnp.concatenate([q_ref[0, r0:r0 + BAND, g * LANE:(g + 1) * LANE] for g in range(GQA_GROUP)], axis=0)
        q_st = (rot(q_st, cos_t, sin_t) * scale).astype(BF16)
        lo, hi = window(n)
        c0 = lo - (r0 - BAND)
        s = lax.dot_general(q_st, k_scr[lo:hi, :], (((1,), (1,)), ((), ())), preferred_element_type=F32)
        return s + bias[:, c0:c0 + hi - lo]

    s_next = scores(0)
    for n in range(n_blocks):
        s = s_next
        if n + 1 < n_blocks:
            s_next = scores(n + 1)
        r0 = n * BAND
        lo, hi = window(n)
        m = jnp.maximum(jnp.max(s, axis=-1, keepdims=True), sink)
        p = jnp.exp(s - m).astype(BF16)
        oe = jnp.dot(p, v_scr[lo:hi, :], preferred_element_type=F32)
        o = oe[:, :LANE] / (oe[:, LANE:] + jnp.exp(sink - m))
        for g in range(GQA_GROUP):
            o_ref[0, r0:r0 + BAND, g * LANE:(g + 1) * LANE] = o[g * BAND:(g + 1) * BAND, :].astype(BF16)


def _gqa_attn(qkv, sinks, tabs):
    b, s, _ = qkv.shape
    gw = GQA_GROUP * GQA_HEAD_DIM
    sink_col = jnp.broadcast_to(sinks.astype(F32).reshape(GQA_KV_HEADS, GQA_GROUP, 1, 1),
                                (GQA_KV_HEADS, GQA_GROUP, BAND, 1)).reshape(GQA_KV_HEADS, GQA_GROUP * BAND, 1)
    est = (4 * _nbytes((s, gw), BF16) + 7 * _nbytes((s, LANE), BF16) + 6 * _nbytes((s, LANE), F32)
           + 8 * _nbytes((GQA_GROUP * BAND, 3 * BAND), F32))
    tab = pl.BlockSpec((s, LANE), lambda bi, h: (0, 0))
    cos_t, sin_t = tabs
    half = GQA_ROT // 2
    lane = jnp.arange(LANE)
    swap = (((lane[None, :] < half) & (lane[:, None] == lane[None, :] + half))
            | ((lane[None, :] >= half) & (lane[None, :] < 2 * half) & (lane[:, None] == lane[None, :] - half)))
    return pl.pallas_call(
        functools.partial(_gqa_kernel, scale=GQA_HEAD_DIM ** -0.5),
        name="gqa_window_attn",
        grid=(b, GQA_KV_HEADS),
        in_specs=[
            pl.BlockSpec((1, s, gw), lambda bi, h: (bi, 0, h)),
            pl.BlockSpec((1, s, LANE), lambda bi, h: (bi, 0, GQA_HEADS + h)),
            pl.BlockSpec((1, s, LANE), lambda bi, h: (bi, 0, GQA_HEADS + GQA_KV_HEADS + h)),
            pl.BlockSpec((1, GQA_GROUP * BAND, 1), lambda bi, h: (h, 0, 0)),
            tab, tab,
            pl.BlockSpec((LANE, LANE), lambda bi, h: (0, 0)),
        ],
        out_specs=pl.BlockSpec((1, s, gw), lambda bi, h: (bi, 0, h)),
        out_shape=jax.ShapeDtypeStruct((b, s, GQA_HEADS * GQA_HEAD_DIM), BF16),
        scratch_shapes=[pltpu.VMEM((s, LANE), BF16), pltpu.VMEM((s, 2 * LANE), BF16)],
        compiler_params=_cparams(("parallel", "parallel"), est),
    )(qkv, qkv, qkv, sink_col, cos_t, sin_t, swap.astype(BF16))


def _proj_short_conv_kernel(h_ref, w_ref, cw_ref, cb_ref, o_ref, *, sub_cols):
    tn = o_ref.shape[-1]
    h = h_ref[0]

    def project(c0):
        return lax.dot_general(h, w_ref[0, c0:c0 + sub_cols, :].astype(BF16), _NT_DIMS, preferred_element_type=F32)

    x_next = project(0)
    for c0 in range(0, tn, sub_cols):
        x = x_next
        if c0 + sub_cols < tn:
            x_next = project(c0 + sub_cols)
        cols = slice(c0, c0 + sub_cols)
        s_len = x.shape[0]
        row = lax.broadcasted_iota(jnp.int32, x.shape, 0)
        prev = jnp.where(row == 0, 0.0, pltpu.roll(x, 1, 0))
        nxt = jnp.where(row == s_len - 1, 0.0, pltpu.roll(x, s_len - 1, 0))
        y = cb_ref[:, cols] + prev * cw_ref[0:1, cols]
        y = y + x * cw_ref[1:2, cols]
        y = y + nxt * cw_ref[2:3, cols]
        o_ref[0, :, cols] = y.astype(BF16)


def _proj_short_conv(h, wt_all, layer, row0, n, cw, cb, tn=1024, sub_cols=512):
    b, s, d = h.shape
    assert row0 % 8 == 0 and n % tn == 0
    est = (2 * _nbytes((s, d), BF16) + 2 * _nbytes((d, tn), F32) + 2 * _nbytes((d, sub_cols), BF16)
           + 2 * _nbytes((s, tn), BF16) + 8 * _nbytes((s, sub_cols), F32))
    return pl.pallas_call(
        functools.partial(_proj_short_conv_kernel, sub_cols=sub_cols),
        name="hyena_proj_short_conv",
        grid=(b, n // tn),
        in_specs=[
            pl.BlockSpec((1, s, d), lambda bi, j: (bi, 0, 0)),
            pl.BlockSpec((pl.Element(1), pl.Element(tn), pl.Element(d)),
                         lambda bi, j: (layer, pl.multiple_of(row0 + j * tn, 8), 0)),
            pl.BlockSpec((HY_SHORT, tn), lambda bi, j: (0, j)),
            pl.BlockSpec((1, tn), lambda bi, j: (0, j)),
        ],
        out_specs=pl.BlockSpec((1, s, tn), lambda bi, j: (bi, 0, j)),
        out_shape=jax.ShapeDtypeStruct((b, s, n), BF16),
        compiler_params=_cparams(("parallel", "parallel"), est),
    )(h, wt_all, cw, cb.reshape(1, n))


def _filter_mlp_kernel(z_ref, w1_ref, b1_ref, w2_ref, b2_ref, w3_ref, b3_ref, fr_ref, h_ref):
    dot = functools.partial(jnp.dot, preferred_element_type=F32, precision=HIGHEST)
    fr = fr_ref[...]
    h = jnp.sin(fr * (dot(z_ref[...], w1_ref[...]) + b1_ref[...]))
    h = jnp.sin(fr * (dot(h, w2_ref[...]) + b2_ref[...]))
    h_ref[...] = jnp.sin(fr * (dot(h, w3_ref[...]) + b3_ref[...]))


def _filter_mlp(z, w1, b1, w2, b2, w3, b3, freq):
    n_lag = z.shape[0]
    full = lambda shape: pl.BlockSpec(shape, lambda i: (0,) * len(shape))
    est = 12 * _nbytes((n_lag, LANE), F32)
    return pl.pallas_call(
        _filter_mlp_kernel,
        name="hyena_filter_mlp",
        grid=(1,),
        in_specs=[full(z.shape), full(w1.shape), full((1, HY_FFN)), full(w2.shape), full((1, HY_FFN)),
                  full(w3.shape), full((1, HY_FFN)), full((1, HY_FFN))],
        out_specs=full((n_lag, HY_FFN)),
        out_shape=jax.ShapeDtypeStruct((n_lag, HY_FFN), F32),
        compiler_params=_cparams(("arbitrary",), est),
    )(z, w1, b1.reshape(1, -1), w2, b2.reshape(1, -1), w3, b3.reshape(1, -1), freq.reshape(1, -1))


def _filter_gen_kernel(h_ref, w4f_ref, w4b_ref, ksum_ref, kdiff_ref, nyq_ref):
    h = h_ref[...].astype(BF16)
    hf = jnp.dot(h, w4f_ref[...].astype(BF16), preferred_element_type=F32)
    hb = jnp.dot(h, w4b_ref[...].astype(BF16), preferred_element_type=F32)
    n_lag, tn = hf.shape
    row = lax.broadcasted_iota(jnp.int32, (n_lag, tn), 0)
    col = pl.program_id(1) * tn + lax.broadcasted_iota(jnp.int32, (1, tn), 1)
    t = row.astype(F32) / (n_lag - 1)
    max_decay = math.log(HY_TARGET) / HY_DECAY_PCT_SHORT
    min_decay = math.log(HY_TARGET) / HY_DECAY_PCT_LONG
    delta = min_decay + (max_decay - min_decay) * (col.astype(F32) / (HY_D - 1))
    decay = jnp.exp(-t * jnp.abs(delta))
    kf = hf * decay
    kb = jnp.where(row == 0, 0.0, hb * decay)
    inv = 1.0 / (jnp.sum(jnp.abs(kf), axis=0, keepdims=True) + jnp.sum(jnp.abs(kb), axis=0, keepdims=True))
    ksum = (kf + kb) * inv
    ksum_ref[0] = ksum.astype(BF16)
    kdiff_ref[0] = ((kb - kf) * inv).astype(BF16)
    alt = (1 - 2 * (row & 1)).astype(F32)
    nyq_ref[0] = jnp.sum(ksum * alt, axis=0, keepdims=True) * (1.0 / DFT_N)


def _filter_gen(h, w4, tn=256):
    n_lag = h.shape[0]
    nblk = HY_D // tn
    est = 10 * _nbytes((n_lag, tn), F32) + 4 * _nbytes((n_lag, LANE), F32) + 4 * _nbytes((n_lag, tn), BF16)
    out = jax.ShapeDtypeStruct((HY_ORDER, n_lag, HY_D), BF16)
    return pl.pallas_call(
        _filter_gen_kernel,
        name="hyena_filter_gen",
        grid=(HY_ORDER, nblk),
        in_specs=[
            pl.BlockSpec((n_lag, HY_FFN), lambda o, j: (0, 0)),
            pl.BlockSpec((HY_FFN, tn), lambda o, j: (0, 2 * o * nblk + j)),
            pl.BlockSpec((HY_FFN, tn), lambda o, j: (0, (2 * o + 1) * nblk + j)),
        ],
        out_specs=[pl.BlockSpec((1, n_lag, tn), lambda o, j: (o, 0, j))] * 2
        + [pl.BlockSpec((1, 1, tn), lambda o, j: (o, 0, j))],
        out_shape=[out, out, jax.ShapeDtypeStruct((HY_ORDER, 1, HY_D), F32)],
        compiler_params=_cparams(("parallel", "parallel"), est),
    )(h, w4, w4)


def _filter_dft_kernel(ct_ref, st_ref, ksum_ref, kdiff_ref, a_ref, bq_ref):
    re = jnp.dot(ct_ref[...], ksum_ref[0], preferred_element_type=F32)
    im = jnp.dot(st_ref[...], kdiff_ref[0], preferred_element_type=F32)
    tf = re.shape[0]
    row = pl.program_id(2) * tf + lax.broadcasted_iota(jnp.int32, re.shape, 0)
    is0 = row == 0
    wgt = jnp.where(is0, 1.0 / DFT_N, 2.0 / DFT_N)
    a_ref[0] = (re * wgt).astype(BF16)
    bq_ref[0] = jnp.where(is0, 0.0, im * wgt).astype(BF16)


def _filter_dft(ct, st, ksum, kdiff, tf=512, tn=512):
    n_f, n_s = ct.shape
    est = (4 * _nbytes((tf, n_s), BF16) + 4 * _nbytes((n_s, tn), BF16) + 4 * _nbytes((tf, tn), BF16)
           + 8 * _nbytes((tf, tn), F32))
    out = jax.ShapeDtypeStruct((HY_ORDER, n_f, HY_D), BF16)
    return pl.pallas_call(
        _filter_dft_kernel,
        name="hyena_filter_dft",
        grid=(HY_ORDER, HY_D // tn, n_f // tf),
        in_specs=[
            pl.BlockSpec((tf, n_s), lambda o, j, k: (k, 0)),
            pl.BlockSpec((tf, n_s), lambda o, j, k: (k, 0)),
            pl.BlockSpec((1, n_s, tn), lambda o, j, k: (o, 0, j)),
            pl.BlockSpec((1, n_s, tn), lambda o, j, k: (o, 0, j)),
        ],
        out_specs=[pl.BlockSpec((1, tf, tn), lambda o, j, k: (o, k, j))] * 2,
        out_shape=[out, out],
        compiler_params=_cparams(("parallel", "parallel", "parallel"), est),
    )(ct, st, ksum, kdiff)


def _alt_sign(idx):
    return (1 - 2 * (idx & 1)).astype(F32)


def _long_conv_kernel(ce_ref, se_ref, co_ref, so_ref, cot_ref, sot_ref, flip_ref, u_ref, g_ref,
                      ae_ref, bqe_ref, ao_ref, bqo_ref, knyq_ref, skip_ref, o_ref,
                      e_scr, o_scr, yre_scr, yse_scr, yro_scr, yso_scr, d_scr):
    hh = e_scr.shape[0]
    sub = CONV_SUB_ROWS
    dot = functools.partial(jnp.dot, preferred_element_type=F32)
    alt_h = _alt_sign(lax.broadcasted_iota(jnp.int32, (hh, 1), 0))

    u_lo = u_ref[0, :hh, :].astype(F32)
    u_rev = dot(flip_ref[...], u_ref[0, hh:, :])
    e32 = u_lo + u_rev
    e_scr[...] = e32.astype(BF16)
    o_scr[...] = (u_lo - u_rev).astype(BF16)
    mid = u_ref[0, hh:hh + 16, :].astype(F32)[0:1]
    nyq_u = jnp.sum(e32 * alt_h, axis=0, keepdims=True) + mid
    e, o = e_scr[...], o_scr[...]

    for r0 in range(0, hh, sub):
        rs = slice(r0, r0 + sub)
        m = r0 + lax.broadcasted_iota(jnp.int32, (sub, 1), 0)
        corr = _alt_sign(m) * mid
        uc = dot(ce_ref[rs, :], e) + corr
        us = dot(se_ref[rs, :], o)
        a, bq = ae_ref[0, rs, :].astype(F32), bqe_ref[0, rs, :].astype(F32)
        yre_scr[rs, :] = (uc * a + us * bq).astype(BF16)
        ys = us * a - uc * bq
        if r0 == 0:
            ys = jnp.where(m == 0, nyq_u * knyq_ref[0], ys)
        yse_scr[rs, :] = ys.astype(BF16)
        uc = dot(co_ref[rs, :], o)
        us = dot(so_ref[rs, :], e) + corr
        a, bq = ao_ref[0, rs, :].astype(F32), bqo_ref[0, rs, :].astype(F32)
        yro_scr[rs, :] = (uc * a + us * bq).astype(BF16)
        yso_scr[rs, :] = (us * a - uc * bq).astype(BF16)

    yre, yse, yro, yso = yre_scr[...], yse_scr[...], yro_scr[...], yso_scr[...]
    nyq = yse_scr[0:16, :].astype(F32)[0:1]
    y_mid = jnp.sum((yre.astype(F32) + yso.astype(F32)) * alt_h, axis=0, keepdims=True) + nyq
    skip = skip_ref[0]
    for r0 in range(0, hh, sub):
        rs = slice(r0, r0 + sub)
        t = r0 + lax.broadcasted_iota(jnp.int32, (sub, 1), 0)
        pa = dot(ce_ref[rs, :], yre) + dot(sot_ref[rs, :], yso) + _alt_sign(t) * nyq
        pb = dot(se_ref[rs, :], yse) + dot(cot_ref[rs, :], yro)
        u_t = u_ref[0, rs, :].astype(F32)
        o_ref[0, rs, :] = (g_ref[0, rs, :].astype(F32) * (pa + pb + u_t * skip)).astype(BF16)
        d_scr[rs, :] = (pa - pb).astype(BF16)
    y_hi = dot(flip_ref[...], d_scr[...])
    y_hi = jnp.where(lax.broadcasted_iota(jnp.int32, (hh, 1), 0) == 0, y_mid, y_hi)
    u_hi = u_ref[0, hh:, :].astype(F32)
    o_ref[0, hh:, :] = (g_ref[0, hh:, :].astype(F32) * (y_hi + u_hi * skip)).astype(BF16)


def _long_conv(tabs, spec, order, usrc, u_blk0, gsrc, g_blk0, skip, tn=512):
    a, bq, knyq = spec
    b, s, _ = usrc.shape
    hh = s // 2
    nblk = HY_D // tn
    est = (7 * _nbytes((hh, hh), BF16) + 10 * _nbytes((s, tn), BF16) + 8 * _nbytes((hh, tn), BF16)
           + 7 * _nbytes((hh, tn), BF16) + 8 * _nbytes((hh, tn), F32) + 12 * _nbytes((CONV_SUB_ROWS, tn), F32))
    tab = pl.BlockSpec((hh, hh), lambda bi, j: (0, 0), pipeline_mode=pl.Buffered(1))
    even = pl.BlockSpec((1, hh, tn), lambda bi, j: (order, 0, j))
    odd = pl.BlockSpec((1, hh, tn), lambda bi, j: (order, 1, j))
    row = pl.BlockSpec((1, 1, tn), lambda bi, j: (order, 0, j))
    return pl.pallas_call(
        _long_conv_kernel,
        name="hyena_long_conv",
        grid=(b, nblk),
        in_specs=[
            tab, tab, tab, tab, tab, tab, tab,
            pl.BlockSpec((1, s, tn), lambda bi, j: (bi, 0, u_blk0 * nblk + j)),
            pl.BlockSpec((1, s, tn), lambda bi, j: (bi, 0, g_blk0 * nblk + j)),
            even, even, odd, odd, row, row,
        ],
        out_specs=pl.BlockSpec((1, s, tn), lambda bi, j: (bi, 0, j)),
        out_shape=jax.ShapeDtypeStruct((b, s, HY_D), BF16),
        scratch_shapes=[pltpu.VMEM((hh, tn), BF16)] * 7,
        compiler_params=_cparams(("parallel", "parallel"), est),
    )(tabs["ce"], tabs["se"], tabs["co"], tabs["so"], tabs["cot"], tabs["sot"], tabs["flip"], usrc, gsrc,
      a, bq, a, bq, knyq, skip.reshape(HY_ORDER, 1, HY_D))


def _rope_tables(seq, rot_dim):
    half = rot_dim // 2
    pos = jnp.arange(seq, dtype=F32)
    inv = ROPE_THETA ** (-jnp.arange(0, rot_dim, 2, dtype=F32) / rot_dim)
    ang = pos[:, None] * inv[None, :]
    cos, sin = jnp.cos(ang), jnp.sin(ang)
    rest = LANE - rot_dim
    return cos, sin, half, rest


def _rope_lane_tables(seq, rot_dim, rest_passthrough):
    cos, sin, half, rest = _rope_tables(seq, rot_dim)
    fill = jnp.ones((seq, rest), F32) if rest_passthrough else jnp.zeros((seq, rest), F32)
    zero_h = jnp.zeros((seq, half), F32)
    zero_r = jnp.zeros((seq, rest), F32)
    cos_t = jnp.concatenate([cos, cos, fill], axis=1)
    sin_up = jnp.concatenate([zero_h, sin, zero_r], axis=1)
    sin_dn = jnp.concatenate([-sin, zero_h, zero_r], axis=1)
    return cos_t, sin_up, sin_dn


def _dft_tables(n_half):
    hh = n_half // 2
    blk = 32
    idx = jnp.arange(n_half, dtype=jnp.int32)
    unit = 2.0 * math.pi / n_half
    ang_a = ((blk * idx[:hh // blk, None] * idx[None, :]) & (n_half - 1)).astype(F32) * unit
    ang_b = ((idx[:blk, None] * idx[None, :]) & (n_half - 1)).astype(F32) * unit
    ca, sa = jnp.cos(ang_a)[:, None, :], jnp.sin(ang_a)[:, None, :]
    cb, sb = jnp.cos(ang_b)[None], jnp.sin(ang_b)[None]
    ce_f = (ca * cb - sa * sb).reshape(hh, n_half)
    se_f = (sa * cb + ca * sb).reshape(hh, n_half)
    turn = idx.astype(F32) * (0.5 * unit)
    c_row, s_row = jnp.cos(turn)[None, :], jnp.sin(turn)[None, :]
    co_f = ce_f * c_row - se_f * s_row
    so_f = se_f * c_row + ce_f * s_row
    alt = (1 - 2 * (idx & 1)).astype(F32)
    ce, se = ce_f[:, :hh], se_f[:, :hh]
    c_col, s_col = jnp.cos(turn[:hh])[:, None], jnp.sin(turn[:hh])[:, None]
    flip = (idx[:hh, None] >= 1) & (idx[None, :hh] == hh - idx[:hh, None])
    tabs = dict(
        ct_eo=jnp.concatenate([ce_f, co_f], axis=0),
        st_eo=jnp.concatenate([jnp.where(idx[:hh, None] == 0, alt[None, :], se_f), so_f], axis=0),
        ce=ce, se=se, co=co_f[:, :hh], so=so_f[:, :hh],
        cot=ce * c_col - se * s_col, sot=se * c_col + ce * s_col, flip=flip)
    return {name: t.astype(BF16) for name, t in tabs.items()}


def _filter_features(n_lag):
    t = jnp.linspace(0.0, 1.0, n_lag, dtype=F32)[:, None]
    bands = (HY_EMB - 1) // 2
    wpos = 2.0 * math.pi * jnp.arange(n_lag, dtype=F32) / n_lag
    fb = jnp.linspace(1e-4, bands - 1, bands, dtype=F32)
    fw = wpos[:, None] * fb[None, :]
    z = jnp.concatenate([t, jnp.cos(fw), -jnp.sin(fw)], axis=-1)
    return jnp.pad(z, ((0, 0), (0, LANE - HY_EMB)))


def kernel(x, c, ada_mix_w, ada_mix_b, norm_mix_g, ada_mlp_w, ada_mlp_b, norm_mlp_g, w_mlp_in, w_mlp_out, e_w_in, e_q_norm_g, e_kv_norm_g, e_w_uq, e_w_ukv, e_conv_w, e_conv_b, e_f_w1, e_f_b1, e_f_w2, e_f_b2, e_f_w3, e_f_b3, e_f_freq, e_f_w4, e_hy_skip, e_w_out, o_w_qkv, o_sinks, o_w_o, final_norm_g):
    b, s, d = x.shape
    c8 = jnp.pad(c, ((0, 8 - b), (0, 0)))
    mod_mix = _ada(c8, ada_mix_w, ada_mix_b)[:, :b].reshape(DEPTH, b, 1, 3 * d)
    mod_mlp = _ada(c8, ada_mlp_w, ada_mlp_b)[:, :b].reshape(DEPTH, b, 1, 3 * d)

    mla_tabs = _rope_lane_tables(s, MLA_ROPE, rest_passthrough=False)
    gqa_cos, gqa_sin_up, gqa_sin_dn = _rope_lane_tables(s, GQA_ROT, rest_passthrough=True)
    gqa_tabs = (gqa_cos, gqa_sin_up + gqa_sin_dn)
    dft = _dft_tables(s)
    z_feat = _filter_features(s)
    q_scale = (MLA_NOPE + MLA_ROPE) ** -0.5 * math.log2(math.e)
    w_in_t = jnp.swapaxes(e_w_in, 1, 2)
    hy_row0 = Q_LORA + KV_LORA + MLA_ROPE

    w_uq = e_w_uq.reshape(-1, Q_LORA, MLA_HEADS, MLA_NOPE + MLA_ROPE)
    wq_all = jnp.concatenate([
        w_uq[..., :MLA_NOPE].reshape(-1, Q_LORA, MLA_HEADS * MLA_NOPE),
        jnp.pad(w_uq[..., MLA_NOPE:], ((0, 0), (0, 0), (0, 0), (0, LANE - MLA_ROPE))).reshape(
            -1, Q_LORA, MLA_HEADS * LANE),
    ], axis=2).astype(BF16)
    w_ukv = e_w_ukv.reshape(-1, KV_LORA, MLA_HEADS, MLA_NOPE + MLA_V)
    wk_all = w_ukv[..., :MLA_NOPE].reshape(-1, KV_LORA, MLA_HEADS * MLA_NOPE).astype(BF16)
    wv_all = w_ukv[..., MLA_NOPE:].reshape(-1, KV_LORA, MLA_HEADS * MLA_V).astype(BF16)

    for l in range(DEPTH):
        i = l // 2
        if l % 2 == 0:
            h = _normmod(x, mod_mix[l], norm_mix_g[l])
            u = _proj_short_conv(h, w_in_t, i, hy_row0, 3 * HY_D, e_conv_w[i], e_conv_b[i])

            q, k, v = _mla_proj(h, w_in_t, e_q_norm_g[i], e_kv_norm_g[i], wq_all, wk_all, wv_all, i, mla_tabs,
                                q_scale)
            a_mla = _mla_attn(q, k, v)

            w1 = jnp.pad(e_f_w1[i], ((0, LANE - HY_EMB), (0, 0)))
            h_filt = _filter_mlp(z_feat, w1, e_f_b1[i], e_f_w2[i], e_f_b2[i], e_f_w3[i], e_f_b3[i], e_f_freq[i])
            ksum, kdiff, knyq = _filter_gen(h_filt, e_f_w4[i])
            spec = (*_filter_dft(dft["ct_eo"], dft["st_eo"], ksum, kdiff), knyq)
            zc = _long_conv(dft, spec, 0, u, 0, u, 1, e_hy_skip[i])
            b_hy = _long_conv(dft, spec, 1, zc, 0, u, 2, e_hy_skip[i])

            x = _proj_res([a_mla, b_hy], e_w_out, i, x, mod_mix[l])
        else:
            h = _normmod(x, mod_mix[l], norm_mix_g[l])
            qkv = _matmul(h, o_w_qkv, i, BF16)
            o = _gqa_attn(qkv, o_sinks[i], gqa_tabs)
            x = _proj_res([o], o_w_o, i, x, mod_mix[l])
        x = _mlp(x, mod_mlp[l], norm_mlp_g[l], w_mlp_in, w_mlp_out, l, final_norm_g, l == DEPTH - 1)
    return x
```

```python
import functools
import math

import jax
import jax.numpy as jnp
from jax import lax
from jax.experimental import pallas as pl
from jax.experimental.pallas import tpu as pltpu

F32 = jnp.float32
BF16 = jnp.bfloat16
HIGHEST = lax.Precision.HIGHEST

D_MODEL = 2048
BATCH = 4
SEQ = 2048
DEPTH = 4
RMS_EPS = 1e-6
ROPE_THETA = 500000.0
NEG_INF = -1e30
MLA_HEADS = 8
MLA_NOPE = 128
MLA_ROPE = 64
MLA_V = 128
Q_LORA = 512
KV_LORA = 256
HY_D = 1024
HY_ORDER = 2
HY_SHORT = 3
HY_EMB = 33
HY_FFN = 64
HY_DECAY_PCT_SHORT = 0.3
HY_DECAY_PCT_LONG = 1.5
HY_TARGET = 1e-2
GQA_HEADS = 16
GQA_KV_HEADS = 4
GQA_HEAD_DIM = 128
GQA_ROT = GQA_HEAD_DIM // 4
GQA_GROUP = GQA_HEADS // GQA_KV_HEADS
WINDOW = 128
BAND = 128
D_FF = 4 * D_MODEL
LAT_W = Q_LORA + KV_LORA + 128
MLA_QK = 256
DFT_N = 2 * SEQ
CONV_SUB_ROWS = 256

LANE = 128
V7X_VMEM_BYTES = 64 * 1024 * 1024
V7X_VMEM_BUDGET = 56 * 1024 * 1024


def _cparams(semantics, est_bytes):
    limit = int(min(V7X_VMEM_BUDGET, max(32 * 1024 * 1024, est_bytes * 3 // 2)))
    return pltpu.CompilerParams(dimension_semantics=semantics, vmem_limit_bytes=limit)


def _nbytes(shape, dtype):
    return math.prod(shape) * jnp.dtype(dtype).itemsize


def _rms(x, g):
    ms = jnp.mean(x * x, axis=-1, keepdims=True)
    return x * lax.rsqrt(ms + RMS_EPS) * g


def _ada_kernel(c_ref, w_ref, b_ref, o_ref):
    cv = c_ref[...]
    s = cv * (1.0 / (1.0 + jnp.exp(-cv)))
    o_ref[0] = jnp.dot(s.astype(BF16), w_ref[0].astype(BF16), preferred_element_type=F32) + b_ref[0]


def _ada(c8, w, b, tn=1536):
    n_l, d, n = w.shape
    est = 2 * _nbytes((d, tn), F32) + _nbytes((d, tn), BF16) + 4 * _nbytes((8, tn), F32) + 2 * _nbytes((8, d), F32)
    return pl.pallas_call(
        _ada_kernel,
        name="ada_modulation",
        grid=(n_l, n // tn),
        in_specs=[
            pl.BlockSpec((8, d), lambda l, j: (0, 0)),
            pl.BlockSpec((1, d, tn), lambda l, j: (l, 0, j)),
            pl.BlockSpec((1, 1, tn), lambda l, j: (l, 0, j)),
        ],
        out_specs=pl.BlockSpec((1, 8, tn), lambda l, j: (l, 0, j)),
        out_shape=jax.ShapeDtypeStruct((n_l, 8, n), F32),
        compiler_params=_cparams(("parallel", "parallel"), est),
    )(c8, w, b.reshape(n_l, 1, n))


def _normmod_rows(x_ref, shift_ref, scale_ref, g_ref, h_ref, rows):
    gain = g_ref[...] * (1.0 + scale_ref[0])
    sh = shift_ref[0]
    tm = h_ref.shape[0]

    def body(r, carry):
        sl = pl.ds(pl.multiple_of(r * rows, rows), rows)
        xv = x_ref[0, sl, :]
        inv = lax.rsqrt(jnp.mean(xv * xv, axis=-1, keepdims=True) + RMS_EPS)
        h_ref[sl, :] = (xv * inv * gain + sh).astype(BF16)
        return carry

    lax.fori_loop(0, tm // rows, body, 0)


def _normmod_kernel(x_ref, shift_ref, scale_ref, g_ref, h_ref):
    _normmod_rows(x_ref, shift_ref, scale_ref, g_ref, h_ref.at[0], 128)


def _normmod(x, mod, g, tm=1024):
    b, s, d = x.shape
    est = 2 * _nbytes((tm, d), F32) + 2 * _nbytes((tm, d), BF16) + 8 * _nbytes((128, d), F32)
    return pl.pallas_call(
        _normmod_kernel,
        name="normmod",
        grid=(b, s // tm),
        in_specs=[
            pl.BlockSpec((1, tm, d), lambda bi, i: (bi, i, 0)),
            pl.BlockSpec((1, 1, d), lambda bi, i: (bi, 0, 0)),
            pl.BlockSpec((1, 1, d), lambda bi, i: (bi, 0, 1)),
            pl.BlockSpec((1, d), lambda bi, i: (0, 0)),
        ],
        out_specs=pl.BlockSpec((1, tm, d), lambda bi, i: (bi, i, 0)),
        out_shape=jax.ShapeDtypeStruct((b, s, d), BF16),
        compiler_params=_cparams(("parallel", "parallel"), est),
    )(x, mod, mod, g.reshape(1, d))


_NT_DIMS = (((1,), (1,)), ((), ()))


def _matmul_kernel(h_ref, w_ref, o_ref):
    o_ref[0] = jnp.dot(h_ref[0], w_ref[...].astype(BF16), preferred_element_type=F32).astype(o_ref.dtype)


def _matmul(h, w_all, layer, out_dtype, tm=2048, tn=1024):
    b, s, d = h.shape
    n = w_all.shape[2]
    est = (2 * _nbytes((tm, d), BF16) + 2 * _nbytes((d, tn), F32) + _nbytes((d, tn), BF16)
           + 3 * _nbytes((tm, tn), F32))
    return pl.pallas_call(
        _matmul_kernel,
        name="matmul",
        grid=(b, s // tm, n // tn),
        in_specs=[
            pl.BlockSpec((1, tm, d), lambda bi, i, j: (bi, i, 0)),
            pl.BlockSpec((None, d, tn), lambda bi, i, j: (layer, 0, j)),
        ],
        out_specs=pl.BlockSpec((1, tm, tn), lambda bi, i, j: (bi, i, j)),
        out_shape=jax.ShapeDtypeStruct((b, s, n), out_dtype),
        compiler_params=_cparams(("parallel", "parallel", "parallel"), est),
    )(h, w_all)


def _mlp_kernel(x_ref, shift_ref, scale_ref, gate_ref, g_ref, w1_ref, w2_ref, fg_ref, o_ref, h_ref, *, final_norm):
    f = pl.program_id(2)

    @pl.when(f == 0)
    def _():
        _normmod_rows(x_ref, shift_ref, scale_ref, g_ref, h_ref, 128)

        o_ref[...] = jnp.zeros_like(o_ref)

    a = jnp.dot(h_ref[...], w1_ref[...].astype(BF16), preferred_element_type=F32)
    a = jnp.square(jnp.maximum(a, 0.0)).astype(BF16)
    o_ref[0] += jnp.dot(a, w2_ref[...].astype(BF16), preferred_element_type=F32)

    @pl.when(f == pl.num_programs(2) - 1)
    def _():
        gate = gate_ref[0]
        fg = fg_ref[...]
        rows = 128

        def body(r, carry):
            sl = pl.ds(pl.multiple_of(r * rows, rows), rows)
            y = x_ref[0, sl, :] + gate * o_ref[0, sl, :]
            if final_norm:
                y = _rms(y, fg)
            o_ref[0, sl, :] = y
            return carry

        lax.fori_loop(0, o_ref.shape[1] // rows, body, 0)


def _mlp(x, mod, g, w1_all, w2_all, layer, final_g, final_norm, tm=1024, tf=512):
    b, s, d = x.shape
    ff = w1_all.shape[2]
    est = (4 * _nbytes((tm, d), F32) + _nbytes((tm, d), BF16) + 4 * _nbytes((d, tf), F32)
           + 2 * _nbytes((d, tf), BF16) + 2 * _nbytes((tm, tf), F32))
    return pl.pallas_call(
        functools.partial(_mlp_kernel, final_norm=final_norm),
        name="mlp_relu2",
        grid=(b, s // tm, ff // tf),
        in_specs=[
            pl.BlockSpec((1, tm, d), lambda bi, i, f: (bi, i, 0)),
            pl.BlockSpec((1, 1, d), lambda bi, i, f: (bi, 0, 0)),
            pl.BlockSpec((1, 1, d), lambda bi, i, f: (bi, 0, 1)),
            pl.BlockSpec((1, 1, d), lambda bi, i, f: (bi, 0, 2)),
            pl.BlockSpec((1, d), lambda bi, i, f: (0, 0)),
            pl.BlockSpec((None, d, tf), lambda bi, i, f: (layer, 0, f)),
            pl.BlockSpec((None, tf, d), lambda bi, i, f: (layer, f, 0)),
            pl.BlockSpec((1, d), lambda bi, i, f: (0, 0)),
        ],
        out_specs=pl.BlockSpec((1, tm, d), lambda bi, i, f: (bi, i, 0)),
        out_shape=jax.ShapeDtypeStruct((b, s, d), F32),
        scratch_shapes=[pltpu.VMEM((tm, d), BF16)],
        compiler_params=_cparams(("parallel", "parallel", "arbitrary"), est),
    )(x, mod, mod, mod, g.reshape(1, d), w1_all, w2_all, final_g.reshape(1, d))


def _proj_res_kernel(*refs, n_in):
    a_refs = refs[:n_in]
    w_ref, x_ref, gate_ref, o_ref, wb_scr = refs[n_in:]
    rows = 256

    @pl.when((pl.program_id(0) == 0) & (pl.program_id(1) == 0))
    def _():
        def body(r, carry):
            sl = pl.ds(pl.multiple_of(r * rows, rows), rows)
            wb_scr[sl, :] = w_ref[sl, :].astype(BF16)
            return carry

        lax.fori_loop(0, w_ref.shape[0] // rows, body, 0)

    kk = a_refs[0].shape[-1]
    acc = None
    for r, a_ref in enumerate(a_refs):
        p = jnp.dot(a_ref[0], wb_scr[r * kk:(r + 1) * kk, :], preferred_element_type=F32)
        acc = p if acc is None else acc + p
    o_ref[0] = x_ref[0] + gate_ref[0] * acc


def _proj_res(a_list, w_all, layer, x, mod, tm=512):
    b, s, d = x.shape
    n_in = len(a_list)
    kk = a_list[0].shape[-1]
    k_all = n_in * kk
    assert all(a.shape[-1] == kk for a in a_list) and w_all.shape[1] == k_all
    est = (_nbytes((k_all, d), F32) + _nbytes((k_all, d), BF16) + 2 * _nbytes((tm, k_all), BF16)
           + 6 * _nbytes((tm, d), F32))
    in_specs = [pl.BlockSpec((1, tm, kk), lambda bi, i: (bi, i, 0)) for _ in a_list]
    in_specs += [
        pl.BlockSpec((None, k_all, d), lambda bi, i: (layer, 0, 0), pipeline_mode=pl.Buffered(1)),
        pl.BlockSpec((1, tm, d), lambda bi, i: (bi, i, 0)),
        pl.BlockSpec((1, 1, d), lambda bi, i: (bi, 0, 2)),
    ]
    return pl.pallas_call(
        functools.partial(_proj_res_kernel, n_in=n_in),
        name="proj_residual",
        grid=(b, s // tm),
        in_specs=in_specs,
        out_specs=pl.BlockSpec((1, tm, d), lambda bi, i: (bi, i, 0)),
        out_shape=jax.ShapeDtypeStruct((b, s, d), F32),
        scratch_shapes=[pltpu.VMEM((k_all, d), BF16)],
        compiler_params=_cparams(("arbitrary", "arbitrary"), est),
    )(*a_list, w_all, x, mod)


def _rot_lanes(blk, cos_t, sin_up, sin_dn, half):
    return (blk * cos_t + pltpu.roll(blk, half, 1) * sin_up + pltpu.roll(blk, LANE - half, 1) * sin_dn)


def _mla_proj_kernel(h_ref, wlat_ref, gq_ref, gkv_ref, wq_ref, wk_ref, wv_ref, cos_ref, sup_ref, sdn_ref,
                     q_ref, k_ref, v_ref, *, q_scale):
    lat = lax.dot_general(h_ref[0], wlat_ref[...].astype(BF16), _NT_DIMS, preferred_element_type=F32)
    qn = _rms(lat[:, :Q_LORA], gq_ref[...]).astype(BF16)
    kvn = _rms(lat[:, Q_LORA:Q_LORA + KV_LORA], gkv_ref[...]).astype(BF16)
    cos_t, sin_up, sin_dn = cos_ref[...], sup_ref[...], sdn_ref[...]
    half = MLA_ROPE // 2
    q = jnp.dot(qn, wq_ref[...], preferred_element_type=F32) * q_scale
    kn = jnp.dot(kvn, wk_ref[...], preferred_element_type=F32)
    v = jnp.dot(kvn, wv_ref[...], preferred_element_type=F32).astype(BF16)
    ones = jnp.ones((v.shape[0], MLA_V), BF16)
    kr = _rot_lanes(lat[:, Q_LORA + KV_LORA:], cos_t, sin_up, sin_dn, half).astype(BF16)
    nope_w = MLA_HEADS * MLA_NOPE
    for h in range(MLA_HEADS):
        c0 = h * MLA_QK
        v_ref[0, :, 2 * h * MLA_V:(2 * h + 1) * MLA_V] = v[:, h * MLA_V:(h + 1) * MLA_V]
        v_ref[0, :, (2 * h + 1) * MLA_V:(2 * h + 2) * MLA_V] = ones
        q_ref[0, :, c0:c0 + LANE] = q[:, h * LANE:(h + 1) * LANE].astype(BF16)
        qr = q[:, nope_w + h * LANE:nope_w + (h + 1) * LANE]
        q_ref[0, :, c0 + LANE:c0 + 2 * LANE] = _rot_lanes(qr, cos_t, sin_up, sin_dn, half).astype(BF16)
        k_ref[0, :, c0:c0 + LANE] = kn[:, h * LANE:(h + 1) * LANE].astype(BF16)
        k_ref[0, :, c0 + LANE:c0 + 2 * LANE] = kr


def _mla_proj(h, w_in_t, gq, gkv, wq_all, wk_all, wv_all, layer, tabs, q_scale, tm=512):
    b, s, d = h.shape
    hq = MLA_HEADS * MLA_QK
    hv = MLA_HEADS * 2 * MLA_V
    est = (2 * _nbytes((tm, d), BF16) + _nbytes((LAT_W, d), F32) + _nbytes((LAT_W, d), BF16)
           + 2 * _nbytes((tm, LAT_W), F32) + 2 * _nbytes(wq_all.shape[1:], BF16) + 2 * _nbytes(wk_all.shape[1:], BF16)
           + 2 * _nbytes(wv_all.shape[1:], BF16) + 4 * _nbytes((tm, hq), BF16) + 2 * _nbytes((tm, hv), BF16)
           + 3 * _nbytes((tm, hq), F32))
    full = lambda shape: pl.BlockSpec(shape, lambda bi, i: (0,) * len(shape))
    stacked = lambda w: pl.BlockSpec((None,) + w.shape[1:], lambda bi, i: (layer, 0, 0))
    tab = pl.BlockSpec((tm, LANE), lambda bi, i: (i, 0))
    return pl.pallas_call(
        functools.partial(_mla_proj_kernel, q_scale=q_scale),
        name="mla_proj",
        grid=(b, s // tm),
        in_specs=[
            pl.BlockSpec((1, tm, d), lambda bi, i: (bi, i, 0)),
            pl.BlockSpec((None, LAT_W, d), lambda bi, i: (layer, 0, 0), pipeline_mode=pl.Buffered(1)),
            full((1, Q_LORA)), full((1, KV_LORA)), stacked(wq_all), stacked(wk_all), stacked(wv_all),
            tab, tab, tab,
        ],
        out_specs=[
            pl.BlockSpec((1, tm, hq), lambda bi, i: (bi, i, 0)),
            pl.BlockSpec((1, tm, hq), lambda bi, i: (bi, i, 0)),
            pl.BlockSpec((1, tm, hv), lambda bi, i: (bi, i, 0)),
        ],
        out_shape=[
            jax.ShapeDtypeStruct((b, s, hq), BF16),
            jax.ShapeDtypeStruct((b, s, hq), BF16),
            jax.ShapeDtypeStruct((b, s, hv), BF16),
        ],
        compiler_params=_cparams(("parallel", "parallel"), est),
    )(h, w_in_t, gq.reshape(1, -1), gkv.reshape(1, -1), wq_all, wk_all, wv_all, *tabs)


def _mla_attn_kernel(q_ref, k_ref, v_ref, o_ref, *, chain_rows):
    n_chains = q_ref.shape[1] // chain_rows

    def scores(c):
        rs = slice(c * chain_rows, (c + 1) * chain_rows)
        return lax.dot_general(q_ref[0, rs, :], k_ref[0], (((1,), (1,)), ((), ())), preferred_element_type=F32)

    s_next = scores(0)
    for c in range(n_chains):
        s = s_next
        if c + 1 < n_chains:
            s_next = scores(c + 1)
        m = jnp.max(s, axis=-1, keepdims=True)
        p = jnp.exp2(s - m).astype(BF16)
        oe = jnp.dot(p, v_ref[0], preferred_element_type=F32)
        o_ref[0, c * chain_rows:(c + 1) * chain_rows, :] = (oe[:, :MLA_V] / oe[:, MLA_V:]).astype(BF16)


def _mla_attn(q, k, v, tq=2048, chain_rows=1024):
    b, s, _ = q.shape
    est = (2 * _nbytes((tq, MLA_QK), BF16) + 2 * _nbytes((s, MLA_QK), BF16) + 2 * _nbytes((s, 2 * MLA_V), BF16)
           + 2 * _nbytes((tq, MLA_V), BF16) + 5 * _nbytes((chain_rows, s), F32))
    return pl.pallas_call(
        functools.partial(_mla_attn_kernel, chain_rows=chain_rows),
        name="mla_attn",
        grid=(b, MLA_HEADS, s // tq),
        in_specs=[
            pl.BlockSpec((1, tq, MLA_QK), lambda bi, h, i: (bi, i, h)),
            pl.BlockSpec((1, s, MLA_QK), lambda bi, h, i: (bi, 0, h)),
            pl.BlockSpec((1, s, 2 * MLA_V), lambda bi, h, i: (bi, 0, h)),
        ],
        out_specs=pl.BlockSpec((1, tq, MLA_V), lambda bi, h, i: (bi, i, h)),
        out_shape=jax.ShapeDtypeStruct((b, s, MLA_HEADS * MLA_V), BF16),
        compiler_params=_cparams(("parallel", "parallel", "parallel"), est),
    )(q, k, v)


def _gqa_kernel(q_ref, k_ref, v_ref, sink_ref, cos_ref, sin_ref, swap_ref, o_ref, k_scr, v_scr, *, scale):
    s_len = k_scr.shape[0]
    n_blocks = s_len // BAND
    rows = GQA_GROUP * BAND
    swap = swap_ref[...]

    def rot(x, cos_t, sin_t):
        return x.astype(F32) * cos_t + jnp.dot(x, swap, preferred_element_type=F32) * sin_t

    k_scr[...] = rot(k_ref[0], cos_ref[...], sin_ref[...]).astype(BF16)
    v_scr[:, :LANE] = v_ref[0]
    v_scr[:, LANE:] = jnp.ones((s_len, LANE), BF16)
    sink = sink_ref[0]
    rel = (lax.broadcasted_iota(jnp.int32, (rows, 3 * BAND), 1) - BAND
           - (lax.broadcasted_iota(jnp.int32, (rows, 3 * BAND), 0) & (BAND - 1)))
    bias = jnp.where(jnp.abs(rel) <= WINDOW, 0.0, NEG_INF)

    def window(n):
        r0 = n * BAND
        return max(0, r0 - BAND), min(s_len, r0 + 2 * BAND)

    def scores(n):
        r0 = n * BAND
        cos_t = jnp.concatenate([cos_ref[r0:r0 + BAND, :]] * GQA_GROUP, axis=0)
        sin_t = jnp.concatenate([sin_ref[r0:r0 + BAND, :]] * GQA_GROUP, axis=0)
        q_st = jnp.concatenate([q_ref[0, r0:r0 + BAND, g * LANE:(g + 1) * LANE] for g in range(GQA_GROUP)], axis=0)
        q_st = (rot(q_st, cos_t, sin_t) * scale).astype(BF16)
        lo, hi = window(n)
        c0 = lo - (r0 - BAND)
        s = lax.dot_general(q_st, k_scr[lo:hi, :], (((1,), (1,)), ((), ())), preferred_element_type=F32)
        return s + bias[:, c0:c0 + hi - lo]

    s_next = scores(0)
    for n in range(n_blocks):
        s = s_next
        if n + 1 < n_blocks:
            s_next = scores(n + 1)
        r0 = n * BAND
        lo, hi = window(n)
        m = jnp.maximum(jnp.max(s, axis=-1, keepdims=True), sink)
        p = jnp.exp(s - m).astype(BF16)
        oe = jnp.dot(p, v_scr[lo:hi, :], preferred_element_type=F32)
        o = oe[:, :LANE] / (oe[:, LANE:] + jnp.exp(sink - m))
        for g in range(GQA_GROUP):
            o_ref[0, r0:r0 + BAND, g * LANE:(g + 1) * LANE] = o[g * BAND:(g + 1) * BAND, :].astype(BF16)


def _gqa_attn(qkv, sinks, tabs):
    b, s, _ = qkv.shape
    gw = GQA_GROUP * GQA_HEAD_DIM
    sink_col = jnp.broadcast_to(sinks.astype(F32).reshape(GQA_KV_HEADS, GQA_GROUP, 1, 1),
                                (GQA_KV_HEADS, GQA_GROUP, BAND, 1)).reshape(GQA_KV_HEADS, GQA_GROUP * BAND, 1)
    est = (4 * _nbytes((s, gw), BF16) + 7 * _nbytes((s, LANE), BF16) + 6 * _nbytes((s, LANE), F32)
           + 8 * _nbytes((GQA_GROUP * BAND, 3 * BAND), F32))
    tab = pl.BlockSpec((s, LANE), lambda bi, h: (0, 0))
    cos_t, sin_t = tabs
    half = GQA_ROT // 2
    lane = jnp.arange(LANE)
    swap = (((lane[None, :] < half) & (lane[:, None] == lane[None, :] + half))
            | ((lane[None, :] >= half) & (lane[None, :] < 2 * half) & (lane[:, None] == lane[None, :] - half)))
    return pl.pallas_call(
        functools.partial(_gqa_kernel, scale=GQA_HEAD_DIM ** -0.5),
        name="gqa_window_attn",
        grid=(b, GQA_KV_HEADS),
        in_specs=[
            pl.BlockSpec((1, s, gw), lambda bi, h: (bi, 0, h)),
            pl.BlockSpec((1, s, LANE), lambda bi, h: (bi, 0, GQA_HEADS + h)),
            pl.BlockSpec((1, s, LANE), lambda bi, h: (bi, 0, GQA_HEADS + GQA_KV_HEADS + h)),
            pl.BlockSpec((1, GQA_GROUP * BAND, 1), lambda bi, h: (h, 0, 0)),
            tab, tab,
            pl.BlockSpec((LANE, LANE), lambda bi, h: (0, 0)),
        ],
        out_specs=pl.BlockSpec((1, s, gw), lambda bi, h: (bi, 0, h)),
        out_shape=jax.ShapeDtypeStruct((b, s, GQA_HEADS * GQA_HEAD_DIM), BF16),
        scratch_shapes=[pltpu.VMEM((s, LANE), BF16), pltpu.VMEM((s, 2 * LANE), BF16)],
        compiler_params=_cparams(("parallel", "parallel"), est),
    )(qkv, qkv, qkv, sink_col, cos_t, sin_t, swap.astype(BF16))


def _proj_short_conv_kernel(h_ref, w_ref, cw_ref, cb_ref, o_ref, *, sub_cols):
    tn = o_ref.shape[-1]
    h = h_ref[0]

    def project(c0):
        return lax.dot_general(h, w_ref[0, c0:c0 + sub_cols, :].astype(BF16), _NT_DIMS, preferred_element_type=F32)

    x_next = project(0)
    for c0 in range(0, tn, sub_cols):
        x = x_next
        if c0 + sub_cols < tn:
            x_next = project(c0 + sub_cols)
        cols = slice(c0, c0 + sub_cols)
        s_len = x.shape[0]
        row = lax.broadcasted_iota(jnp.int32, x.shape, 0)
        prev = jnp.where(row == 0, 0.0, pltpu.roll(x, 1, 0))
        nxt = jnp.where(row == s_len - 1, 0.0, pltpu.roll(x, s_len - 1, 0))
        y = cb_ref[:, cols] + prev * cw_ref[0:1, cols]
        y = y + x * cw_ref[1:2, cols]
        y = y + nxt * cw_ref[2:3, cols]
        o_ref[0, :, cols] = y.astype(BF16)


def _proj_short_conv(h, wt_all, layer, row0, n, cw, cb, tn=1024, sub_cols=512):
    b, s, d = h.shape
    assert row0 % 8 == 0 and n % tn == 0
    est = (2 * _nbytes((s, d), BF16) + 2 * _nbytes((d, tn), F32) + 2 * _nbytes((d, sub_cols), BF16)
           + 2 * _nbytes((s, tn), BF16) + 8 * _nbytes((s, sub_cols), F32))
    return pl.pallas_call(
        functools.partial(_proj_short_conv_kernel, sub_cols=sub_cols),
        name="hyena_proj_short_conv",
        grid=(b, n // tn),
        in_specs=[
            pl.BlockSpec((1, s, d), lambda bi, j: (bi, 0, 0)),
            pl.BlockSpec((pl.Element(1), pl.Element(tn), pl.Element(d)),
                         lambda bi, j: (layer, pl.multiple_of(row0 + j * tn, 8), 0)),
            pl.BlockSpec((HY_SHORT, tn), lambda bi, j: (0, j)),
            pl.BlockSpec((1, tn), lambda bi, j: (0, j)),
        ],
        out_specs=pl.BlockSpec((1, s, tn), lambda bi, j: (bi, 0, j)),
        out_shape=jax.ShapeDtypeStruct((b, s, n), BF16),
        compiler_params=_cparams(("parallel", "parallel"), est),
    )(h, wt_all, cw, cb.reshape(1, n))


def _filter_mlp_kernel(z_ref, w1_ref, b1_ref, w2_ref, b2_ref, w3_ref, b3_ref, fr_ref, h_ref):
    dot = functools.partial(jnp.dot, preferred_element_type=F32, precision=HIGHEST)
    fr = fr_ref[...]
    h = jnp.sin(fr * (dot(z_ref[...], w1_ref[...]) + b1_ref[...]))
    h = jnp.sin(fr * (dot(h, w2_ref[...]) + b2_ref[...]))
    h_ref[...] = jnp.sin(fr * (dot(h, w3_ref[...]) + b3_ref[...]))


def _filter_mlp(z, w1, b1, w2, b2, w3, b3, freq):
    n_lag = z.shape[0]
    full = lambda shape: pl.BlockSpec(shape, lambda i: (0,) * len(shape))
    est = 12 * _nbytes((n_lag, LANE), F32)
    return pl.pallas_call(
        _filter_mlp_kernel,
        name="hyena_filter_mlp",
        grid=(1,),
        in_specs=[full(z.shape), full(w1.shape), full((1, HY_FFN)), full(w2.shape), full((1, HY_FFN)),
                  full(w3.shape), full((1, HY_FFN)), full((1, HY_FFN))],
        out_specs=full((n_lag, HY_FFN)),
        out_shape=jax.ShapeDtypeStruct((n_lag, HY_FFN), F32),
        compiler_params=_cparams(("arbitrary",), est),
    )(z, w1, b1.reshape(1, -1), w2, b2.reshape(1, -1), w3, b3.reshape(1, -1), freq.reshape(1, -1))


def _filter_gen_kernel(h_ref, w4f_ref, w4b_ref, ksum_ref, kdiff_ref, nyq_ref):
    h = h_ref[...].astype(BF16)
    hf = jnp.dot(h, w4f_ref[...].astype(BF16), preferred_element_type=F32)
    hb = jnp.dot(h, w4b_ref[...].astype(BF16), preferred_element_type=F32)
    n_lag, tn = hf.shape
    row = lax.broadcasted_iota(jnp.int32, (n_lag, tn), 0)
    col = pl.program_id(1) * tn + lax.broadcasted_iota(jnp.int32, (1, tn), 1)
    t = row.astype(F32) / (n_lag - 1)
    max_decay = math.log(HY_TARGET) / HY_DECAY_PCT_SHORT
    min_decay = math.log(HY_TARGET) / HY_DECAY_PCT_LONG
    delta = min_decay + (max_decay - min_decay) * (col.astype(F32) / (HY_D - 1))
    decay = jnp.exp(-t * jnp.abs(delta))
    kf = hf * decay
    kb = jnp.where(row == 0, 0.0, hb * decay)
    inv = 1.0 / (jnp.sum(jnp.abs(kf), axis=0, keepdims=True) + jnp.sum(jnp.abs(kb), axis=0, keepdims=True))
    ksum = (kf + kb) * inv
    ksum_ref[0] = ksum.astype(BF16)
    kdiff_ref[0] = ((kb - kf) * inv).astype(BF16)
    alt = (1 - 2 * (row & 1)).astype(F32)
    nyq_ref[0] = jnp.sum(ksum * alt, axis=0, keepdims=True) * (1.0 / DFT_N)


def _filter_gen(h, w4, tn=256):
    n_lag = h.shape[0]
    nblk = HY_D // tn
    est = 10 * _nbytes((n_lag, tn), F32) + 4 * _nbytes((n_lag, LANE), F32) + 4 * _nbytes((n_lag, tn), BF16)
    out = jax.ShapeDtypeStruct((HY_ORDER, n_lag, HY_D), BF16)
    return pl.pallas_call(
        _filter_gen_kernel,
        name="hyena_filter_gen",
        grid=(HY_ORDER, nblk),
        in_specs=[
            pl.BlockSpec((n_lag, HY_FFN), lambda o, j: (0, 0)),
            pl.BlockSpec((HY_FFN, tn), lambda o, j: (0, 2 * o * nblk + j)),
            pl.BlockSpec((HY_FFN, tn), lambda o, j: (0, (2 * o + 1) * nblk + j)),
        ],
        out_specs=[pl.BlockSpec((1, n_lag, tn), lambda o, j: (o, 0, j))] * 2
        + [pl.BlockSpec((1, 1, tn), lambda o, j: (o, 0, j))],
        out_shape=[out, out, jax.ShapeDtypeStruct((HY_ORDER, 1, HY_D), F32)],
        compiler_params=_cparams(("parallel", "parallel"), est),
    )(h, w4, w4)


def _filter_dft_kernel(ct_ref, st_ref, ksum_ref, kdiff_ref, a_ref, bq_ref):
    re = jnp.dot(ct_ref[...], ksum_ref[0], preferred_element_type=F32)
    im = jnp.dot(st_ref[...], kdiff_ref[0], preferred_element_type=F32)
    tf = re.shape[0]
    row = pl.program_id(2) * tf + lax.broadcasted_iota(jnp.int32, re.shape, 0)
    is0 = row == 0
    wgt = jnp.where(is0, 1.0 / DFT_N, 2.0 / DFT_N)
    a_ref[0] = (re * wgt).astype(BF16)
    bq_ref[0] = jnp.where(is0, 0.0, im * wgt).astype(BF16)


def _filter_dft(ct, st, ksum, kdiff, tf=512, tn=512):
    n_f, n_s = ct.shape
    est = (4 * _nbytes((tf, n_s), BF16) + 4 * _nbytes((n_s, tn), BF16) + 4 * _nbytes((tf, tn), BF16)
           + 8 * _nbytes((tf, tn), F32))
    out = jax.ShapeDtypeStruct((HY_ORDER, n_f, HY_D), BF16)
    return pl.pallas_call(
        _filter_dft_kernel,
        name="hyena_filter_dft",
        grid=(HY_ORDER, HY_D // tn, n_f // tf),
        in_specs=[
            pl.BlockSpec((tf, n_s), lambda o, j, k: (k, 0)),
            pl.BlockSpec((tf, n_s), lambda o, j, k: (k, 0)),
            pl.BlockSpec((1, n_s, tn), lambda o, j, k: (o, 0, j)),
            pl.BlockSpec((1, n_s, tn), lambda o, j, k: (o, 0, j)),
        ],
        out_specs=[pl.BlockSpec((1, tf, tn), lambda o, j, k: (o, k, j))] * 2,
        out_shape=[out, out],
        compiler_params=_cparams(("parallel", "parallel", "parallel"), est),
    )(ct, st, ksum, kdiff)


def _alt_sign(idx):
    return (1 - 2 * (idx & 1)).astype(F32)


def _long_conv_kernel(ce_ref, se_ref, co_ref, so_ref, cot_ref, sot_ref, flip_ref, u_ref, g_ref,
                      ae_ref, bqe_ref, ao_ref, bqo_ref, knyq_ref, skip_ref, o_ref,
                      e_scr, o_scr, yre_scr, yse_scr, yro_scr, yso_scr, d_scr):
    hh = e_scr.shape[0]
    sub = CONV_SUB_ROWS
    dot = functools.partial(jnp.dot, preferred_element_type=F32)
    alt_h = _alt_sign(lax.broadcasted_iota(jnp.int32, (hh, 1), 0))

    u_lo = u_ref[0, :hh, :].astype(F32)
    u_rev = dot(flip_ref[...], u_ref[0, hh:, :])
    e32 = u_lo + u_rev
    e_scr[...] = e32.astype(BF16)
    o_scr[...] = (u_lo - u_rev).astype(BF16)
    mid = u_ref[0, hh:hh + 16, :].astype(F32)[0:1]
    nyq_u = jnp.sum(e32 * alt_h, axis=0, keepdims=True) + mid
    e, o = e_scr[...], o_scr[...]

    for r0 in range(0, hh, sub):
        rs = slice(r0, r0 + sub)
        m = r0 + lax.broadcasted_iota(jnp.int32, (sub, 1), 0)
        corr = _alt_sign(m) * mid
        uc = dot(ce_ref[rs, :], e) + corr
        us = dot(se_ref[rs, :], o)
        a, bq = ae_ref[0, rs, :].astype(F32), bqe_ref[0, rs, :].astype(F32)
        yre_scr[rs, :] = (uc * a + us * bq).astype(BF16)
        ys = us * a - uc * bq
        if r0 == 0:
            ys = jnp.where(m == 0, nyq_u * knyq_ref[0], ys)
        yse_scr[rs, :] = ys.astype(BF16)
        uc = dot(co_ref[rs, :], o)
        us = dot(so_ref[rs, :], e) + corr
        a, bq = ao_ref[0, rs, :].astype(F32), bqo_ref[0, rs, :].astype(F32)
        yro_scr[rs, :] = (uc * a + us * bq).astype(BF16)
        yso_scr[rs, :] = (us * a - uc * bq).astype(BF16)

    yre, yse, yro, yso = yre_scr[...], yse_scr[...], yro_scr[...], yso_scr[...]
    nyq = yse_scr[0:16, :].astype(F32)[0:1]
    y_mid = jnp.sum((yre.astype(F32) + yso.astype(F32)) * alt_h, axis=0, keepdims=True) + nyq
    skip = skip_ref[0]
    for r0 in range(0, hh, sub):
        rs = slice(r0, r0 + sub)
        t = r0 + lax.broadcasted_iota(jnp.int32, (sub, 1), 0)
        pa = dot(ce_ref[rs, :], yre) + dot(sot_ref[rs, :], yso) + _alt_sign(t) * nyq
        pb = dot(se_ref[rs, :], yse) + dot(cot_ref[rs, :], yro)
        u_t = u_ref[0, rs, :].astype(F32)
        o_ref[0, rs, :] = (g_ref[0, rs, :].astype(F32) * (pa + pb + u_t * skip)).astype(BF16)
        d_scr[rs, :] = (pa - pb).astype(BF16)
    y_hi = dot(flip_ref[...], d_scr[...])
    y_hi = jnp.where(lax.broadcasted_iota(jnp.int32, (hh, 1), 0) == 0, y_mid, y_hi)
    u_hi = u_ref[0, hh:, :].astype(F32)
    o_ref[0, hh:, :] = (g_ref[0, hh:, :].astype(F32) * (y_hi + u_hi * skip)).astype(BF16)


def _long_conv(tabs, spec, order, usrc, u_blk0, gsrc, g_blk0, skip, tn=512):
    a, bq, knyq = spec
    b, s, _ = usrc.shape
    hh = s // 2
    nblk = HY_D // tn
    est = (7 * _nbytes((hh, hh), BF16) + 10 * _nbytes((s, tn), BF16) + 8 * _nbytes((hh, tn), BF16)
           + 7 * _nbytes((hh, tn), BF16) + 8 * _nbytes((hh, tn), F32) + 12 * _nbytes((CONV_SUB_ROWS, tn), F32))
    tab = pl.BlockSpec((hh, hh), lambda bi, j: (0, 0), pipeline_mode=pl.Buffered(1))
    even = pl.BlockSpec((1, hh, tn), lambda bi, j: (order, 0, j))
    odd = pl.BlockSpec((1, hh, tn), lambda bi, j: (order, 1, j))
    row = pl.BlockSpec((1, 1, tn), lambda bi, j: (order, 0, j))
    return pl.pallas_call(
        _long_conv_kernel,
        name="hyena_long_conv",
        grid=(b, nblk),
        in_specs=[
            tab, tab, tab, tab, tab, tab, tab,
            pl.BlockSpec((1, s, tn), lambda bi, j: (bi, 0, u_blk0 * nblk + j)),
            pl.BlockSpec((1, s, tn), lambda bi, j: (bi, 0, g_blk0 * nblk + j)),
            even, even, odd, odd, row, row,
        ],
        out_specs=pl.BlockSpec((1, s, tn), lambda bi, j: (bi, 0, j)),
        out_shape=jax.ShapeDtypeStruct((b, s, HY_D), BF16),
        scratch_shapes=[pltpu.VMEM((hh, tn), BF16)] * 7,
        compiler_params=_cparams(("parallel", "parallel"), est),
    )(tabs["ce"], tabs["se"], tabs["co"], tabs["so"], tabs["cot"], tabs["sot"], tabs["flip"], usrc, gsrc,
      a, bq, a, bq, knyq, skip.reshape(HY_ORDER, 1, HY_D))


def _rope_tables(seq, rot_dim):
    half = rot_dim // 2
    pos = jnp.arange(seq, dtype=F32)
    inv = ROPE_THETA ** (-jnp.arange(0, rot_dim, 2, dtype=F32) / rot_dim)
    ang = pos[:, None] * inv[None, :]
    cos, sin = jnp.cos(ang), jnp.sin(ang)
    rest = LANE - rot_dim
    return cos, sin, half, rest


def _rope_lane_tables(seq, rot_dim, rest_passthrough):
    cos, sin, half, rest = _rope_tables(seq, rot_dim)
    fill = jnp.ones((seq, rest), F32) if rest_passthrough else jnp.zeros((seq, rest), F32)
    zero_h = jnp.zeros((seq, half), F32)
    zero_r = jnp.zeros((seq, rest), F32)
    cos_t = jnp.concatenate([cos, cos, fill], axis=1)
    sin_up = jnp.concatenate([zero_h, sin, zero_r], axis=1)
    sin_dn = jnp.concatenate([-sin, zero_h, zero_r], axis=1)
    return cos_t, sin_up, sin_dn


def _dft_tables(n_half):
    hh = n_half // 2
    blk = 32
    idx = jnp.arange(n_half, dtype=jnp.int32)
    unit = 2.0 * math.pi / n_half
    ang_a = ((blk * idx[:hh // blk, None] * idx[None, :]) & (n_half - 1)).astype(F32) * unit
    ang_b = ((idx[:blk, None] * idx[None, :]) & (n_half - 1)).astype(F32) * unit
    ca, sa = jnp.cos(ang_a)[:, None, :], jnp.sin(ang_a)[:, None, :]
    cb, sb = jnp.cos(ang_b)[None], jnp.sin(ang_b)[None]
    ce_f = (ca * cb - sa * sb).reshape(hh, n_half)
    se_f = (sa * cb + ca * sb).reshape(hh, n_half)
    turn = idx.astype(F32) * (0.5 * unit)
    c_row, s_row = jnp.cos(turn)[None, :], jnp.sin(turn)[None, :]
    co_f = ce_f * c_row - se_f * s_row
    so_f = se_f * c_row + ce_f * s_row
    alt = (1 - 2 * (idx & 1)).astype(F32)
    ce, se = ce_f[:, :hh], se_f[:, :hh]
    c_col, s_col = jnp.cos(turn[:hh])[:, None], jnp.sin(turn[:hh])[:, None]
    flip = (idx[:hh, None] >= 1) & (idx[None, :hh] == hh - idx[:hh, None])
    tabs = dict(
        ct_eo=jnp.concatenate([ce_f, co_f], axis=0),
        st_eo=jnp.concatenate([jnp.where(idx[:hh, None] == 0, alt[None, :], se_f), so_f], axis=0),
        ce=ce, se=se, co=co_f[:, :hh], so=so_f[:, :hh],
        cot=ce * c_col - se * s_col, sot=se * c_col + ce * s_col, flip=flip)
    return {name: t.astype(BF16) for name, t in tabs.items()}


def _filter_features(n_lag):
    t = jnp.linspace(0.0, 1.0, n_lag, dtype=F32)[:, None]
    bands = (HY_EMB - 1) // 2
    wpos = 2.0 * math.pi * jnp.arange(n_lag, dtype=F32) / n_lag
    fb = jnp.linspace(1e-4, bands - 1, bands, dtype=F32)
    fw = wpos[:, None] * fb[None, :]
    z = jnp.concatenate([t, jnp.cos(fw), -jnp.sin(fw)], axis=-1)
    return jnp.pad(z, ((0, 0), (0, LANE - HY_EMB)))


def kernel(x, c, ada_mix_w, ada_mix_b, norm_mix_g, ada_mlp_w, ada_mlp_b, norm_mlp_g, w_mlp_in, w_mlp_out, e_w_in, e_q_norm_g, e_kv_norm_g, e_w_uq, e_w_ukv, e_conv_w, e_conv_b, e_f_w1, e_f_b1, e_f_w2, e_f_b2, e_f_w3, e_f_b3, e_f_freq, e_f_w4, e_hy_skip, e_w_out, o_w_qkv, o_sinks, o_w_o, final_norm_g):
    b, s, d = x.shape
    c8 = jnp.pad(c, ((0, 8 - b), (0, 0)))
    mod_mix = _ada(c8, ada_mix_w, ada_mix_b)[:, :b].reshape(DEPTH, b, 1, 3 * d)
    mod_mlp = _ada(c8, ada_mlp_w, ada_mlp_b)[:, :b].reshape(DEPTH, b, 1, 3 * d)

    mla_tabs = _rope_lane_tables(s, MLA_ROPE, rest_passthrough=False)
    gqa_cos, gqa_sin_up, gqa_sin_dn = _rope_lane_tables(s, GQA_ROT, rest_passthrough=True)
    gqa_tabs = (gqa_cos, gqa_sin_up + gqa_sin_dn)
    dft = _dft_tables(s)
    z_feat = _filter_features(s)
    q_scale = (MLA_NOPE + MLA_ROPE) ** -0.5 * math.log2(math.e)
    w_in_t = jnp.swapaxes(e_w_in, 1, 2)
    hy_row0 = Q_LORA + KV_LORA + MLA_ROPE

    w_uq = e_w_uq.reshape(-1, Q_LORA, MLA_HEADS, MLA_NOPE + MLA_ROPE)
    wq_all = jnp.concatenate([
        w_uq[..., :MLA_NOPE].reshape(-1, Q_LORA, MLA_HEADS * MLA_NOPE),
        jnp.pad(w_uq[..., MLA_NOPE:], ((0, 0), (0, 0), (0, 0), (0, LANE - MLA_ROPE))).reshape(
            -1, Q_LORA, MLA_HEADS * LANE),
    ], axis=2).astype(BF16)
    w_ukv = e_w_ukv.reshape(-1, KV_LORA, MLA_HEADS, MLA_NOPE + MLA_V)
    wk_all = w_ukv[..., :MLA_NOPE].reshape(-1, KV_LORA, MLA_HEADS * MLA_NOPE).astype(BF16)
    wv_all = w_ukv[..., MLA_NOPE:].reshape(-1, KV_LORA, MLA_HEADS * MLA_V).astype(BF16)

    for l in range(DEPTH):
        i = l // 2
        if l % 2 == 0:
            h = _normmod(x, mod_mix[l], norm_mix_g[l])
            u = _proj_short_conv(h, w_in_t, i, hy_row0, 3 * HY_D, e_conv_w[i], e_conv_b[i])

            q, k, v = _mla_proj(h, w_in_t, e_q_norm_g[i], e_kv_norm_g[i], wq_all, wk_all, wv_all, i, mla_tabs,
                                q_scale)
            a_mla = _mla_attn(q, k, v)

            w1 = jnp.pad(e_f_w1[i], ((0, LANE - HY_EMB), (0, 0)))
            h_filt = _filter_mlp(z_feat, w1, e_f_b1[i], e_f_w2[i], e_f_b2[i], e_f_w3[i], e_f_b3[i], e_f_freq[i])
            ksum, kdiff, knyq = _filter_gen(h_filt, e_f_w4[i])
            spec = (*_filter_dft(dft["ct_eo"], dft["st_eo"], ksum, kdiff), knyq)
            zc = _long_conv(dft, spec, 0, u, 0, u, 1, e_hy_skip[i])
            b_hy = _long_conv(dft, spec, 1, zc, 0, u, 2, e_hy_skip[i])

            x = _proj_res([a_mla, b_hy], e_w_out, i, x, mod_mix[l])
        else:
            h = _normmod(x, mod_mix[l], norm_mix_g[l])
            qkv = _matmul(h, o_w_qkv, i, BF16)
            o = _gqa_attn(qkv, o_sinks[i], gqa_tabs)
            x = _proj_res([o], o_w_o, i, x, mod_mix[l])
        x = _mlp(x, mod_mlp[l], norm_mlp_g[l], w_mlp_in, w_mlp_out, l, final_norm_g, l == DEPTH - 1)
    return x
```

```python
import functools
import math

import jax
import jax.numpy as jnp
from jax import lax
from jax.experimental import pallas as pl
from jax.experimental.pallas import tpu as pltpu

F32 = jnp.float32
BF16 = jnp.bfloat16
HIGHEST = lax.Precision.HIGHEST

D_MODEL = 2048
SEQ = 2048
DEPTH = 4
RMS_EPS = 1e-6
ROPE_THETA = 500000.0
NEG_INF = -1e30
MLA_HEADS = 8
MLA_NOPE = 128
MLA_ROPE = 64
MLA_V = 128
Q_LORA = 512
KV_LORA = 256
HY_D = 1024
HY_ORDER = 2
HY_SHORT = 3
HY_EMB = 33
HY_FFN = 64
HY_DECAY_PCT_SHORT = 0.3
HY_DECAY_PCT_LONG = 1.5
HY_TARGET = 1e-2
GQA_HEADS = 16
GQA_KV_HEADS = 4
GQA_HEAD_DIM = 128
GQA_ROT = GQA_HEAD_DIM // 4
GQA_GROUP = GQA_HEADS // GQA_KV_HEADS
WINDOW = 128
BAND = 128
LAT_W = Q_LORA + KV_LORA + 128
MLA_QK = 256
DFT_N = 2 * SEQ
CONV_SUB_ROWS = 256

LANE = 128
V7X_VMEM_BYTES = 64 * 1024 * 1024
V7X_VMEM_BUDGET = V7X_VMEM_BYTES * 7 // 8
ROW_CHUNK = 128


def _cparams(semantics, est_bytes):
    limit = int(min(V7X_VMEM_BUDGET, max(V7X_VMEM_BYTES // 2, est_bytes * 3 // 2)))
    return pltpu.CompilerParams(dimension_semantics=semantics, vmem_limit_bytes=limit)


def _nbytes(shape, dtype):
    return math.prod(shape) * jnp.dtype(dtype).itemsize


def _rms(x, g):
    ms = jnp.mean(x * x, axis=-1, keepdims=True)
    return x * lax.rsqrt(ms + RMS_EPS) * g


def _ada_kernel(c_ref, w_ref, b_ref, o_ref):
    cv = c_ref[...]
    s = cv * (1.0 / (1.0 + jnp.exp(-cv)))
    o_ref[0] = jnp.dot(s.astype(BF16), w_ref[0].astype(BF16), preferred_element_type=F32) + b_ref[0]


def _ada(c8, w, b, tn=1536):
    n_l, d, n = w.shape
    est = 2 * _nbytes((d, tn), F32) + _nbytes((d, tn), BF16) + 4 * _nbytes((8, tn), F32) + 2 * _nbytes((8, d), F32)
    return pl.pallas_call(
        _ada_kernel,
        name="ada_modulation",
        grid=(n_l, n // tn),
        in_specs=[
            pl.BlockSpec((8, d), lambda l, j: (0, 0)),
            pl.BlockSpec((1, d, tn), lambda l, j: (l, 0, j)),
            pl.BlockSpec((1, 1, tn), lambda l, j: (l, 0, j)),
        ],
        out_specs=pl.BlockSpec((1, 8, tn), lambda l, j: (l, 0, j)),
        out_shape=jax.ShapeDtypeStruct((n_l, 8, n), F32),
        compiler_params=_cparams(("parallel", "parallel"), est),
    )(c8, w, b.reshape(n_l, 1, n))


def _normmod_rows(x_ref, shift_ref, scale_ref, g_ref, h_ref, rows):
    gain = g_ref[...] * (1.0 + scale_ref[0])
    sh = shift_ref[0]
    tm = h_ref.shape[0]

    def body(r, carry):
        sl = pl.ds(pl.multiple_of(r * rows, rows), rows)
        xv = x_ref[0, sl, :]
        inv = lax.rsqrt(jnp.mean(xv * xv, axis=-1, keepdims=True) + RMS_EPS)
        h_ref[sl, :] = (xv * inv * gain + sh).astype(BF16)
        return carry

    lax.fori_loop(0, tm // rows, body, 0)


def _normmod_kernel(x_ref, shift_ref, scale_ref, g_ref, h_ref):
    _normmod_rows(x_ref, shift_ref, scale_ref, g_ref, h_ref.at[0], ROW_CHUNK)


def _normmod(x, mod, g, tm=1024):
    b, s, d = x.shape
    est = 2 * _nbytes((tm, d), F32) + 2 * _nbytes((tm, d), BF16) + 8 * _nbytes((128, d), F32)
    return pl.pallas_call(
        _normmod_kernel,
        name="normmod",
        grid=(b, s // tm),
        in_specs=[
            pl.BlockSpec((1, tm, d), lambda bi, i: (bi, i, 0)),
            pl.BlockSpec((1, 1, d), lambda bi, i: (bi, 0, 0)),
            pl.BlockSpec((1, 1, d), lambda bi, i: (bi, 0, 1)),
            pl.BlockSpec((1, d), lambda bi, i: (0, 0)),
        ],
        out_specs=pl.BlockSpec((1, tm, d), lambda bi, i: (bi, i, 0)),
        out_shape=jax.ShapeDtypeStruct((b, s, d), BF16),
        compiler_params=_cparams(("parallel", "parallel"), est),
    )(x, mod, mod, g.reshape(1, d))


_NT_DIMS = (((1,), (1,)), ((), ()))


def _matmul_kernel(h_ref, w_ref, o_ref):
    o_ref[0] = jnp.dot(h_ref[0], w_ref[...].astype(BF16), preferred_element_type=F32).astype(o_ref.dtype)


def _matmul(h, w_all, layer, out_dtype, tm=2048, tn=1024):
    b, s, d = h.shape
    n = w_all.shape[2]
    est = (2 * _nbytes((tm, d), BF16) + 2 * _nbytes((d, tn), F32) + _nbytes((d, tn), BF16)
           + 3 * _nbytes((tm, tn), F32))
    return pl.pallas_call(
        _matmul_kernel,
        name="matmul",
        grid=(b, s // tm, n // tn),
        in_specs=[
            pl.BlockSpec((1, tm, d), lambda bi, i, j: (bi, i, 0)),
            pl.BlockSpec((None, d, tn), lambda bi, i, j: (layer, 0, j)),
        ],
        out_specs=pl.BlockSpec((1, tm, tn), lambda bi, i, j: (bi, i, j)),
        out_shape=jax.ShapeDtypeStruct((b, s, n), out_dtype),
        compiler_params=_cparams(("parallel", "parallel", "parallel"), est),
    )(h, w_all)


def _mlp_kernel(x_ref, shift_ref, scale_ref, gate_ref, g_ref, w1_ref, w2_ref, fg_ref, o_ref, h_ref, *, final_norm):
    f = pl.program_id(2)

    @pl.when(f == 0)
    def _():
        _normmod_rows(x_ref, shift_ref, scale_ref, g_ref, h_ref, ROW_CHUNK)

        o_ref[...] = jnp.zeros_like(o_ref)

    a = jnp.dot(h_ref[...], w1_ref[...].astype(BF16), preferred_element_type=F32)
    a = jnp.square(jnp.maximum(a, 0.0)).astype(BF16)
    o_ref[0] += jnp.dot(a, w2_ref[...].astype(BF16), preferred_element_type=F32)

    @pl.when(f == pl.num_programs(2) - 1)
    def _():
        gate = gate_ref[0]
        fg = fg_ref[...]
        rows = ROW_CHUNK

        def body(r, carry):
            sl = pl.ds(pl.multiple_of(r * rows, rows), rows)
            y = x_ref[0, sl, :] + gate * o_ref[0, sl, :]
            if final_norm:
                y = _rms(y, fg)
            o_ref[0, sl, :] = y
            return carry

        lax.fori_loop(0, o_ref.shape[1] // rows, body, 0)


def _mlp(x, mod, g, w1_all, w2_all, layer, final_g, final_norm, tm=1024, tf=512):
    b, s, d = x.shape
    ff = w1_all.shape[2]
    est = (4 * _nbytes((tm, d), F32) + _nbytes((tm, d), BF16) + 4 * _nbytes((d, tf), F32)
           + 2 * _nbytes((d, tf), BF16) + 2 * _nbytes((tm, tf), F32))
    return pl.pallas_call(
        functools.partial(_mlp_kernel, final_norm=final_norm),
        name="mlp_relu2",
        grid=(b, s // tm, ff // tf),
        in_specs=[
            pl.BlockSpec((1, tm, d), lambda bi, i, f: (bi, i, 0)),
            pl.BlockSpec((1, 1, d), lambda bi, i, f: (bi, 0, 0)),
            pl.BlockSpec((1, 1, d), lambda bi, i, f: (bi, 0, 1)),
            pl.BlockSpec((1, 1, d), lambda bi, i, f: (bi, 0, 2)),
            pl.BlockSpec((1, d), lambda bi, i, f: (0, 0)),
            pl.BlockSpec((None, d, tf), lambda bi, i, f: (layer, 0, f)),
            pl.BlockSpec((None, tf, d), lambda bi, i, f: (layer, f, 0)),
            pl.BlockSpec((1, d), lambda bi, i, f: (0, 0)),
        ],
        out_specs=pl.BlockSpec((1, tm, d), lambda bi, i, f: (bi, i, 0)),
        out_shape=jax.ShapeDtypeStruct((b, s, d), F32),
        scratch_shapes=[pltpu.VMEM((tm, d), BF16)],
        compiler_params=_cparams(("parallel", "parallel", "arbitrary"), est),
    )(x, mod, mod, mod, g.reshape(1, d), w1_all, w2_all, final_g.reshape(1, d))


def _proj_res_kernel(*refs, n_in):
    a_refs = refs[:n_in]
    w_ref, x_ref, gate_ref, o_ref, wb_scr = refs[n_in:]
    rows = 2 * ROW_CHUNK

    @pl.when((pl.program_id(0) == 0) & (pl.program_id(1) == 0))
    def _():
        def body(r, carry):
            sl = pl.ds(pl.multiple_of(r * rows, rows), rows)
            wb_scr[sl, :] = w_ref[sl, :].astype(BF16)
            return carry

        lax.fori_loop(0, w_ref.shape[0] // rows, body, 0)

    kk = a_refs[0].shape[-1]
    acc = None
    for r, a_ref in enumerate(a_refs):
        p = jnp.dot(a_ref[0], wb_scr[r * kk:(r + 1) * kk, :], preferred_element_type=F32)
        acc = p if acc is None else acc + p
    o_ref[0] = x_ref[0] + gate_ref[0] * acc


def _proj_res(a_list, w_all, layer, x, mod, tm=512):
    b, s, d = x.shape
    n_in = len(a_list)
    kk = a_list[0].shape[-1]
    k_all = n_in * kk
    assert all(a.shape[-1] == kk for a in a_list) and w_all.shape[1] == k_all
    est = (_nbytes((k_all, d), F32) + _nbytes((k_all, d), BF16) + 2 * _nbytes((tm, k_all), BF16)
           + 6 * _nbytes((tm, d), F32))
    in_specs = [pl.BlockSpec((1, tm, kk), lambda bi, i: (bi, i, 0)) for _ in a_list]
    in_specs += [
        pl.BlockSpec((None, k_all, d), lambda bi, i: (layer, 0, 0), pipeline_mode=pl.Buffered(1)),
        pl.BlockSpec((1, tm, d), lambda bi, i: (bi, i, 0)),
        pl.BlockSpec((1, 1, d), lambda bi, i: (bi, 0, 2)),
    ]
    return pl.pallas_call(
        functools.partial(_proj_res_kernel, n_in=n_in),
        name="proj_residual",
        grid=(b, s // tm),
        in_specs=in_specs,
        out_specs=pl.BlockSpec((1, tm, d), lambda bi, i: (bi, i, 0)),
        out_shape=jax.ShapeDtypeStruct((b, s, d), F32),
        scratch_shapes=[pltpu.VMEM((k_all, d), BF16)],
        compiler_params=_cparams(("arbitrary", "arbitrary"), est),
    )(*a_list, w_all, x, mod)


def _rot_lanes(blk, cos_t, sin_up, sin_dn, half):
    return (blk * cos_t + pltpu.roll(blk, half, 1) * sin_up + pltpu.roll(blk, LANE - half, 1) * sin_dn)


def _mla_proj_kernel(h_ref, wlat_ref, gq_ref, gkv_ref, wq_ref, wk_ref, wv_ref, cos_ref, sup_ref, sdn_ref,
                     q_ref, k_ref, v_ref, *, q_scale):
    lat = lax.dot_general(h_ref[0], wlat_ref[...].astype(BF16), _NT_DIMS, preferred_element_type=F32)
    qn = _rms(lat[:, :Q_LORA], gq_ref[...]).astype(BF16)
    kvn = _rms(lat[:, Q_LORA:Q_LORA + KV_LORA], gkv_ref[...]).astype(BF16)
    cos_t, sin_up, sin_dn = cos_ref[...], sup_ref[...], sdn_ref[...]
    half = MLA_ROPE // 2
    q = jnp.dot(qn, wq_ref[...], preferred_element_type=F32) * q_scale
    kn = jnp.dot(kvn, wk_ref[...], preferred_element_type=F32)
    v = jnp.dot(kvn, wv_ref[...], preferred_element_type=F32).astype(BF16)
    ones = jnp.ones((v.shape[0], MLA_V), BF16)
    kr = _rot_lanes(lat[:, Q_LORA + KV_LORA:], cos_t, sin_up, sin_dn, half).astype(BF16)
    nope_w = MLA_HEADS * MLA_NOPE
    for h in range(MLA_HEADS):
        c0 = h * MLA_QK
        v_ref[0, :, 2 * h * MLA_V:(2 * h + 1) * MLA_V] = v[:, h * MLA_V:(h + 1) * MLA_V]
        v_ref[0, :, (2 * h + 1) * MLA_V:(2 * h + 2) * MLA_V] = ones
        q_ref[0, :, c0:c0 + LANE] = q[:, h * LANE:(h + 1) * LANE].astype(BF16)
        qr = q[:, nope_w + h * LANE:nope_w + (h + 1) * LANE]
        q_ref[0, :, c0 + LANE:c0 + 2 * LANE] = _rot_lanes(qr, cos_t, sin_up, sin_dn, half).astype(BF16)
        k_ref[0, :, c0:c0 + LANE] = kn[:, h * LANE:(h + 1) * LANE].astype(BF16)
        k_ref[0, :, c0 + LANE:c0 + 2 * LANE] = kr


def _mla_proj(h, w_in_t, gq, gkv, wq_all, wk_all, wv_all, layer, tabs, q_scale, tm=512):
    b, s, d = h.shape
    hq = MLA_HEADS * MLA_QK
    hv = MLA_HEADS * 2 * MLA_V
    est = (2 * _nbytes((tm, d), BF16) + _nbytes((LAT_W, d), F32) + _nbytes((LAT_W, d), BF16)
           + 2 * _nbytes((tm, LAT_W), F32) + 2 * _nbytes(wq_all.shape[1:], BF16) + 2 * _nbytes(wk_all.shape[1:], BF16)
           + 2 * _nbytes(wv_all.shape[1:], BF16) + 4 * _nbytes((tm, hq), BF16) + 2 * _nbytes((tm, hv), BF16)
           + 3 * _nbytes((tm, hq), F32))
    full = lambda shape: pl.BlockSpec(shape, lambda bi, i: (0,) * len(shape))
    stacked = lambda w: pl.BlockSpec((None,) + w.shape[1:], lambda bi, i: (layer, 0, 0))
    tab = pl.BlockSpec((tm, LANE), lambda bi, i: (i, 0))
    return pl.pallas_call(
        functools.partial(_mla_proj_kernel, q_scale=q_scale),
        name="mla_proj",
        grid=(b, s // tm),
        in_specs=[
            pl.BlockSpec((1, tm, d), lambda bi, i: (bi, i, 0)),
            pl.BlockSpec((None, LAT_W, d), lambda bi, i: (layer, 0, 0), pipeline_mode=pl.Buffered(1)),
            full((1, Q_LORA)), full((1, KV_LORA)), stacked(wq_all), stacked(wk_all), stacked(wv_all),
            tab, tab, tab,
        ],
        out_specs=[
            pl.BlockSpec((1, tm, hq), lambda bi, i: (bi, i, 0)),
            pl.BlockSpec((1, tm, hq), lambda bi, i: (bi, i, 0)),
            pl.BlockSpec((1, tm, hv), lambda bi, i: (bi, i, 0)),
        ],
        out_shape=[
            jax.ShapeDtypeStruct((b, s, hq), BF16),
            jax.ShapeDtypeStruct((b, s, hq), BF16),
            jax.ShapeDtypeStruct((b, s, hv), BF16),
        ],
        compiler_params=_cparams(("parallel", "parallel"), est),
    )(h, w_in_t, gq.reshape(1, -1), gkv.reshape(1, -1), wq_all, wk_all, wv_all, *tabs)


def _mla_attn_kernel(q_ref, k_ref, v_ref, o_ref, *, chain_rows):
    n_chains = q_ref.shape[1] // chain_rows

    def scores(c):
        rs = slice(c * chain_rows, (c + 1) * chain_rows)
        return lax.dot_general(q_ref[0, rs, :], k_ref[0], (((1,), (1,)), ((), ())), preferred_element_type=F32)

    s_next = scores(0)
    for c in range(n_chains):
        s = s_next
        if c + 1 < n_chains:
            s_next = scores(c + 1)
        m = jnp.max(s, axis=-1, keepdims=True)
        p = jnp.exp2(s - m).astype(BF16)
        oe = jnp.dot(p, v_ref[0], preferred_element_type=F32)
        o_ref[0, c * chain_rows:(c + 1) * chain_rows, :] = (oe[:, :MLA_V] / oe[:, MLA_V:]).astype(BF16)


def _mla_attn(q, k, v, tq=2048, chain_rows=1024):
    b, s, _ = q.shape
    est = (2 * _nbytes((tq, MLA_QK), BF16) + 2 * _nbytes((s, MLA_QK), BF16) + 2 * _nbytes((s, 2 * MLA_V), BF16)
           + 2 * _nbytes((tq, MLA_V), BF16) + 5 * _nbytes((chain_rows, s), F32))
    return pl.pallas_call(
        functools.partial(_mla_attn_kernel, chain_rows=chain_rows),
        name="mla_attn",
        grid=(b, MLA_HEADS, s // tq),
        in_specs=[
            pl.BlockSpec((1, tq, MLA_QK), lambda bi, h, i: (bi, i, h)),
            pl.BlockSpec((1, s, MLA_QK), lambda bi, h, i: (bi, 0, h)),
            pl.BlockSpec((1, s, 2 * MLA_V), lambda bi, h, i: (bi, 0, h)),
        ],
        out_specs=pl.BlockSpec((1, tq, MLA_V), lambda bi, h, i: (bi, i, h)),
        out_shape=jax.ShapeDtypeStruct((b, s, MLA_HEADS * MLA_V), BF16),
        compiler_params=_cparams(("parallel", "parallel", "parallel"), est),
    )(q, k, v)


def _gqa_kernel(q_ref, k_ref, v_ref, sink_ref, cos_ref, sin_ref, swap_ref, o_ref, k_scr, v_scr, *, scale):
    s_len = k_scr.shape[0]
    n_blocks = s_len // BAND
    rows = GQA_GROUP * BAND
    swap = swap_ref[...]

    def rot(x, cos_t, sin_t):
        return x.astype(F32) * cos_t + jnp.dot(x, swap, preferred_element_type=F32) * sin_t

    k_scr[...] = rot(k_ref[0], cos_ref[...], sin_ref[...]).astype(BF16)
    v_scr[:, :LANE] = v_ref[0]
    v_scr[:, LANE:] = jnp.ones((s_len, LANE), BF16)
    sink = sink_ref[0]
    rel = (lax.broadcasted_iota(jnp.int32, (rows, 3 * BAND), 1) - BAND
           - (lax.broadcasted_iota(jnp.int32, (rows, 3 * BAND), 0) & (BAND - 1)))
    bias = jnp.where(jnp.abs(rel) <= WINDOW, 0.0, NEG_INF)

    def window(n):
        r0 = n * BAND
        return max(0, r0 - BAND), min(s_len, r0 + 2 * BAND)

    def scores(n):
        r0 = n * BAND
        cos_t = jnp.concatenate([cos_ref[r0:r0 + BAND, :]] * GQA_GROUP, axis=0)
        sin_t = jnp.concatenate([sin_ref[r0:r0 + BAND, :]] * GQA_GROUP, axis=0)
        q_st = jnp.concatenate([q_ref[0, r0:r0 + BAND, g * LANE:(g + 1) * LANE] for g in range(GQA_GROUP)], axis=0)
        q_st = (rot(q_st, cos_t, sin_t) * scale).astype(BF16)
        lo, hi = window(n)
        c0 = lo - (r0 - BAND)
        s = lax.dot_general(q_st, k_scr[lo:hi, :], (((1,), (1,)), ((), ())), preferred_element_type=F32)
        return s + bias[:, c0:c0 + hi - lo]

    s_next = scores(0)
    for n in range(n_blocks):
        s = s_next
        if n + 1 < n_blocks:
            s_next = scores(n + 1)
        r0 = n * BAND
        lo, hi = window(n)
        m = jnp.maximum(jnp.max(s, axis=-1, keepdims=True), sink)
        p = jnp.exp(s - m).astype(BF16)
        oe = jnp.dot(p, v_scr[lo:hi, :], preferred_element_type=F32)
        o = oe[:, :LANE] / (oe[:, LANE:] + jnp.exp(sink - m))
        for g in range(GQA_GROUP):
            o_ref[0, r0:r0 + BAND, g * LANE:(g + 1) * LANE] = o[g * BAND:(g + 1) * BAND, :].astype(BF16)


def _gqa_attn(qkv, sinks, tabs):
    b, s, _ = qkv.shape
    gw = GQA_GROUP * GQA_HEAD_DIM
    sink_col = jnp.broadcast_to(sinks.astype(F32).reshape(GQA_KV_HEADS, GQA_GROUP, 1, 1),
                                (GQA_KV_HEADS, GQA_GROUP, BAND, 1)).reshape(GQA_KV_HEADS, GQA_GROUP * BAND, 1)
    est = (4 * _nbytes((s, gw), BF16) + 7 * _nbytes((s, LANE), BF16) + 6 * _nbytes((s, LANE), F32)
           + 8 * _nbytes((GQA_GROUP * BAND, 3 * BAND), F32))
    tab = pl.BlockSpec((s, LANE), lambda bi, h: (0, 0))
    cos_t, sin_t = tabs
    half = GQA_ROT // 2
    lane = jnp.arange(LANE)
    swap = (((lane[None, :] < half) & (lane[:, None] == lane[None, :] + half))
            | ((lane[None, :] >= half) & (lane[None, :] < 2 * half) & (lane[:, None] == lane[None, :] - half)))
    return pl.pallas_call(
        functools.partial(_gqa_kernel, scale=GQA_HEAD_DIM ** -0.5),
        name="gqa_window_attn",
        grid=(b, GQA_KV_HEADS),
        in_specs=[
            pl.BlockSpec((1, s, gw), lambda bi, h: (bi, 0, h)),
            pl.BlockSpec((1, s, LANE), lambda bi, h: (bi, 0, GQA_HEADS + h)),
            pl.BlockSpec((1, s, LANE), lambda bi, h: (bi, 0, GQA_HEADS + GQA_KV_HEADS + h)),
            pl.BlockSpec((1, GQA_GROUP * BAND, 1), lambda bi, h: (h, 0, 0)),
            tab, tab,
            pl.BlockSpec((LANE, LANE), lambda bi, h: (0, 0)),
        ],
        out_specs=pl.BlockSpec((1, s, gw), lambda bi, h: (bi, 0, h)),
        out_shape=jax.ShapeDtypeStruct((b, s, GQA_HEADS * GQA_HEAD_DIM), BF16),
        scratch_shapes=[pltpu.VMEM((s, LANE), BF16), pltpu.VMEM((s, 2 * LANE), BF16)],
        compiler_params=_cparams(("parallel", "parallel"), est),
    )(qkv, qkv, qkv, sink_col, cos_t, sin_t, swap.astype(BF16))


def _proj_short_conv_kernel(h_ref, w_ref, cw_ref, cb_ref, o_ref, *, sub_cols):
    tn = o_ref.shape[-1]
    h = h_ref[0]

    def project(c0):
        return lax.dot_general(h, w_ref[0, c0:c0 + sub_cols, :].astype(BF16), _NT_DIMS, preferred_element_type=F32)

    x_next = project(0)
    for c0 in range(0, tn, sub_cols):
        x = x_next
        if c0 + sub_cols < tn:
            x_next = project(c0 + sub_cols)
        cols = slice(c0, c0 + sub_cols)
        s_len = x.shape[0]
        row = lax.broadcasted_iota(jnp.int32, x.shape, 0)
        prev = jnp.where(row == 0, 0.0, pltpu.roll(x, 1, 0))
        nxt = jnp.where(row == s_len - 1, 0.0, pltpu.roll(x, s_len - 1, 0))
        y = cb_ref[:, cols] + prev * cw_ref[0:1, cols]
        y = y + x * cw_ref[1:2, cols]
        y = y + nxt * cw_ref[2:3, cols]
        o_ref[0, :, cols] = y.astype(BF16)


def _proj_short_conv(h, wt_all, layer, row0, n, cw, cb, tn=1024, sub_cols=512):
    b, s, d = h.shape
    assert row0 % 8 == 0 and n % tn == 0
    est = (2 * _nbytes((s, d), BF16) + 2 * _nbytes((d, tn), F32) + 2 * _nbytes((d, sub_cols), BF16)
           + 2 * _nbytes((s, tn), BF16) + 8 * _nbytes((s, sub_cols), F32))
    return pl.pallas_call(
        functools.partial(_proj_short_conv_kernel, sub_cols=sub_cols),
        name="hyena_proj_short_conv",
        grid=(b, n // tn),
        in_specs=[
            pl.BlockSpec((1, s, d), lambda bi, j: (bi, 0, 0)),
            pl.BlockSpec((pl.Element(1), pl.Element(tn), pl.Element(d)),
                         lambda bi, j: (layer, pl.multiple_of(row0 + j * tn, 8), 0)),
            pl.BlockSpec((HY_SHORT, tn), lambda bi, j: (0, j)),
            pl.BlockSpec((1, tn), lambda bi, j: (0, j)),
        ],
        out_specs=pl.BlockSpec((1, s, tn), lambda bi, j: (bi, 0, j)),
        out_shape=jax.ShapeDtypeStruct((b, s, n), BF16),
        compiler_params=_cparams(("parallel", "parallel"), est),
    )(h, wt_all, cw, cb.reshape(1, n))


def _filter_mlp_kernel(z_ref, w1_ref, b1_ref, w2_ref, b2_ref, w3_ref, b3_ref, fr_ref, h_ref):
    dot = functools.partial(jnp.dot, preferred_element_type=F32, precision=HIGHEST)
    fr = fr_ref[...]
    h = jnp.sin(fr * (dot(z_ref[...], w1_ref[...]) + b1_ref[...]))
    h = jnp.sin(fr * (dot(h, w2_ref[...]) + b2_ref[...]))
    h_ref[...] = jnp.sin(fr * (dot(h, w3_ref[...]) + b3_ref[...]))


def _filter_mlp(z, w1, b1, w2, b2, w3, b3, freq):
    n_lag = z.shape[0]
    full = lambda shape: pl.BlockSpec(shape, lambda i: (0,) * len(shape))
    est = 12 * _nbytes((n_lag, LANE), F32)
    return pl.pallas_call(
        _filter_mlp_kernel,
        name="hyena_filter_mlp",
        grid=(1,),
        in_specs=[full(z.shape), full(w1.shape), full((1, HY_FFN)), full(w2.shape), full((1, HY_FFN)),
                  full(w3.shape), full((1, HY_FFN)), full((1, HY_FFN))],
        out_specs=full((n_lag, HY_FFN)),
        out_shape=jax.ShapeDtypeStruct((n_lag, HY_FFN), F32),
        compiler_params=_cparams(("arbitrary",), est),
    )(z, w1, b1.reshape(1, -1), w2, b2.reshape(1, -1), w3, b3.reshape(1, -1), freq.reshape(1, -1))


def _filter_gen_kernel(h_ref, w4f_ref, w4b_ref, ksum_ref, kdiff_ref, nyq_ref):
    h = h_ref[...].astype(BF16)
    hf = jnp.dot(h, w4f_ref[...].astype(BF16), preferred_element_type=F32)
    hb = jnp.dot(h, w4b_ref[...].astype(BF16), preferred_element_type=F32)
    n_lag, tn = hf.shape
    row = lax.broadcasted_iota(jnp.int32, (n_lag, tn), 0)
    col = pl.program_id(1) * tn + lax.broadcasted_iota(jnp.int32, (1, tn), 1)
    t = row.astype(F32) / (n_lag - 1)
    max_decay = math.log(HY_TARGET) / HY_DECAY_PCT_SHORT
    min_decay = math.log(HY_TARGET) / HY_DECAY_PCT_LONG
    delta = min_decay + (max_decay - min_decay) * (col.astype(F32) / (HY_D - 1))
    decay = jnp.exp(-t * jnp.abs(delta))
    kf = hf * decay
    kb = jnp.where(row == 0, 0.0, hb * decay)
    inv = 1.0 / (jnp.sum(jnp.abs(kf), axis=0, keepdims=True) + jnp.sum(jnp.abs(kb), axis=0, keepdims=True))
    ksum = (kf + kb) * inv
    ksum_ref[0] = ksum.astype(BF16)
    kdiff_ref[0] = ((kb - kf) * inv).astype(BF16)
    alt = (1 - 2 * (row & 1)).astype(F32)
    nyq_ref[0] = jnp.sum(ksum * alt, axis=0, keepdims=True) * (1.0 / DFT_N)


def _filter_gen(h, w4, tn=256):
    n_lag = h.shape[0]
    nblk = HY_D // tn
    est = 10 * _nbytes((n_lag, tn), F32) + 4 * _nbytes((n_lag, LANE), F32) + 4 * _nbytes((n_lag, tn), BF16)
    out = jax.ShapeDtypeStruct((HY_ORDER, n_lag, HY_D), BF16)
    return pl.pallas_call(
        _filter_gen_kernel,
        name="hyena_filter_gen",
        grid=(HY_ORDER, nblk),
        in_specs=[
            pl.BlockSpec((n_lag, HY_FFN), lambda o, j: (0, 0)),
            pl.BlockSpec((HY_FFN, tn), lambda o, j: (0, 2 * o * nblk + j)),
            pl.BlockSpec((HY_FFN, tn), lambda o, j: (0, (2 * o + 1) * nblk + j)),
        ],
        out_specs=[pl.BlockSpec((1, n_lag, tn), lambda o, j: (o, 0, j))] * 2
        + [pl.BlockSpec((1, 1, tn), lambda o, j: (o, 0, j))],
        out_shape=[out, out, jax.ShapeDtypeStruct((HY_ORDER, 1, HY_D), F32)],
        compiler_params=_cparams(("parallel", "parallel"), est),
    )(h, w4, w4)


def _filter_dft_kernel(ct_ref, st_ref, ksum_ref, kdiff_ref, a_ref, bq_ref):
    re = jnp.dot(ct_ref[...], ksum_ref[0], preferred_element_type=F32)
    im = jnp.dot(st_ref[...], kdiff_ref[0], preferred_element_type=F32)
    tf = re.shape[0]
    row = pl.program_id(2) * tf + lax.broadcasted_iota(jnp.int32, re.shape, 0)
    is0 = row == 0
    wgt = jnp.where(is0, 1.0 / DFT_N, 2.0 / DFT_N)
    a_ref[0] = (re * wgt).astype(BF16)
    bq_ref[0] = jnp.where(is0, 0.0, im * wgt).astype(BF16)


def _filter_dft(ct, st, ksum, kdiff, tf=1024, tn=512):
    n_f, n_s = ct.shape
    est = (4 * _nbytes((tf, n_s), BF16) + 4 * _nbytes((n_s, tn), BF16) + 4 * _nbytes((tf, tn), BF16)
           + 8 * _nbytes((tf, tn), F32))
    out = jax.ShapeDtypeStruct((HY_ORDER, n_f, HY_D), BF16)
    return pl.pallas_call(
        _filter_dft_kernel,
        name="hyena_filter_dft",
        grid=(HY_ORDER, HY_D // tn, n_f // tf),
        in_specs=[
            pl.BlockSpec((tf, n_s), lambda o, j, k: (k, 0)),
            pl.BlockSpec((tf, n_s), lambda o, j, k: (k, 0)),
            pl.BlockSpec((1, n_s, tn), lambda o, j, k: (o, 0, j)),
            pl.BlockSpec((1, n_s, tn), lambda o, j, k: (o, 0, j)),
        ],
        out_specs=[pl.BlockSpec((1, tf, tn), lambda o, j, k: (o, k, j))] * 2,
        out_shape=[out, out],
        compiler_params=_cparams(("parallel", "parallel", "parallel"), est),
    )(ct, st, ksum, kdiff)


def _alt_sign(idx):
    return (1 - 2 * (idx & 1)).astype(F32)


def _long_conv_kernel(ce_ref, se_ref, co_ref, so_ref, cot_ref, sot_ref, flip_ref, u_ref, g_ref,
                      ae_ref, bqe_ref, ao_ref, bqo_ref, knyq_ref, skip_ref, o_ref,
                      e_scr, o_scr, yre_scr, yse_scr, yro_scr, yso_scr, d_scr):
    hh = e_scr.shape[0]
    sub = CONV_SUB_ROWS
    dot = functools.partial(jnp.dot, preferred_element_type=F32)
    alt_h = _alt_sign(lax.broadcasted_iota(jnp.int32, (hh, 1), 0))

    u_lo = u_ref[0, :hh, :].astype(F32)
    u_rev = dot(flip_ref[...], u_ref[0, hh:, :])
    e32 = u_lo + u_rev
    e_scr[...] = e32.astype(BF16)
    o_scr[...] = (u_lo - u_rev).astype(BF16)
    mid = u_ref[0, hh:hh + 16, :].astype(F32)[0:1]
    nyq_u = jnp.sum(e32 * alt_h, axis=0, keepdims=True) + mid
    e, o = e_scr[...], o_scr[...]

    for r0 in range(0, hh, sub):
        rs = slice(r0, r0 + sub)
        m = r0 + lax.broadcasted_iota(jnp.int32, (sub, 1), 0)
        corr = _alt_sign(m) * mid
        uc = dot(ce_ref[rs, :], e) + corr
        us = dot(se_ref[rs, :], o)
        a, bq = ae_ref[0, rs, :].astype(F32), bqe_ref[0, rs, :].astype(F32)
        yre_scr[rs, :] = (uc * a + us * bq).astype(BF16)
        ys = us * a - uc * bq
        if r0 == 0:
            ys = jnp.where(m == 0, nyq_u * knyq_ref[0], ys)
        yse_scr[rs, :] = ys.astype(BF16)
        uc = dot(co_ref[rs, :], o)
        us = dot(so_ref[rs, :], e) + corr
        a, bq = ao_ref[0, rs, :].astype(F32), bqo_ref[0, rs, :].astype(F32)
        yro_scr[rs, :] = (uc * a + us * bq).astype(BF16)
        yso_scr[rs, :] = (us * a - uc * bq).astype(BF16)

    yre, yse, yro, yso = yre_scr[...], yse_scr[...], yro_scr[...], yso_scr[...]
    nyq = yse_scr[0:16, :].astype(F32)[0:1]
    y_mid = jnp.sum((yre.astype(F32) + yso.astype(F32)) * alt_h, axis=0, keepdims=True) + nyq
    skip = skip_ref[0]
    for r0 in range(0, hh, sub):
        rs = slice(r0, r0 + sub)
        t = r0 + lax.broadcasted_iota(jnp.int32, (sub, 1), 0)
        pa = dot(ce_ref[rs, :], yre) + dot(sot_ref[rs, :], yso) + _alt_sign(t) * nyq
        pb = dot(se_ref[rs, :], yse) + dot(cot_ref[rs, :], yro)
        u_t = u_ref[0, rs, :].astype(F32)
        o_ref[0, rs, :] = (g_ref[0, rs, :].astype(F32) * (pa + pb + u_t * skip)).astype(BF16)
        d_scr[rs, :] = (pa - pb).astype(BF16)
    y_hi = dot(flip_ref[...], d_scr[...])
    y_hi = jnp.where(lax.broadcasted_iota(jnp.int32, (hh, 1), 0) == 0, y_mid, y_hi)
    u_hi = u_ref[0, hh:, :].astype(F32)
    o_ref[0, hh:, :] = (g_ref[0, hh:, :].astype(F32) * (y_hi + u_hi * skip)).astype(BF16)


def _long_conv(tabs, spec, order, usrc, u_blk0, gsrc, g_blk0, skip, tn=512):
    a, bq, knyq = spec
    b, s, _ = usrc.shape
    hh = s // 2
    nblk = HY_D // tn
    est = (7 * _nbytes((hh, hh), BF16) + 10 * _nbytes((s, tn), BF16) + 8 * _nbytes((hh, tn), BF16)
           + 7 * _nbytes((hh, tn), BF16) + 8 * _nbytes((hh, tn), F32) + 12 * _nbytes((CONV_SUB_ROWS, tn), F32))
    tab = pl.BlockSpec((hh, hh), lambda bi, j: (0, 0), pipeline_mode=pl.Buffered(1))
    even = pl.BlockSpec((1, hh, tn), lambda bi, j: (order, 0, j))
    odd = pl.BlockSpec((1, hh, tn), lambda bi, j: (order, 1, j))
    row = pl.BlockSpec((1, 1, tn), lambda bi, j: (order, 0, j))
    return pl.pallas_call(
        _long_conv_kernel,
        name="hyena_long_conv",
        grid=(b, nblk),
        in_specs=[
            tab, tab, tab, tab, tab, tab, tab,
            pl.BlockSpec((1, s, tn), lambda bi, j: (bi, 0, u_blk0 * nblk + j)),
            pl.BlockSpec((1, s, tn), lambda bi, j: (bi, 0, g_blk0 * nblk + j)),
            even, even, odd, odd, row, row,
        ],
        out_specs=pl.BlockSpec((1, s, tn), lambda bi, j: (bi, 0, j)),
        out_shape=jax.ShapeDtypeStruct((b, s, HY_D), BF16),
        scratch_shapes=[pltpu.VMEM((hh, tn), BF16)] * 7,
        compiler_params=_cparams(("parallel", "parallel"), est),
    )(tabs["ce"], tabs["se"], tabs["co"], tabs["so"], tabs["cot"], tabs["sot"], tabs["flip"], usrc, gsrc,
      a, bq, a, bq, knyq, skip.reshape(HY_ORDER, 1, HY_D))


def _rope_tables(seq, rot_dim):
    half = rot_dim // 2
    pos = jnp.arange(seq, dtype=F32)
    inv = ROPE_THETA ** (-jnp.arange(0, rot_dim, 2, dtype=F32) / rot_dim)
    ang = pos[:, None] * inv[None, :]
    cos, sin = jnp.cos(ang), jnp.sin(ang)
    rest = LANE - rot_dim
    return cos, sin, half, rest


def _rope_lane_tables(seq, rot_dim, rest_passthrough):
    cos, sin, half, rest = _rope_tables(seq, rot_dim)
    fill = jnp.ones((seq, rest), F32) if rest_passthrough else jnp.zeros((seq, rest), F32)
    zero_h = jnp.zeros((seq, half), F32)
    zero_r = jnp.zeros((seq, rest), F32)
    cos_t = jnp.concatenate([cos, cos, fill], axis=1)
    sin_up = jnp.concatenate([zero_h, sin, zero_r], axis=1)
    sin_dn = jnp.concatenate([-sin, zero_h, zero_r], axis=1)
    return cos_t, sin_up, sin_dn


def _dft_tables(n_half):
    hh = n_half // 2
    blk = 32
    idx = jnp.arange(n_half, dtype=jnp.int32)
    unit = 2.0 * math.pi / n_half
    ang_a = ((blk * idx[:hh // blk, None] * idx[None, :]) & (n_half - 1)).astype(F32) * unit
    ang_b = ((idx[:blk, None] * idx[None, :]) & (n_half - 1)).astype(F32) * unit
    ca, sa = jnp.cos(ang_a)[:, None, :], jnp.sin(ang_a)[:, None, :]
    cb, sb = jnp.cos(ang_b)[None], jnp.sin(ang_b)[None]
    ce_f = (ca * cb - sa * sb).reshape(hh, n_half)
    se_f = (sa * cb + ca * sb).reshape(hh, n_half)
    turn = idx.astype(F32) * (0.5 * unit)
    c_row, s_row = jnp.cos(turn)[None, :], jnp.sin(turn)[None, :]
    co_f = ce_f * c_row - se_f * s_row
    so_f = se_f * c_row + ce_f * s_row
    alt = (1 - 2 * (idx & 1)).astype(F32)
    ce, se = ce_f[:, :hh], se_f[:, :hh]
    c_col, s_col = jnp.cos(turn[:hh])[:, None], jnp.sin(turn[:hh])[:, None]
    flip = (idx[:hh, None] >= 1) & (idx[None, :hh] == hh - idx[:hh, None])
    tabs = dict(
        ct_eo=jnp.concatenate([ce_f, co_f], axis=0),
        st_eo=jnp.concatenate([jnp.where(idx[:hh, None] == 0, alt[None, :], se_f), so_f], axis=0),
        ce=ce, se=se, co=co_f[:, :hh], so=so_f[:, :hh],
        cot=ce * c_col - se * s_col, sot=se * c_col + ce * s_col, flip=flip)
    return {name: t.astype(BF16) for name, t in tabs.items()}


def _filter_features(n_lag):
    t = jnp.linspace(0.0, 1.0, n_lag, dtype=F32)[:, None]
    bands = (HY_EMB - 1) // 2
    wpos = 2.0 * math.pi * jnp.arange(n_lag, dtype=F32) / n_lag
    fb = jnp.linspace(1e-4, bands - 1, bands, dtype=F32)
    fw = wpos[:, None] * fb[None, :]
    z = jnp.concatenate([t, jnp.cos(fw), -jnp.sin(fw)], axis=-1)
    return jnp.pad(z, ((0, 0), (0, LANE - HY_EMB)))


def kernel(x, c, ada_mix_w, ada_mix_b, norm_mix_g, ada_mlp_w, ada_mlp_b, norm_mlp_g, w_mlp_in, w_mlp_out, e_w_in, e_q_norm_g, e_kv_norm_g, e_w_uq, e_w_ukv, e_conv_w, e_conv_b, e_f_w1, e_f_b1, e_f_w2, e_f_b2, e_f_w3, e_f_b3, e_f_freq, e_f_w4, e_hy_skip, e_w_out, o_w_qkv, o_sinks, o_w_o, final_norm_g):
    b, s, d = x.shape
    c8 = jnp.pad(c, ((0, 8 - b), (0, 0)))
    mod_mix = _ada(c8, ada_mix_w, ada_mix_b)[:, :b].reshape(DEPTH, b, 1, 3 * d)
    mod_mlp = _ada(c8, ada_mlp_w, ada_mlp_b)[:, :b].reshape(DEPTH, b, 1, 3 * d)

    mla_tabs = _rope_lane_tables(s, MLA_ROPE, rest_passthrough=False)
    gqa_cos, gqa_sin_up, gqa_sin_dn = _rope_lane_tables(s, GQA_ROT, rest_passthrough=True)
    gqa_tabs = (gqa_cos, gqa_sin_up + gqa_sin_dn)
    dft = _dft_tables(s)
    z_feat = _filter_features(s)
    q_scale = (MLA_NOPE + MLA_ROPE) ** -0.5 * math.log2(math.e)
    w_in_t = jnp.swapaxes(e_w_in, 1, 2)
    hy_row0 = Q_LORA + KV_LORA + MLA_ROPE

    w_uq = e_w_uq.reshape(-1, Q_LORA, MLA_HEADS, MLA_NOPE + MLA_ROPE)
    wq_all = jnp.concatenate([
        w_uq[..., :MLA_NOPE].reshape(-1, Q_LORA, MLA_HEADS * MLA_NOPE),
        jnp.pad(w_uq[..., MLA_NOPE:], ((0, 0), (0, 0), (0, 0), (0, LANE - MLA_ROPE))).reshape(
            -1, Q_LORA, MLA_HEADS * LANE),
    ], axis=2).astype(BF16)
    w_ukv = e_w_ukv.reshape(-1, KV_LORA, MLA_HEADS, MLA_NOPE + MLA_V)
    wk_all = w_ukv[..., :MLA_NOPE].reshape(-1, KV_LORA, MLA_HEADS * MLA_NOPE).astype(BF16)
    wv_all = w_ukv[..., MLA_NOPE:].reshape(-1, KV_LORA, MLA_HEADS * MLA_V).astype(BF16)

    for l in range(DEPTH):
        i = l // 2
        if l % 2 == 0:
            h = _normmod(x, mod_mix[l], norm_mix_g[l])
            u = _proj_short_conv(h, w_in_t, i, hy_row0, 3 * HY_D, e_conv_w[i], e_conv_b[i])

            q, k, v = _mla_proj(h, w_in_t, e_q_norm_g[i], e_kv_norm_g[i], wq_all, wk_all, wv_all, i, mla_tabs,
                                q_scale)
            a_mla = _mla_attn(q, k, v)

            w1 = jnp.pad(e_f_w1[i], ((0, LANE - HY_EMB), (0, 0)))
            h_filt = _filter_mlp(z_feat, w1, e_f_b1[i], e_f_w2[i], e_f_b2[i], e_f_w3[i], e_f_b3[i], e_f_freq[i])
            ksum, kdiff, knyq = _filter_gen(h_filt, e_f_w4[i])
            spec = (*_filter_dft(dft["ct_eo"], dft["st_eo"], ksum, kdiff), knyq)
            zc = _long_conv(dft, spec, 0, u, 0, u, 1, e_hy_skip[i])
            b_hy = _long_conv(dft, spec, 1, zc, 0, u, 2, e_hy_skip[i])

            x = _proj_res([a_mla, b_hy], e_w_out, i, x, mod_mix[l])
        else:
            h = _normmod(x, mod_mix[l], norm_mix_g[l])
            qkv = _matmul(h, o_w_qkv, i, BF16)
            o = _gqa_attn(qkv, o_sinks[i], gqa_tabs)
            x = _proj_res([o], o_w_o, i, x, mod_mix[l])
        x = _mlp(x, mod_mlp[l], norm_mlp_g[l], w_mlp_in, w_mlp_out, l, final_norm_g, l == DEPTH - 1)
    return x
```

```python
import functools
import math

import jax
import jax.numpy as jnp
from jax import lax
from jax.experimental import pallas as pl
from jax.experimental.pallas import tpu as pltpu

F32 = jnp.float32
BF16 = jnp.bfloat16
HIGHEST = lax.Precision.HIGHEST

D_MODEL = 2048
SEQ = 2048
DEPTH = 4
RMS_EPS = 1e-6
ROPE_THETA = 500000.0
NEG_INF = -1e30
MLA_HEADS = 8
MLA_NOPE = 128
MLA_ROPE = 64
MLA_V = 128
Q_LORA = 512
KV_LORA = 256
HY_D = 1024
HY_ORDER = 2
HY_SHORT = 3
HY_EMB = 33
HY_FFN = 64
HY_DECAY_PCT_SHORT = 0.3
HY_DECAY_PCT_LONG = 1.5
HY_TARGET = 1e-2
GQA_HEADS = 16
GQA_KV_HEADS = 4
GQA_HEAD_DIM = 128
GQA_ROT = GQA_HEAD_DIM // 4
GQA_GROUP = GQA_HEADS // GQA_KV_HEADS
WINDOW = 128
BAND = 128
LAT_W = Q_LORA + KV_LORA + 128
MLA_QK = 256
DFT_N = 2 * SEQ
CONV_SUB_ROWS = 256

LANE = 128
V7X_VMEM_BYTES = 64 * 1024 * 1024
V7X_VMEM_BUDGET = V7X_VMEM_BYTES * 7 // 8
ROW_CHUNK = 128


def _cparams(semantics, est_bytes):
    limit = int(min(V7X_VMEM_BUDGET, max(V7X_VMEM_BYTES // 2, est_bytes * 3 // 2)))
    return pltpu.CompilerParams(dimension_semantics=semantics, vmem_limit_bytes=limit)


def _nbytes(shape, dtype):
    return math.prod(shape) * jnp.dtype(dtype).itemsize


def _rms(x, g):
    ms = jnp.mean(x * x, axis=-1, keepdims=True)
    return x * lax.rsqrt(ms + RMS_EPS) * g


def _ada_kernel(c_ref, w_ref, b_ref, o_ref):
    cv = c_ref[...]
    s = cv * (1.0 / (1.0 + jnp.exp(-cv)))
    o_ref[0] = jnp.dot(s.astype(BF16), w_ref[0].astype(BF16), preferred_element_type=F32) + b_ref[0]


def _ada(c8, w, b, tn=1536):
    n_l, d, n = w.shape
    est = 2 * _nbytes((d, tn), F32) + _nbytes((d, tn), BF16) + 4 * _nbytes((8, tn), F32) + 2 * _nbytes((8, d), F32)
    return pl.pallas_call(
        _ada_kernel,
        name="ada_modulation",
        grid=(n_l, n // tn),
        in_specs=[
            pl.BlockSpec((8, d), lambda l, j: (0, 0)),
            pl.BlockSpec((1, d, tn), lambda l, j: (l, 0, j)),
            pl.BlockSpec((1, 1, tn), lambda l, j: (l, 0, j)),
        ],
        out_specs=pl.BlockSpec((1, 8, tn), lambda l, j: (l, 0, j)),
        out_shape=jax.ShapeDtypeStruct((n_l, 8, n), F32),
        compiler_params=_cparams(("parallel", "parallel"), est),
    )(c8, w, b.reshape(n_l, 1, n))


def _normmod_rows(x_ref, shift_ref, scale_ref, g_ref, h_ref, rows):
    gain = g_ref[...] * (1.0 + scale_ref[0])
    sh = shift_ref[0]
    tm = h_ref.shape[0]

    def body(r, carry):
        sl = pl.ds(pl.multiple_of(r * rows, rows), rows)
        xv = x_ref[0, sl, :]
        inv = lax.rsqrt(jnp.mean(xv * xv, axis=-1, keepdims=True) + RMS_EPS)
        h_ref[sl, :] = (xv * inv * gain + sh).astype(BF16)
        return carry

    lax.fori_loop(0, tm // rows, body, 0)


def _normmod_kernel(x_ref, shift_ref, scale_ref, g_ref, h_ref):
    _normmod_rows(x_ref, shift_ref, scale_ref, g_ref, h_ref.at[0], ROW_CHUNK)


def _normmod(x, mod, g, tm=1024):
    b, s, d = x.shape
    est = 2 * _nbytes((tm, d), F32) + 2 * _nbytes((tm, d), BF16) + 8 * _nbytes((128, d), F32)
    return pl.pallas_call(
        _normmod_kernel,
        name="normmod",
        grid=(b, s // tm),
        in_specs=[
            pl.BlockSpec((1, tm, d), lambda bi, i: (bi, i, 0)),
            pl.BlockSpec((1, 1, d), lambda bi, i: (bi, 0, 0)),
            pl.BlockSpec((1, 1, d), lambda bi, i: (bi, 0, 1)),
            pl.BlockSpec((1, d), lambda bi, i: (0, 0)),
        ],
        out_specs=pl.BlockSpec((1, tm, d), lambda bi, i: (bi, i, 0)),
        out_shape=jax.ShapeDtypeStruct((b, s, d), BF16),
        compiler_params=_cparams(("parallel", "parallel"), est),
    )(x, mod, mod, g.reshape(1, d))


_NT_DIMS = (((1,), (1,)), ((), ()))


def _matmul_kernel(h_ref, w_ref, o_ref):
    o_ref[0] = jnp.dot(h_ref[0], w_ref[...].astype(BF16), preferred_element_type=F32).astype(o_ref.dtype)


def _matmul(h, w_all, layer, out_dtype, tm=2048, tn=1024):
    b, s, d = h.shape
    n = w_all.shape[2]
    est = (2 * _nbytes((tm, d), BF16) + 2 * _nbytes((d, tn), F32) + _nbytes((d, tn), BF16)
           + 3 * _nbytes((tm, tn), F32))
    return pl.pallas_call(
        _matmul_kernel,
        name="matmul",
        grid=(b, s // tm, n // tn),
        in_specs=[
            pl.BlockSpec((1, tm, d), lambda bi, i, j: (bi, i, 0)),
            pl.BlockSpec((None, d, tn), lambda bi, i, j: (layer, 0, j)),
        ],
        out_specs=pl.BlockSpec((1, tm, tn), lambda bi, i, j: (bi, i, j)),
        out_shape=jax.ShapeDtypeStruct((b, s, n), out_dtype),
        compiler_params=_cparams(("parallel", "parallel", "parallel"), est),
    )(h, w_all)


def _mlp_kernel(x_ref, shift_ref, scale_ref, gate_ref, g_ref, w1_ref, w2_ref, fg_ref, o_ref, h_ref, *, final_norm):
    f = pl.program_id(2)

    @pl.when(f == 0)
    def _():
        _normmod_rows(x_ref, shift_ref, scale_ref, g_ref, h_ref, ROW_CHUNK)

        o_ref[...] = jnp.zeros_like(o_ref)

    a = jnp.dot(h_ref[...], w1_ref[...].astype(BF16), preferred_element_type=F32)
    a = jnp.square(jnp.maximum(a, 0.0)).astype(BF16)
    o_ref[0] += jnp.dot(a, w2_ref[...].astype(BF16), preferred_element_type=F32)

    @pl.when(f == pl.num_programs(2) - 1)
    def _():
        gate = gate_ref[0]
        fg = fg_ref[...]
        rows = ROW_CHUNK

        def body(r, carry):
            sl = pl.ds(pl.multiple_of(r * rows, rows), rows)
            y = x_ref[0, sl, :] + gate * o_ref[0, sl, :]
            if final_norm:
                y = _rms(y, fg)
            o_ref[0, sl, :] = y
            return carry

        lax.fori_loop(0, o_ref.shape[1] // rows, body, 0)


def _mlp(x, mod, g, w1_all, w2_all, layer, final_g, final_norm, tm=1024, tf=512):
    b, s, d = x.shape
    ff = w1_all.shape[2]
    est = (4 * _nbytes((tm, d), F32) + _nbytes((tm, d), BF16) + 4 * _nbytes((d, tf), F32)
           + 2 * _nbytes((d, tf), BF16) + 2 * _nbytes((tm, tf), F32))
    return pl.pallas_call(
        functools.partial(_mlp_kernel, final_norm=final_norm),
        name="mlp_relu2",
        grid=(b, s // tm, ff // tf),
        in_specs=[
            pl.BlockSpec((1, tm, d), lambda bi, i, f: (bi, i, 0)),
            pl.BlockSpec((1, 1, d), lambda bi, i, f: (bi, 0, 0)),
            pl.BlockSpec((1, 1, d), lambda bi, i, f: (bi, 0, 1)),
            pl.BlockSpec((1, 1, d), lambda bi, i, f: (bi, 0, 2)),
            pl.BlockSpec((1, d), lambda bi, i, f: (0, 0)),
            pl.BlockSpec((None, d, tf), lambda bi, i, f: (layer, 0, f)),
            pl.BlockSpec((None, tf, d), lambda bi, i, f: (layer, f, 0)),
            pl.BlockSpec((1, d), lambda bi, i, f: (0, 0)),
        ],
        out_specs=pl.BlockSpec((1, tm, d), lambda bi, i, f: (bi, i, 0)),
        out_shape=jax.ShapeDtypeStruct((b, s, d), F32),
        scratch_shapes=[pltpu.VMEM((tm, d), BF16)],
        compiler_params=_cparams(("parallel", "parallel", "arbitrary"), est),
    )(x, mod, mod, mod, g.reshape(1, d), w1_all, w2_all, final_g.reshape(1, d))


def _proj_res_kernel(*refs, n_in):
    a_refs = refs[:n_in]
    w_ref, x_ref, gate_ref, o_ref, wb_scr = refs[n_in:]
    rows = 2 * ROW_CHUNK

    @pl.when((pl.program_id(0) == 0) & (pl.program_id(1) == 0))
    def _():
        def body(r, carry):
            sl = pl.ds(pl.multiple_of(r * rows, rows), rows)
            wb_scr[sl, :] = w_ref[sl, :].astype(BF16)
            return carry

        lax.fori_loop(0, w_ref.shape[0] // rows, body, 0)

    kk = a_refs[0].shape[-1]
    acc = None
    for r, a_ref in enumerate(a_refs):
        p = jnp.dot(a_ref[0], wb_scr[r * kk:(r + 1) * kk, :], preferred_element_type=F32)
        acc = p if acc is None else acc + p
    o_ref[0] = x_ref[0] + gate_ref[0] * acc


def _proj_res(a_list, w_all, layer, x, mod, tm=512):
    b, s, d = x.shape
    n_in = len(a_list)
    kk = a_list[0].shape[-1]
    k_all = n_in * kk
    assert all(a.shape[-1] == kk for a in a_list) and w_all.shape[1] == k_all
    est = (_nbytes((k_all, d), F32) + _nbytes((k_all, d), BF16) + 2 * _nbytes((tm, k_all), BF16)
           + 6 * _nbytes((tm, d), F32))
    in_specs = [pl.BlockSpec((1, tm, kk), lambda bi, i: (bi, i, 0)) for _ in a_list]
    in_specs += [
        pl.BlockSpec((None, k_all, d), lambda bi, i: (layer, 0, 0), pipeline_mode=pl.Buffered(1)),
        pl.BlockSpec((1, tm, d), lambda bi, i: (bi, i, 0)),
        pl.BlockSpec((1, 1, d), lambda bi, i: (bi, 0, 2)),
    ]
    return pl.pallas_call(
        functools.partial(_proj_res_kernel, n_in=n_in),
        name="proj_residual",
        grid=(b, s // tm),
        in_specs=in_specs,
        out_specs=pl.BlockSpec((1, tm, d), lambda bi, i: (bi, i, 0)),
        out_shape=jax.ShapeDtypeStruct((b, s, d), F32),
        scratch_shapes=[pltpu.VMEM((k_all, d), BF16)],
        compiler_params=_cparams(("arbitrary", "arbitrary"), est),
    )(*a_list, w_all, x, mod)


def _rot_lanes(blk, cos_t, sin_up, sin_dn, half):
    return (blk * cos_t + pltpu.roll(blk, half, 1) * sin_up + pltpu.roll(blk, LANE - half, 1) * sin_dn)


def _mla_proj_kernel(h_ref, wlat_ref, gq_ref, gkv_ref, wq_ref, wk_ref, wv_ref, cos_ref, sup_ref, sdn_ref,
                     q_ref, k_ref, v_ref, *, q_scale):
    lat = lax.dot_general(h_ref[0], wlat_ref[...].astype(BF16), _NT_DIMS, preferred_element_type=F32)
    qn = _rms(lat[:, :Q_LORA], gq_ref[...]).astype(BF16)
    kvn = _rms(lat[:, Q_LORA:Q_LORA + KV_LORA], gkv_ref[...]).astype(BF16)
    cos_t, sin_up, sin_dn = cos_ref[...], sup_ref[...], sdn_ref[...]
    half = MLA_ROPE // 2
    q = jnp.dot(qn, wq_ref[...], preferred_element_type=F32) * q_scale
    kn = jnp.dot(kvn, wk_ref[...], preferred_element_type=F32)
    v = jnp.dot(kvn, wv_ref[...], preferred_element_type=F32).astype(BF16)
    ones = jnp.ones((v.shape[0], MLA_V), BF16)
    kr = _rot_lanes(lat[:, Q_LORA + KV_LORA:], cos_t, sin_up, sin_dn, half).astype(BF16)
    nope_w = MLA_HEADS * MLA_NOPE
    for h in range(MLA_HEADS):
        c0 = h * MLA_QK
        v_ref[0, :, 2 * h * MLA_V:(2 * h + 1) * MLA_V] = v[:, h * MLA_V:(h + 1) * MLA_V]
        v_ref[0, :, (2 * h + 1) * MLA_V:(2 * h + 2) * MLA_V] = ones
        q_ref[0, :, c0:c0 + LANE] = q[:, h * LANE:(h + 1) * LANE].astype(BF16)
        qr = q[:, nope_w + h * LANE:nope_w + (h + 1) * LANE]
        q_ref[0, :, c0 + LANE:c0 + 2 * LANE] = _rot_lanes(qr, cos_t, sin_up, sin_dn, half).astype(BF16)
        k_ref[0, :, c0:c0 + LANE] = kn[:, h * LANE:(h + 1) * LANE].astype(BF16)
        k_ref[0, :, c0 + LANE:c0 + 2 * LANE] = kr


def _mla_proj(h, w_in_t, gq, gkv, wq_all, wk_all, wv_all, layer, tabs, q_scale, tm=512):
    b, s, d = h.shape
    hq = MLA_HEADS * MLA_QK
    hv = MLA_HEADS * 2 * MLA_V
    est = (2 * _nbytes((tm, d), BF16) + _nbytes((LAT_W, d), F32) + _nbytes((LAT_W, d), BF16)
           + 2 * _nbytes((tm, LAT_W), F32) + 2 * _nbytes(wq_all.shape[1:], BF16) + 2 * _nbytes(wk_all.shape[1:], BF16)
           + 2 * _nbytes(wv_all.shape[1:], BF16) + 4 * _nbytes((tm, hq), BF16) + 2 * _nbytes((tm, hv), BF16)
           + 3 * _nbytes((tm, hq), F32))
    full = lambda shape: pl.BlockSpec(shape, lambda bi, i: (0,) * len(shape))
    stacked = lambda w: pl.BlockSpec((None,) + w.shape[1:], lambda bi, i: (layer, 0, 0))
    tab = pl.BlockSpec((tm, LANE), lambda bi, i: (i, 0))
    return pl.pallas_call(
        functools.partial(_mla_proj_kernel, q_scale=q_scale),
        name="mla_proj",
        grid=(b, s // tm),
        in_specs=[
            pl.BlockSpec((1, tm, d), lambda bi, i: (bi, i, 0)),
            pl.BlockSpec((None, LAT_W, d), lambda bi, i: (layer, 0, 0), pipeline_mode=pl.Buffered(1)),
            full((1, Q_LORA)), full((1, KV_LORA)), stacked(wq_all), stacked(wk_all), stacked(wv_all),
            tab, tab, tab,
        ],
        out_specs=[
            pl.BlockSpec((1, tm, hq), lambda bi, i: (bi, i, 0)),
            pl.BlockSpec((1, tm, hq), lambda bi, i: (bi, i, 0)),
            pl.BlockSpec((1, tm, hv), lambda bi, i: (bi, i, 0)),
        ],
        out_shape=[
            jax.ShapeDtypeStruct((b, s, hq), BF16),
            jax.ShapeDtypeStruct((b, s, hq), BF16),
            jax.ShapeDtypeStruct((b, s, hv), BF16),
        ],
        compiler_params=_cparams(("parallel", "parallel"), est),
    )(h, w_in_t, gq.reshape(1, -1), gkv.reshape(1, -1), wq_all, wk_all, wv_all, *tabs)


def _mla_attn_kernel(q_ref, k_ref, v_ref, o_ref, *, chain_rows):
    n_chains = q_ref.shape[1] // chain_rows

    def scores(c):
        rs = slice(c * chain_rows, (c + 1) * chain_rows)
        return lax.dot_general(q_ref[0, rs, :], k_ref[0], (((1,), (1,)), ((), ())), preferred_element_type=F32)

    s_next = scores(0)
    for c in range(n_chains):
        s = s_next
        if c + 1 < n_chains:
            s_next = scores(c + 1)
        m = jnp.max(s, axis=-1, keepdims=True)
        p = jnp.exp2(s - m).astype(BF16)
        oe = jnp.dot(p, v_ref[0], preferred_element_type=F32)
        o_ref[0, c * chain_rows:(c + 1) * chain_rows, :] = (oe[:, :MLA_V] / oe[:, MLA_V:]).astype(BF16)


def _mla_attn(q, k, v, tq=2048, chain_rows=1024):
    b, s, _ = q.shape
    est = (2 * _nbytes((tq, MLA_QK), BF16) + 2 * _nbytes((s, MLA_QK), BF16) + 2 * _nbytes((s, 2 * MLA_V), BF16)
           + 2 * _nbytes((tq, MLA_V), BF16) + 5 * _nbytes((chain_rows, s), F32))
    return pl.pallas_call(
        functools.partial(_mla_attn_kernel, chain_rows=chain_rows),
        name="mla_attn",
        grid=(b, MLA_HEADS, s // tq),
        in_specs=[
            pl.BlockSpec((1, tq, MLA_QK), lambda bi, h, i: (bi, i, h)),
            pl.BlockSpec((1, s, MLA_QK), lambda bi, h, i: (bi, 0, h)),
            pl.BlockSpec((1, s, 2 * MLA_V), lambda bi, h, i: (bi, 0, h)),
        ],
        out_specs=pl.BlockSpec((1, tq, MLA_V), lambda bi, h, i: (bi, i, h)),
        out_shape=jax.ShapeDtypeStruct((b, s, MLA_HEADS * MLA_V), BF16),
        compiler_params=_cparams(("parallel", "parallel", "parallel"), est),
    )(q, k, v)


def _gqa_kernel(q_ref, k_ref, v_ref, sink_ref, cos_ref, sin_ref, swap_ref, o_ref, k_scr, v_scr, *, scale):
    s_len = k_scr.shape[0]
    n_blocks = s_len // BAND
    rows = GQA_GROUP * BAND
    swap = swap_ref[...]

    def rot(x, cos_t, sin_t):
        return x.astype(F32) * cos_t + jnp.dot(x, swap, preferred_element_type=F32) * sin_t

    k_scr[...] = rot(k_ref[0], cos_ref[...], sin_ref[...]).astype(BF16)
    v_scr[:, :LANE] = v_ref[0]
    v_scr[:, LANE:] = jnp.ones((s_len, LANE), BF16)
    sink = sink_ref[0]
    rel = (lax.broadcasted_iota(jnp.int32, (rows, 3 * BAND), 1) - BAND
           - (lax.broadcasted_iota(jnp.int32, (rows, 3 * BAND), 0) & (BAND - 1)))
    bias = jnp.where(jnp.abs(rel) <= WINDOW, 0.0, NEG_INF)

    def window(n):
        r0 = n * BAND
        return max(0, r0 - BAND), min(s_len, r0 + 2 * BAND)

    def scores(n):
        r0 = n * BAND
        cos_t = jnp.concatenate([cos_ref[r0:r0 + BAND, :]] * GQA_GROUP, axis=0)
        sin_t = jnp.concatenate([sin_ref[r0:r0 + BAND, :]] * GQA_GROUP, axis=0)
        q_st = jnp.concatenate([q_ref[0, r0:r0 + BAND, g * LANE:(g + 1) * LANE] for g in range(GQA_GROUP)], axis=0)
        q_st = (rot(q_st, cos_t, sin_t) * scale).astype(BF16)
        lo, hi = window(n)
        c0 = lo - (r0 - BAND)
        s = lax.dot_general(q_st, k_scr[lo:hi, :], (((1,), (1,)), ((), ())), preferred_element_type=F32)
        return s + bias[:, c0:c0 + hi - lo]

    s_next = scores(0)
    for n in range(n_blocks):
        s = s_next
        if n + 1 < n_blocks:
            s_next = scores(n + 1)
        r0 = n * BAND
        lo, hi = window(n)
        m = jnp.maximum(jnp.max(s, axis=-1, keepdims=True), sink)
        p = jnp.exp(s - m).astype(BF16)
        oe = jnp.dot(p, v_scr[lo:hi, :], preferred_element_type=F32)
        o = oe[:, :LANE] / (oe[:, LANE:] + jnp.exp(sink - m))
        for g in range(GQA_GROUP):
            o_ref[0, r0:r0 + BAND, g * LANE:(g + 1) * LANE] = o[g * BAND:(g + 1) * BAND, :].astype(BF16)


def _gqa_attn(qkv, sinks, tabs):
    b, s, _ = qkv.shape
    gw = GQA_GROUP * GQA_HEAD_DIM
    sink_col = jnp.broadcast_to(sinks.astype(F32).reshape(GQA_KV_HEADS, GQA_GROUP, 1, 1),
                                (GQA_KV_HEADS, GQA_GROUP, BAND, 1)).reshape(GQA_KV_HEADS, GQA_GROUP * BAND, 1)
    est = (4 * _nbytes((s, gw), BF16) + 7 * _nbytes((s, LANE), BF16) + 6 * _nbytes((s, LANE), F32)
           + 8 * _nbytes((GQA_GROUP * BAND, 3 * BAND), F32))
    tab = pl.BlockSpec((s, LANE), lambda bi, h: (0, 0))
    cos_t, sin_t = tabs
    half = GQA_ROT // 2
    lane = jnp.arange(LANE)
    swap = (((lane[None, :] < half) & (lane[:, None] == lane[None, :] + half))
            | ((lane[None, :] >= half) & (lane[None, :] < 2 * half) & (lane[:, None] == lane[None, :] - half)))
    return pl.pallas_call(
        functools.partial(_gqa_kernel, scale=GQA_HEAD_DIM ** -0.5),
        name="gqa_window_attn",
        grid=(b, GQA_KV_HEADS),
        in_specs=[
            pl.BlockSpec((1, s, gw), lambda bi, h: (bi, 0, h)),
            pl.BlockSpec((1, s, LANE), lambda bi, h: (bi, 0, GQA_HEADS + h)),
            pl.BlockSpec((1, s, LANE), lambda bi, h: (bi, 0, GQA_HEADS + GQA_KV_HEADS + h)),
            pl.BlockSpec((1, GQA_GROUP * BAND, 1), lambda bi, h: (h, 0, 0)),
            tab, tab,
            pl.BlockSpec((LANE, LANE), lambda bi, h: (0, 0)),
        ],
        out_specs=pl.BlockSpec((1, s, gw), lambda bi, h: (bi, 0, h)),
        out_shape=jax.ShapeDtypeStruct((b, s, GQA_HEADS * GQA_HEAD_DIM), BF16),
        scratch_shapes=[pltpu.VMEM((s, LANE), BF16), pltpu.VMEM((s, 2 * LANE), BF16)],
        compiler_params=_cparams(("parallel", "parallel"), est),
    )(qkv, qkv, qkv, sink_col, cos_t, sin_t, swap.astype(BF16))


def _proj_short_conv_kernel(h_ref, w_ref, cw_ref, cb_ref, o_ref, *, sub_cols):
    tn = o_ref.shape[-1]
    h = h_ref[0]

    def project(c0):
        return lax.dot_general(h, w_ref[0, c0:c0 + sub_cols, :].astype(BF16), _NT_DIMS, preferred_element_type=F32)

    x_next = project(0)
    for c0 in range(0, tn, sub_cols):
        x = x_next
        if c0 + sub_cols < tn:
            x_next = project(c0 + sub_cols)
        cols = slice(c0, c0 + sub_cols)
        s_len = x.shape[0]
        row = lax.broadcasted_iota(jnp.int32, x.shape, 0)
        prev = jnp.where(row == 0, 0.0, pltpu.roll(x, 1, 0))
        nxt = jnp.where(row == s_len - 1, 0.0, pltpu.roll(x, s_len - 1, 0))
        y = cb_ref[:, cols] + prev * cw_ref[0:1, cols]
        y = y + x * cw_ref[1:2, cols]
        y = y + nxt * cw_ref[2:3, cols]
        o_ref[0, :, cols] = y.astype(BF16)


def _proj_short_conv(h, wt_all, layer, row0, n, cw, cb, tn=1024, sub_cols=512):
    b, s, d = h.shape
    assert row0 % 8 == 0 and n % tn == 0
    est = (2 * _nbytes((s, d), BF16) + 2 * _nbytes((d, tn), F32) + 2 * _nbytes((d, sub_cols), BF16)
           + 2 * _nbytes((s, tn), BF16) + 8 * _nbytes((s, sub_cols), F32))
    return pl.pallas_call(
        functools.partial(_proj_short_conv_kernel, sub_cols=sub_cols),
        name="hyena_proj_short_conv",
        grid=(b, n // tn),
        in_specs=[
            pl.BlockSpec((1, s, d), lambda bi, j: (bi, 0, 0)),
            pl.BlockSpec((pl.Element(1), pl.Element(tn), pl.Element(d)),
                         lambda bi, j: (layer, pl.multiple_of(row0 + j * tn, 8), 0)),
            pl.BlockSpec((HY_SHORT, tn), lambda bi, j: (0, j)),
            pl.BlockSpec((1, tn), lambda bi, j: (0, j)),
        ],
        out_specs=pl.BlockSpec((1, s, tn), lambda bi, j: (bi, 0, j)),
        out_shape=jax.ShapeDtypeStruct((b, s, n), BF16),
        compiler_params=_cparams(("parallel", "parallel"), est),
    )(h, wt_all, cw, cb.reshape(1, n))


def _filter_mlp_kernel(z_ref, w1_ref, b1_ref, w2_ref, b2_ref, w3_ref, b3_ref, fr_ref, h_ref):
    dot = functools.partial(jnp.dot, preferred_element_type=F32, precision=HIGHEST)
    fr = fr_ref[...]
    h = jnp.sin(fr * (dot(z_ref[...], w1_ref[...]) + b1_ref[...]))
    h = jnp.sin(fr * (dot(h, w2_ref[...]) + b2_ref[...]))
    h_ref[...] = jnp.sin(fr * (dot(h, w3_ref[...]) + b3_ref[...]))


def _filter_mlp(z, w1, b1, w2, b2, w3, b3, freq):
    n_lag = z.shape[0]
    full = lambda shape: pl.BlockSpec(shape, lambda i: (0,) * len(shape))
    est = 12 * _nbytes((n_lag, LANE), F32)
    return pl.pallas_call(
        _filter_mlp_kernel,
        name="hyena_filter_mlp",
        grid=(1,),
        in_specs=[full(z.shape), full(w1.shape), full((1, HY_FFN)), full(w2.shape), full((1, HY_FFN)),
                  full(w3.shape), full((1, HY_FFN)), full((1, HY_FFN))],
        out_specs=full((n_lag, HY_FFN)),
        out_shape=jax.ShapeDtypeStruct((n_lag, HY_FFN), F32),
        compiler_params=_cparams(("arbitrary",), est),
    )(z, w1, b1.reshape(1, -1), w2, b2.reshape(1, -1), w3, b3.reshape(1, -1), freq.reshape(1, -1))


def _filter_gen_kernel(h_ref, w4f_ref, w4b_ref, ksum_ref, kdiff_ref, nyq_ref):
    h = h_ref[...].astype(BF16)
    hf = jnp.dot(h, w4f_ref[...].astype(BF16), preferred_element_type=F32)
    hb = jnp.dot(h, w4b_ref[...].astype(BF16), preferred_element_type=F32)
    n_lag, tn = hf.shape
    row = lax.broadcasted_iota(jnp.int32, (n_lag, tn), 0)
    col = pl.program_id(1) * tn + lax.broadcasted_iota(jnp.int32, (1, tn), 1)
    t = row.astype(F32) / (n_lag - 1)
    max_decay = math.log(HY_TARGET) / HY_DECAY_PCT_SHORT
    min_decay = math.log(HY_TARGET) / HY_DECAY_PCT_LONG
    delta = min_decay + (max_decay - min_decay) * (col.astype(F32) / (HY_D - 1))
    decay = jnp.exp(-t * jnp.abs(delta))
    kf = hf * decay
    kb = jnp.where(row == 0, 0.0, hb * decay)
    inv = 1.0 / (jnp.sum(jnp.abs(kf), axis=0, keepdims=True) + jnp.sum(jnp.abs(kb), axis=0, keepdims=True))
    ksum = (kf + kb) * inv
    ksum_ref[0] = ksum.astype(BF16)
    kdiff_ref[0] = ((kb - kf) * inv).astype(BF16)
    alt = (1 - 2 * (row & 1)).astype(F32)
    nyq_ref[0] = jnp.sum(ksum * alt, axis=0, keepdims=True) * (1.0 / DFT_N)


def _filter_gen(h, w4, tn=256):
    n_lag = h.shape[0]
    nblk = HY_D // tn
    est = 10 * _nbytes((n_lag, tn), F32) + 4 * _nbytes((n_lag, LANE), F32) + 4 * _nbytes((n_lag, tn), BF16)
    out = jax.ShapeDtypeStruct((HY_ORDER, n_lag, HY_D), BF16)
    return pl.pallas_call(
        _filter_gen_kernel,
        name="hyena_filter_gen",
        grid=(HY_ORDER, nblk),
        in_specs=[
            pl.BlockSpec((n_lag, HY_FFN), lambda o, j: (0, 0)),
            pl.BlockSpec((HY_FFN, tn), lambda o, j: (0, 2 * o * nblk + j)),
            pl.BlockSpec((HY_FFN, tn), lambda o, j: (0, (2 * o + 1) * nblk + j)),
        ],
        out_specs=[pl.BlockSpec((1, n_lag, tn), lambda o, j: (o, 0, j))] * 2
        + [pl.BlockSpec((1, 1, tn), lambda o, j: (o, 0, j))],
        out_shape=[out, out, jax.ShapeDtypeStruct((HY_ORDER, 1, HY_D), F32)],
        compiler_params=_cparams(("parallel", "parallel"), est),
    )(h, w4, w4)


def _filter_dft_kernel(ce_ref, se_ref, co_ref, so_ref, flip_ref, ksum_ref, kdiff_ref, a_ref, bq_ref):
    hh = ce_ref.shape[0]
    dot = functools.partial(jnp.dot, preferred_element_type=F32)
    idx = lax.broadcasted_iota(jnp.int32, (hh, 1), 0)
    alt = _alt_sign(idx)

    def fold(x_ref):
        lo = x_ref[0, :hh, :].astype(F32)
        rev = dot(flip_ref[...], x_ref[0, hh:, :])
        mid = x_ref[0, hh:hh + 16, :].astype(F32)[0:1]
        return (lo + rev).astype(BF16), (lo - rev).astype(BF16), mid

    e_sum, o_sum, mid_sum = fold(ksum_ref)
    e_diff, o_diff, mid_diff = fold(kdiff_ref)
    wgt = 2.0 / DFT_N
    a_even = (dot(ce_ref[...], e_sum) + alt * mid_sum) * jnp.where(idx == 0, 1.0 / DFT_N, wgt)
    a_ref[0, :hh, :] = a_even.astype(BF16)
    a_ref[0, hh:, :] = (dot(co_ref[...], o_sum) * wgt).astype(BF16)
    bq_ref[0, :hh, :] = jnp.where(idx == 0, 0.0, dot(se_ref[...], o_diff) * wgt).astype(BF16)
    bq_ref[0, hh:, :] = ((dot(so_ref[...], e_diff) + alt * mid_diff) * wgt).astype(BF16)


def _filter_dft(tabs, ksum, kdiff, tn=512):
    _, s, _ = ksum.shape
    hh = s // 2
    est = (5 * _nbytes((hh, hh), BF16) + 8 * _nbytes((s, tn), BF16) + 4 * _nbytes((hh, tn), BF16)
           + 10 * _nbytes((hh, tn), F32))
    tab = pl.BlockSpec((hh, hh), lambda o, j: (0, 0), pipeline_mode=pl.Buffered(1))
    blk = pl.BlockSpec((1, s, tn), lambda o, j: (o, 0, j))
    out = jax.ShapeDtypeStruct((HY_ORDER, s, HY_D), BF16)
    return pl.pallas_call(
        _filter_dft_kernel,
        name="hyena_filter_dft",
        grid=(HY_ORDER, HY_D // tn),
        in_specs=[tab, tab, tab, tab, tab, blk, blk],
        out_specs=[blk, blk],
        out_shape=[out, out],
        compiler_params=_cparams(("parallel", "parallel"), est),
    )(tabs["ce"], tabs["se"], tabs["co"], tabs["so"], tabs["flip"], ksum, kdiff)


def _alt_sign(idx):
    return (1 - 2 * (idx & 1)).astype(F32)


def _long_conv_kernel(ce_ref, se_ref, co_ref, so_ref, cot_ref, sot_ref, flip_ref, u_ref, g_ref,
                      ae_ref, bqe_ref, ao_ref, bqo_ref, knyq_ref, skip_ref, o_ref,
                      e_scr, o_scr, yre_scr, yse_scr, yro_scr, yso_scr, d_scr):
    hh = e_scr.shape[0]
    sub = CONV_SUB_ROWS
    dot = functools.partial(jnp.dot, preferred_element_type=F32)
    alt_h = _alt_sign(lax.broadcasted_iota(jnp.int32, (hh, 1), 0))

    u_lo = u_ref[0, :hh, :].astype(F32)
    u_rev = dot(flip_ref[...], u_ref[0, hh:, :])
    e32 = u_lo + u_rev
    e_scr[...] = e32.astype(BF16)
    o_scr[...] = (u_lo - u_rev).astype(BF16)
    mid = u_ref[0, hh:hh + 16, :].astype(F32)[0:1]
    nyq_u = jnp.sum(e32 * alt_h, axis=0, keepdims=True) + mid
    e, o = e_scr[...], o_scr[...]

    for r0 in range(0, hh, sub):
        rs = slice(r0, r0 + sub)
        m = r0 + lax.broadcasted_iota(jnp.int32, (sub, 1), 0)
        corr = _alt_sign(m) * mid
        uc = dot(ce_ref[rs, :], e) + corr
        us = dot(se_ref[rs, :], o)
        a, bq = ae_ref[0, rs, :].astype(F32), bqe_ref[0, rs, :].astype(F32)
        yre_scr[rs, :] = (uc * a + us * bq).astype(BF16)
        ys = us * a - uc * bq
        if r0 == 0:
            ys = jnp.where(m == 0, nyq_u * knyq_ref[0], ys)
        yse_scr[rs, :] = ys.astype(BF16)
        uc = dot(co_ref[rs, :], o)
        us = dot(so_ref[rs, :], e) + corr
        a, bq = ao_ref[0, rs, :].astype(F32), bqo_ref[0, rs, :].astype(F32)
        yro_scr[rs, :] = (uc * a + us * bq).astype(BF16)
        yso_scr[rs, :] = (us * a - uc * bq).astype(BF16)

    yre, yse, yro, yso = yre_scr[...], yse_scr[...], yro_scr[...], yso_scr[...]
    nyq = yse_scr[0:16, :].astype(F32)[0:1]
    y_mid = jnp.sum((yre.astype(F32) + yso.astype(F32)) * alt_h, axis=0, keepdims=True) + nyq
    skip = skip_ref[0]
    for r0 in range(0, hh, sub):
        rs = slice(r0, r0 + sub)
        t = r0 + lax.broadcasted_iota(jnp.int32, (sub, 1), 0)
        pa = dot(ce_ref[rs, :], yre) + dot(sot_ref[rs, :], yso) + _alt_sign(t) * nyq
        pb = dot(se_ref[rs, :], yse) + dot(cot_ref[rs, :], yro)
        u_t = u_ref[0, rs, :].astype(F32)
        o_ref[0, rs, :] = (g_ref[0, rs, :].astype(F32) * (pa + pb + u_t * skip)).astype(BF16)
        d_scr[rs, :] = (pa - pb).astype(BF16)
    y_hi = dot(flip_ref[...], d_scr[...])
    y_hi = jnp.where(lax.broadcasted_iota(jnp.int32, (hh, 1), 0) == 0, y_mid, y_hi)
    u_hi = u_ref[0, hh:, :].astype(F32)
    o_ref[0, hh:, :] = (g_ref[0, hh:, :].astype(F32) * (y_hi + u_hi * skip)).astype(BF16)


def _long_conv(tabs, spec, order, usrc, u_blk0, gsrc, g_blk0, skip, tn=512):
    a, bq, knyq = spec
    b, s, _ = usrc.shape
    hh = s // 2
    nblk = HY_D // tn
    est = (7 * _nbytes((hh, hh), BF16) + 10 * _nbytes((s, tn), BF16) + 8 * _nbytes((hh, tn), BF16)
           + 7 * _nbytes((hh, tn), BF16) + 8 * _nbytes((hh, tn), F32) + 12 * _nbytes((CONV_SUB_ROWS, tn), F32))
    tab = pl.BlockSpec((hh, hh), lambda bi, j: (0, 0), pipeline_mode=pl.Buffered(1))
    even = pl.BlockSpec((1, hh, tn), lambda bi, j: (order, 0, j))
    odd = pl.BlockSpec((1, hh, tn), lambda bi, j: (order, 1, j))
    row = pl.BlockSpec((1, 1, tn), lambda bi, j: (order, 0, j))
    return pl.pallas_call(
        _long_conv_kernel,
        name="hyena_long_conv",
        grid=(b, nblk),
        in_specs=[
            tab, tab, tab, tab, tab, tab, tab,
            pl.BlockSpec((1, s, tn), lambda bi, j: (bi, 0, u_blk0 * nblk + j)),
            pl.BlockSpec((1, s, tn), lambda bi, j: (bi, 0, g_blk0 * nblk + j)),
            even, even, odd, odd, row, row,
        ],
        out_specs=pl.BlockSpec((1, s, tn), lambda bi, j: (bi, 0, j)),
        out_shape=jax.ShapeDtypeStruct((b, s, HY_D), BF16),
        scratch_shapes=[pltpu.VMEM((hh, tn), BF16)] * 7,
        compiler_params=_cparams(("parallel", "parallel"), est),
    )(tabs["ce"], tabs["se"], tabs["co"], tabs["so"], tabs["cot"], tabs["sot"], tabs["flip"], usrc, gsrc,
      a, bq, a, bq, knyq, skip.reshape(HY_ORDER, 1, HY_D))


def _rope_tables(seq, rot_dim):
    half = rot_dim // 2
    pos = jnp.arange(seq, dtype=F32)
    inv = ROPE_THETA ** (-jnp.arange(0, rot_dim, 2, dtype=F32) / rot_dim)
    ang = pos[:, None] * inv[None, :]
    cos, sin = jnp.cos(ang), jnp.sin(ang)
    rest = LANE - rot_dim
    return cos, sin, half, rest


def _rope_lane_tables(seq, rot_dim, rest_passthrough):
    cos, sin, half, rest = _rope_tables(seq, rot_dim)
    fill = jnp.ones((seq, rest), F32) if rest_passthrough else jnp.zeros((seq, rest), F32)
    zero_h = jnp.zeros((seq, half), F32)
    zero_r = jnp.zeros((seq, rest), F32)
    cos_t = jnp.concatenate([cos, cos, fill], axis=1)
    sin_up = jnp.concatenate([zero_h, sin, zero_r], axis=1)
    sin_dn = jnp.concatenate([-sin, zero_h, zero_r], axis=1)
    return cos_t, sin_up, sin_dn


def _dft_tables(n_half):
    hh = n_half // 2
    blk = 32
    idx = jnp.arange(hh, dtype=jnp.int32)
    unit = 2.0 * math.pi / n_half
    ang_a = ((blk * idx[:hh // blk, None] * idx[None, :]) & (n_half - 1)).astype(F32) * unit
    ang_b = ((idx[:blk, None] * idx[None, :]) & (n_half - 1)).astype(F32) * unit
    ca, sa = jnp.cos(ang_a)[:, None, :], jnp.sin(ang_a)[:, None, :]
    cb, sb = jnp.cos(ang_b)[None], jnp.sin(ang_b)[None]
    ce = (ca * cb - sa * sb).reshape(hh, hh)
    se = (sa * cb + ca * sb).reshape(hh, hh)
    turn = idx.astype(F32) * (0.5 * unit)
    c_row, s_row = jnp.cos(turn)[None, :], jnp.sin(turn)[None, :]
    c_col, s_col = jnp.cos(turn)[:, None], jnp.sin(turn)[:, None]
    flip = (idx[:, None] >= 1) & (idx[None, :] == hh - idx[:, None])
    tabs = dict(
        ce=ce, se=se,
        co=ce * c_row - se * s_row,
        so=se * c_row + ce * s_row,
        cot=ce * c_col - se * s_col, sot=se * c_col + ce * s_col, flip=flip)
    return {name: t.astype(BF16) for name, t in tabs.items()}


def _filter_features(n_lag):
    t = jnp.linspace(0.0, 1.0, n_lag, dtype=F32)[:, None]
    bands = (HY_EMB - 1) // 2
    wpos = 2.0 * math.pi * jnp.arange(n_lag, dtype=F32) / n_lag
    fb = jnp.linspace(1e-4, bands - 1, bands, dtype=F32)
    fw = wpos[:, None] * fb[None, :]
    z = jnp.concatenate([t, jnp.cos(fw), -jnp.sin(fw)], axis=-1)
    return jnp.pad(z, ((0, 0), (0, LANE - HY_EMB)))


def kernel(x, c, ada_mix_w, ada_mix_b, norm_mix_g, ada_mlp_w, ada_mlp_b, norm_mlp_g, w_mlp_in, w_mlp_out, e_w_in, e_q_norm_g, e_kv_norm_g, e_w_uq, e_w_ukv, e_conv_w, e_conv_b, e_f_w1, e_f_b1, e_f_w2, e_f_b2, e_f_w3, e_f_b3, e_f_freq, e_f_w4, e_hy_skip, e_w_out, o_w_qkv, o_sinks, o_w_o, final_norm_g):
    b, s, d = x.shape
    c8 = jnp.pad(c, ((0, 8 - b), (0, 0)))
    mod_mix = _ada(c8, ada_mix_w, ada_mix_b)[:, :b].reshape(DEPTH, b, 1, 3 * d)
    mod_mlp = _ada(c8, ada_mlp_w, ada_mlp_b)[:, :b].reshape(DEPTH, b, 1, 3 * d)

    mla_tabs = _rope_lane_tables(s, MLA_ROPE, rest_passthrough=False)
    gqa_cos, gqa_sin_up, gqa_sin_dn = _rope_lane_tables(s, GQA_ROT, rest_passthrough=True)
    gqa_tabs = (gqa_cos, gqa_sin_up + gqa_sin_dn)
    dft = _dft_tables(s)
    z_feat = _filter_features(s)
    q_scale = (MLA_NOPE + MLA_ROPE) ** -0.5 * math.log2(math.e)
    w_in_t = jnp.swapaxes(e_w_in, 1, 2)
    hy_row0 = Q_LORA + KV_LORA + MLA_ROPE

    w_uq = e_w_uq.reshape(-1, Q_LORA, MLA_HEADS, MLA_NOPE + MLA_ROPE)
    wq_all = jnp.concatenate([
        w_uq[..., :MLA_NOPE].reshape(-1, Q_LORA, MLA_HEADS * MLA_NOPE),
        jnp.pad(w_uq[..., MLA_NOPE:], ((0, 0), (0, 0), (0, 0), (0, LANE - MLA_ROPE))).reshape(
            -1, Q_LORA, MLA_HEADS * LANE),
    ], axis=2).astype(BF16)
    w_ukv = e_w_ukv.reshape(-1, KV_LORA, MLA_HEADS, MLA_NOPE + MLA_V)
    wk_all = w_ukv[..., :MLA_NOPE].reshape(-1, KV_LORA, MLA_HEADS * MLA_NOPE).astype(BF16)
    wv_all = w_ukv[..., MLA_NOPE:].reshape(-1, KV_LORA, MLA_HEADS * MLA_V).astype(BF16)

    for l in range(DEPTH):
        i = l // 2
        if l % 2 == 0:
            h = _normmod(x, mod_mix[l], norm_mix_g[l])
            u = _proj_short_conv(h, w_in_t, i, hy_row0, 3 * HY_D, e_conv_w[i], e_conv_b[i])

            q, k, v = _mla_proj(h, w_in_t, e_q_norm_g[i], e_kv_norm_g[i], wq_all, wk_all, wv_all, i, mla_tabs,
                                q_scale)
            a_mla = _mla_attn(q, k, v)

            w1 = jnp.pad(e_f_w1[i], ((0, LANE - HY_EMB), (0, 0)))
            h_filt = _filter_mlp(z_feat, w1, e_f_b1[i], e_f_w2[i], e_f_b2[i], e_f_w3[i], e_f_b3[i], e_f_freq[i])
            ksum, kdiff, knyq = _filter_gen(h_filt, e_f_w4[i])
            spec = (*_filter_dft(dft, ksum, kdiff), knyq)
            zc = _long_conv(dft, spec, 0, u, 0, u, 1, e_hy_skip[i])
            b_hy = _long_conv(dft, spec, 1, zc, 0, u, 2, e_hy_skip[i])

            x = _proj_res([a_mla, b_hy], e_w_out, i, x, mod_mix[l])
        else:
            h = _normmod(x, mod_mix[l], norm_mix_g[l])
            qkv = _matmul(h, o_w_qkv, i, BF16)
            o = _gqa_attn(qkv, o_sinks[i], gqa_tabs)
            x = _proj_res([o], o_w_o, i, x, mod_mix[l])
        x = _mlp(x, mod_mlp[l], norm_mlp_g[l], w_mlp_in, w_mlp_out, l, final_norm_g, l == DEPTH - 1)
    return x
```

```python
import functools
import math

import jax
import jax.numpy as jnp
from jax import lax
from jax.experimental import pallas as pl
from jax.experimental.pallas import tpu as pltpu

F32 = jnp.float32
BF16 = jnp.bfloat16
HIGHEST = lax.Precision.HIGHEST

D_MODEL = 2048
SEQ = 2048
DEPTH = 4
RMS_EPS = 1e-6
ROPE_THETA = 500000.0
NEG_INF = -1e30
MLA_HEADS = 8
MLA_NOPE = 128
MLA_ROPE = 64
MLA_V = 128
Q_LORA = 512
KV_LORA = 256
HY_D = 1024
HY_ORDER = 2
HY_SHORT = 3
HY_EMB = 33
HY_FFN = 64
HY_DECAY_PCT_SHORT = 0.3
HY_DECAY_PCT_LONG = 1.5
HY_TARGET = 1e-2
GQA_HEADS = 16
GQA_KV_HEADS = 4
GQA_HEAD_DIM = 128
GQA_ROT = GQA_HEAD_DIM // 4
GQA_GROUP = GQA_HEADS // GQA_KV_HEADS
WINDOW = 128
BAND = 128
LAT_W = Q_LORA + KV_LORA + 128
MLA_QK = 256
DFT_N = 2 * SEQ
CONV_SUB_ROWS = 256

LANE = 128
V7X_VMEM_BYTES = 64 * 1024 * 1024
V7X_VMEM_BUDGET = V7X_VMEM_BYTES * 7 // 8
ROW_CHUNK = 128


def _cparams(semantics, est_bytes):
    limit = int(min(V7X_VMEM_BUDGET, max(V7X_VMEM_BYTES // 2, est_bytes * 3 // 2)))
    return pltpu.CompilerParams(dimension_semantics=semantics, vmem_limit_bytes=limit)


def _nbytes(shape, dtype):
    return math.prod(shape) * jnp.dtype(dtype).itemsize


def _rms(x, g):
    ms = jnp.mean(x * x, axis=-1, keepdims=True)
    return x * lax.rsqrt(ms + RMS_EPS) * g


def _ada_kernel(c_ref, w_ref, b_ref, o_ref):
    cv = c_ref[...]
    s = cv * (1.0 / (1.0 + jnp.exp(-cv)))
    o_ref[0] = jnp.dot(s.astype(BF16), w_ref[0].astype(BF16), preferred_element_type=F32) + b_ref[0]


def _ada(c8, w, b, tn=1536):
    n_l, d, n = w.shape
    est = 2 * _nbytes((d, tn), F32) + _nbytes((d, tn), BF16) + 4 * _nbytes((8, tn), F32) + 2 * _nbytes((8, d), F32)
    return pl.pallas_call(
        _ada_kernel,
        name="ada_modulation",
        grid=(n_l, n // tn),
        in_specs=[
            pl.BlockSpec((8, d), lambda l, j: (0, 0)),
            pl.BlockSpec((1, d, tn), lambda l, j: (l, 0, j)),
            pl.BlockSpec((1, 1, tn), lambda l, j: (l, 0, j)),
        ],
        out_specs=pl.BlockSpec((1, 8, tn), lambda l, j: (l, 0, j)),
        out_shape=jax.ShapeDtypeStruct((n_l, 8, n), F32),
        compiler_params=_cparams(("parallel", "parallel"), est),
    )(c8, w, b.reshape(n_l, 1, n))


def _normmod_rows(x_ref, shift_ref, scale_ref, g_ref, h_ref, rows):
    gain = g_ref[...] * (1.0 + scale_ref[0])
    sh = shift_ref[0]
    tm = h_ref.shape[0]

    def body(r, carry):
        sl = pl.ds(pl.multiple_of(r * rows, rows), rows)
        xv = x_ref[0, sl, :]
        inv = lax.rsqrt(jnp.mean(xv * xv, axis=-1, keepdims=True) + RMS_EPS)
        h_ref[sl, :] = (xv * inv * gain + sh).astype(BF16)
        return carry

    lax.fori_loop(0, tm // rows, body, 0, unroll=2)


def _normmod_kernel(x_ref, shift_ref, scale_ref, g_ref, h_ref):
    _normmod_rows(x_ref, shift_ref, scale_ref, g_ref, h_ref.at[0], ROW_CHUNK)


def _normmod(x, mod, g, tm=1024):
    b, s, d = x.shape
    est = 2 * _nbytes((tm, d), F32) + 2 * _nbytes((tm, d), BF16) + 8 * _nbytes((128, d), F32)
    return pl.pallas_call(
        _normmod_kernel,
        name="normmod",
        grid=(b, s // tm),
        in_specs=[
            pl.BlockSpec((1, tm, d), lambda bi, i: (bi, i, 0)),
            pl.BlockSpec((1, 1, d), lambda bi, i: (bi, 0, 0)),
            pl.BlockSpec((1, 1, d), lambda bi, i: (bi, 0, 1)),
            pl.BlockSpec((1, d), lambda bi, i: (0, 0)),
        ],
        out_specs=pl.BlockSpec((1, tm, d), lambda bi, i: (bi, i, 0)),
        out_shape=jax.ShapeDtypeStruct((b, s, d), BF16),
        compiler_params=_cparams(("parallel", "parallel"), est),
    )(x, mod, mod, g.reshape(1, d))


_NT_DIMS = (((1,), (1,)), ((), ()))


def _matmul_kernel(h_ref, w_ref, o_ref):
    o_ref[0] = jnp.dot(h_ref[0], w_ref[...].astype(BF16), preferred_element_type=F32).astype(o_ref.dtype)


def _matmul(h, w_all, layer, out_dtype, tm=2048, tn=1024):
    b, s, d = h.shape
    n = w_all.shape[2]
    est = (2 * _nbytes((tm, d), BF16) + 2 * _nbytes((d, tn), F32) + _nbytes((d, tn), BF16)
           + 3 * _nbytes((tm, tn), F32))
    return pl.pallas_call(
        _matmul_kernel,
        name="matmul",
        grid=(b, s // tm, n // tn),
        in_specs=[
            pl.BlockSpec((1, tm, d), lambda bi, i, j: (bi, i, 0)),
            pl.BlockSpec((None, d, tn), lambda bi, i, j: (layer, 0, j)),
        ],
        out_specs=pl.BlockSpec((1, tm, tn), lambda bi, i, j: (bi, i, j)),
        out_shape=jax.ShapeDtypeStruct((b, s, n), out_dtype),
        compiler_params=_cparams(("parallel", "parallel", "parallel"), est),
    )(h, w_all)


def _mlp_kernel(x_ref, shift_ref, scale_ref, gate_ref, g_ref, w1_ref, w2_ref, fg_ref, o_ref, h_ref, *, final_norm):
    f = pl.program_id(2)

    @pl.when(f == 0)
    def _():
        _normmod_rows(x_ref, shift_ref, scale_ref, g_ref, h_ref, ROW_CHUNK)

        o_ref[...] = jnp.zeros_like(o_ref)

    a = jnp.dot(h_ref[...], w1_ref[...].astype(BF16), preferred_element_type=F32)
    a = jnp.square(jnp.maximum(a, 0.0)).astype(BF16)
    o_ref[0] += jnp.dot(a, w2_ref[...].astype(BF16), preferred_element_type=F32)

    @pl.when(f == pl.num_programs(2) - 1)
    def _():
        gate = gate_ref[0]
        fg = fg_ref[...]
        rows = ROW_CHUNK

        def body(r, carry):
            sl = pl.ds(pl.multiple_of(r * rows, rows), rows)
            y = x_ref[0, sl, :] + gate * o_ref[0, sl, :]
            if final_norm:
                y = _rms(y, fg)
            o_ref[0, sl, :] = y
            return carry

        lax.fori_loop(0, o_ref.shape[1] // rows, body, 0)


def _mlp(x, mod, g, w1_all, w2_all, layer, final_g, final_norm, tm=1024, tf=512):
    b, s, d = x.shape
    ff = w1_all.shape[2]
    est = (4 * _nbytes((tm, d), F32) + _nbytes((tm, d), BF16) + 4 * _nbytes((d, tf), F32)
           + 2 * _nbytes((d, tf), BF16) + 2 * _nbytes((tm, tf), F32))
    return pl.pallas_call(
        functools.partial(_mlp_kernel, final_norm=final_norm),
        name="mlp_relu2",
        grid=(b, s // tm, ff // tf),
        in_specs=[
            pl.BlockSpec((1, tm, d), lambda bi, i, f: (bi, i, 0)),
            pl.BlockSpec((1, 1, d), lambda bi, i, f: (bi, 0, 0)),
            pl.BlockSpec((1, 1, d), lambda bi, i, f: (bi, 0, 1)),
            pl.BlockSpec((1, 1, d), lambda bi, i, f: (bi, 0, 2)),
            pl.BlockSpec((1, d), lambda bi, i, f: (0, 0)),
            pl.BlockSpec((None, d, tf), lambda bi, i, f: (layer, 0, f)),
            pl.BlockSpec((None, tf, d), lambda bi, i, f: (layer, f, 0)),
            pl.BlockSpec((1, d), lambda bi, i, f: (0, 0)),
        ],
        out_specs=pl.BlockSpec((1, tm, d), lambda bi, i, f: (bi, i, 0)),
        out_shape=jax.ShapeDtypeStruct((b, s, d), F32),
        scratch_shapes=[pltpu.VMEM((tm, d), BF16)],
        compiler_params=_cparams(("parallel", "parallel", "arbitrary"), est),
    )(x, mod, mod, mod, g.reshape(1, d), w1_all, w2_all, final_g.reshape(1, d))


def _proj_res_kernel(*refs, n_in):
    a_refs = refs[:n_in]
    w_ref, x_ref, gate_ref, o_ref, wb_scr = refs[n_in:]
    rows = 2 * ROW_CHUNK

    @pl.when((pl.program_id(0) == 0) & (pl.program_id(1) == 0))
    def _():
        def body(r, carry):
            sl = pl.ds(pl.multiple_of(r * rows, rows), rows)
            wb_scr[sl, :] = w_ref[sl, :].astype(BF16)
            return carry

        lax.fori_loop(0, w_ref.shape[0] // rows, body, 0)

    kk = a_refs[0].shape[-1]
    acc = None
    for r, a_ref in enumerate(a_refs):
        p = jnp.dot(a_ref[0], wb_scr[r * kk:(r + 1) * kk, :], preferred_element_type=F32)
        acc = p if acc is None else acc + p
    o_ref[0] = x_ref[0] + gate_ref[0] * acc


def _proj_res(a_list, w_all, layer, x, mod, tm=512):
    b, s, d = x.shape
    n_in = len(a_list)
    kk = a_list[0].shape[-1]
    k_all = n_in * kk
    assert all(a.shape[-1] == kk for a in a_list) and w_all.shape[1] == k_all
    est = (_nbytes((k_all, d), F32) + _nbytes((k_all, d), BF16) + 2 * _nbytes((tm, k_all), BF16)
           + 6 * _nbytes((tm, d), F32))
    in_specs = [pl.BlockSpec((1, tm, kk), lambda bi, i: (bi, i, 0)) for _ in a_list]
    in_specs += [
        pl.BlockSpec((None, k_all, d), lambda bi, i: (layer, 0, 0), pipeline_mode=pl.Buffered(1)),
        pl.BlockSpec((1, tm, d), lambda bi, i: (bi, i, 0)),
        pl.BlockSpec((1, 1, d), lambda bi, i: (bi, 0, 2)),
    ]
    return pl.pallas_call(
        functools.partial(_proj_res_kernel, n_in=n_in),
        name="proj_residual",
        grid=(b, s // tm),
        in_specs=in_specs,
        out_specs=pl.BlockSpec((1, tm, d), lambda bi, i: (bi, i, 0)),
        out_shape=jax.ShapeDtypeStruct((b, s, d), F32),
        scratch_shapes=[pltpu.VMEM((k_all, d), BF16)],
        compiler_params=_cparams(("arbitrary", "arbitrary"), est),
    )(*a_list, w_all, x, mod)


def _rot_lanes(blk, cos_t, sin_up, sin_dn, half):
    return (blk * cos_t + pltpu.roll(blk, half, 1) * sin_up + pltpu.roll(blk, LANE - half, 1) * sin_dn)


def _mla_proj_kernel(h_ref, wlat_ref, gq_ref, gkv_ref, wq_ref, wk_ref, wv_ref, cos_ref, sup_ref, sdn_ref,
                     q_ref, k_ref, v_ref, *, q_scale):
    lat = lax.dot_general(h_ref[0], wlat_ref[...].astype(BF16), _NT_DIMS, preferred_element_type=F32)
    qn = _rms(lat[:, :Q_LORA], gq_ref[...]).astype(BF16)
    kvn = _rms(lat[:, Q_LORA:Q_LORA + KV_LORA], gkv_ref[...]).astype(BF16)
    cos_t, sin_up, sin_dn = cos_ref[...], sup_ref[...], sdn_ref[...]
    half = MLA_ROPE // 2
    q = jnp.dot(qn, wq_ref[...], preferred_element_type=F32) * q_scale
    kn = jnp.dot(kvn, wk_ref[...], preferred_element_type=F32)
    v = jnp.dot(kvn, wv_ref[...], preferred_element_type=F32).astype(BF16)
    ones = jnp.ones((v.shape[0], MLA_V), BF16)
    kr = _rot_lanes(lat[:, Q_LORA + KV_LORA:], cos_t, sin_up, sin_dn, half).astype(BF16)
    nope_w = MLA_HEADS * MLA_NOPE
    for h in range(MLA_HEADS):
        c0 = h * MLA_QK
        v_ref[0, :, 2 * h * MLA_V:(2 * h + 1) * MLA_V] = v[:, h * MLA_V:(h + 1) * MLA_V]
        v_ref[0, :, (2 * h + 1) * MLA_V:(2 * h + 2) * MLA_V] = ones
        q_ref[0, :, c0:c0 + LANE] = q[:, h * LANE:(h + 1) * LANE].astype(BF16)
        qr = q[:, nope_w + h * LANE:nope_w + (h + 1) * LANE]
        q_ref[0, :, c0 + LANE:c0 + 2 * LANE] = _rot_lanes(qr, cos_t, sin_up, sin_dn, half).astype(BF16)
        k_ref[0, :, c0:c0 + LANE] = kn[:, h * LANE:(h + 1) * LANE].astype(BF16)
        k_ref[0, :, c0 + LANE:c0 + 2 * LANE] = kr


def _mla_proj(h, w_in_t, gq, gkv, wq_all, wk_all, wv_all, layer, tabs, q_scale, tm=512):
    b, s, d = h.shape
    hq = MLA_HEADS * MLA_QK
    hv = MLA_HEADS * 2 * MLA_V
    est = (2 * _nbytes((tm, d), BF16) + _nbytes((LAT_W, d), F32) + _nbytes((LAT_W, d), BF16)
           + 2 * _nbytes((tm, LAT_W), F32) + 2 * _nbytes(wq_all.shape[1:], BF16) + 2 * _nbytes(wk_all.shape[1:], BF16)
           + 2 * _nbytes(wv_all.shape[1:], BF16) + 4 * _nbytes((tm, hq), BF16) + 2 * _nbytes((tm, hv), BF16)
           + 3 * _nbytes((tm, hq), F32))
    full = lambda shape: pl.BlockSpec(shape, lambda bi, i: (0,) * len(shape))
    stacked = lambda w: pl.BlockSpec((None,) + w.shape[1:], lambda bi, i: (layer, 0, 0))
    tab = pl.BlockSpec((tm, LANE), lambda bi, i: (i, 0))
    return pl.pallas_call(
        functools.partial(_mla_proj_kernel, q_scale=q_scale),
        name="mla_proj",
        grid=(b, s // tm),
        in_specs=[
            pl.BlockSpec((1, tm, d), lambda bi, i: (bi, i, 0)),
            pl.BlockSpec((None, LAT_W, d), lambda bi, i: (layer, 0, 0), pipeline_mode=pl.Buffered(1)),
            full((1, Q_LORA)), full((1, KV_LORA)), stacked(wq_all), stacked(wk_all), stacked(wv_all),
            tab, tab, tab,
        ],
        out_specs=[
            pl.BlockSpec((1, tm, hq), lambda bi, i: (bi, i, 0)),
            pl.BlockSpec((1, tm, hq), lambda bi, i: (bi, i, 0)),
            pl.BlockSpec((1, tm, hv), lambda bi, i: (bi, i, 0)),
        ],
        out_shape=[
            jax.ShapeDtypeStruct((b, s, hq), BF16),
            jax.ShapeDtypeStruct((b, s, hq), BF16),
            jax.ShapeDtypeStruct((b, s, hv), BF16),
        ],
        compiler_params=_cparams(("parallel", "parallel"), est),
    )(h, w_in_t, gq.reshape(1, -1), gkv.reshape(1, -1), wq_all, wk_all, wv_all, *tabs)


def _mla_attn_kernel(q_ref, k_ref, v_ref, o_ref, *, chain_rows):
    n_chains = q_ref.shape[1] // chain_rows

    def scores(c):
        rs = slice(c * chain_rows, (c + 1) * chain_rows)
        return lax.dot_general(q_ref[0, rs, :], k_ref[0], (((1,), (1,)), ((), ())), preferred_element_type=F32)

    s_next = scores(0)
    for c in range(n_chains):
        s = s_next
        if c + 1 < n_chains:
            s_next = scores(c + 1)
        m = jnp.max(s, axis=-1, keepdims=True)
        p = jnp.exp2(s - m).astype(BF16)
        oe = jnp.dot(p, v_ref[0], preferred_element_type=F32)
        o_ref[0, c * chain_rows:(c + 1) * chain_rows, :] = (oe[:, :MLA_V] / oe[:, MLA_V:]).astype(BF16)


def _mla_attn(q, k, v, tq=2048, chain_rows=1024):
    b, s, _ = q.shape
    est = (2 * _nbytes((tq, MLA_QK), BF16) + 2 * _nbytes((s, MLA_QK), BF16) + 2 * _nbytes((s, 2 * MLA_V), BF16)
           + 2 * _nbytes((tq, MLA_V), BF16) + 5 * _nbytes((chain_rows, s), F32))
    return pl.pallas_call(
        functools.partial(_mla_attn_kernel, chain_rows=chain_rows),
        name="mla_attn",
        grid=(b, MLA_HEADS, s // tq),
        in_specs=[
            pl.BlockSpec((1, tq, MLA_QK), lambda bi, h, i: (bi, i, h)),
            pl.BlockSpec((1, s, MLA_QK), lambda bi, h, i: (bi, 0, h)),
            pl.BlockSpec((1, s, 2 * MLA_V), lambda bi, h, i: (bi, 0, h)),
        ],
        out_specs=pl.BlockSpec((1, tq, MLA_V), lambda bi, h, i: (bi, i, h)),
        out_shape=jax.ShapeDtypeStruct((b, s, MLA_HEADS * MLA_V), BF16),
        compiler_params=_cparams(("parallel", "parallel", "parallel"), est),
    )(q, k, v)


def _gqa_kernel(q_ref, k_ref, v_ref, sink_ref, cos_ref, sin_ref, swap_ref, o_ref, k_scr, v_scr, *, scale):
    s_len = k_scr.shape[0]
    n_blocks = s_len // BAND
    rows = GQA_GROUP * BAND
    swap = swap_ref[...]

    def rot(x, cos_t, sin_t):
        return x.astype(F32) * cos_t + jnp.dot(x, swap, preferred_element_type=F32) * sin_t

    k_scr[...] = rot(k_ref[0], cos_ref[...], sin_ref[...]).astype(BF16)
    v_scr[:, :LANE] = v_ref[0]
    v_scr[:, LANE:] = jnp.ones((s_len, LANE), BF16)
    sink = sink_ref[0]
    rel = (lax.broadcasted_iota(jnp.int32, (rows, 3 * BAND), 1) - BAND
           - (lax.broadcasted_iota(jnp.int32, (rows, 3 * BAND), 0) & (BAND - 1)))
    bias = jnp.where(jnp.abs(rel) <= WINDOW, 0.0, NEG_INF)

    def window(n):
        r0 = n * BAND
        return max(0, r0 - BAND), min(s_len, r0 + 2 * BAND)

    def scores(n):
        r0 = n * BAND
        cos_t = jnp.concatenate([cos_ref[r0:r0 + BAND, :]] * GQA_GROUP, axis=0)
        sin_t = jnp.concatenate([sin_ref[r0:r0 + BAND, :]] * GQA_GROUP, axis=0)
        q_st = jnp.concatenate([q_ref[0, r0:r0 + BAND, g * LANE:(g + 1) * LANE] for g in range(GQA_GROUP)], axis=0)
        q_st = (rot(q_st, cos_t, sin_t) * scale).astype(BF16)
        lo, hi = window(n)
        c0 = lo - (r0 - BAND)
        s = lax.dot_general(q_st, k_scr[lo:hi, :], (((1,), (1,)), ((), ())), preferred_element_type=F32)
        return s + bias[:, c0:c0 + hi - lo]

    s_next = scores(0)
    for n in range(n_blocks):
        s = s_next
        if n + 1 < n_blocks:
            s_next = scores(n + 1)
        r0 = n * BAND
        lo, hi = window(n)
        m = jnp.maximum(jnp.max(s, axis=-1, keepdims=True), sink)
        p = jnp.exp(s - m).astype(BF16)
        oe = jnp.dot(p, v_scr[lo:hi, :], preferred_element_type=F32)
        o = oe[:, :LANE] / (oe[:, LANE:] + jnp.exp(sink - m))
        for g in range(GQA_GROUP):
            o_ref[0, r0:r0 + BAND, g * LANE:(g + 1) * LANE] = o[g * BAND:(g + 1) * BAND, :].astype(BF16)


def _gqa_attn(qkv, sinks, tabs):
    b, s, _ = qkv.shape
    gw = GQA_GROUP * GQA_HEAD_DIM
    sink_col = jnp.broadcast_to(sinks.astype(F32).reshape(GQA_KV_HEADS, GQA_GROUP, 1, 1),
                                (GQA_KV_HEADS, GQA_GROUP, BAND, 1)).reshape(GQA_KV_HEADS, GQA_GROUP * BAND, 1)
    est = (4 * _nbytes((s, gw), BF16) + 7 * _nbytes((s, LANE), BF16) + 6 * _nbytes((s, LANE), F32)
           + 8 * _nbytes((GQA_GROUP * BAND, 3 * BAND), F32))
    tab = pl.BlockSpec((s, LANE), lambda bi, h: (0, 0))
    cos_t, sin_t = tabs
    half = GQA_ROT // 2
    lane = jnp.arange(LANE)
    swap = (((lane[None, :] < half) & (lane[:, None] == lane[None, :] + half))
            | ((lane[None, :] >= half) & (lane[None, :] < 2 * half) & (lane[:, None] == lane[None, :] - half)))
    return pl.pallas_call(
        functools.partial(_gqa_kernel, scale=GQA_HEAD_DIM ** -0.5),
        name="gqa_window_attn",
        grid=(b, GQA_KV_HEADS),
        in_specs=[
            pl.BlockSpec((1, s, gw), lambda bi, h: (bi, 0, h)),
            pl.BlockSpec((1, s, LANE), lambda bi, h: (bi, 0, GQA_HEADS + h)),
            pl.BlockSpec((1, s, LANE), lambda bi, h: (bi, 0, GQA_HEADS + GQA_KV_HEADS + h)),
            pl.BlockSpec((1, GQA_GROUP * BAND, 1), lambda bi, h: (h, 0, 0)),
            tab, tab,
            pl.BlockSpec((LANE, LANE), lambda bi, h: (0, 0)),
        ],
        out_specs=pl.BlockSpec((1, s, gw), lambda bi, h: (bi, 0, h)),
        out_shape=jax.ShapeDtypeStruct((b, s, GQA_HEADS * GQA_HEAD_DIM), BF16),
        scratch_shapes=[pltpu.VMEM((s, LANE), BF16), pltpu.VMEM((s, 2 * LANE), BF16)],
        compiler_params=_cparams(("parallel", "parallel"), est),
    )(qkv, qkv, qkv, sink_col, cos_t, sin_t, swap.astype(BF16))


def _proj_short_conv_kernel(h_ref, w_ref, cw_ref, cb_ref, o_ref, *, sub_cols):
    tn = o_ref.shape[-1]
    h = h_ref[0]

    def project(c0):
        return lax.dot_general(h, w_ref[0, c0:c0 + sub_cols, :].astype(BF16), _NT_DIMS, preferred_element_type=F32)

    x_next = project(0)
    for c0 in range(0, tn, sub_cols):
        x = x_next
        if c0 + sub_cols < tn:
            x_next = project(c0 + sub_cols)
        cols = slice(c0, c0 + sub_cols)
        s_len = x.shape[0]
        row = lax.broadcasted_iota(jnp.int32, x.shape, 0)
        prev = jnp.where(row == 0, 0.0, pltpu.roll(x, 1, 0))
        nxt = jnp.where(row == s_len - 1, 0.0, pltpu.roll(x, s_len - 1, 0))
        y = cb_ref[:, cols] + prev * cw_ref[0:1, cols]
        y = y + x * cw_ref[1:2, cols]
        y = y + nxt * cw_ref[2:3, cols]
        o_ref[0, :, cols] = y.astype(BF16)


def _proj_short_conv(h, wt_all, layer, row0, n, cw, cb, tn=1024, sub_cols=512):
    b, s, d = h.shape
    assert row0 % 8 == 0 and n % tn == 0
    est = (2 * _nbytes((s, d), BF16) + 2 * _nbytes((d, tn), F32) + 2 * _nbytes((d, sub_cols), BF16)
           + 2 * _nbytes((s, tn), BF16) + 8 * _nbytes((s, sub_cols), F32))
    return pl.pallas_call(
        functools.partial(_proj_short_conv_kernel, sub_cols=sub_cols),
        name="hyena_proj_short_conv",
        grid=(b, n // tn),
        in_specs=[
            pl.BlockSpec((1, s, d), lambda bi, j: (bi, 0, 0)),
            pl.BlockSpec((pl.Element(1), pl.Element(tn), pl.Element(d)),
                         lambda bi, j: (layer, pl.multiple_of(row0 + j * tn, 8), 0)),
            pl.BlockSpec((HY_SHORT, tn), lambda bi, j: (0, j)),
            pl.BlockSpec((1, tn), lambda bi, j: (0, j)),
        ],
        out_specs=pl.BlockSpec((1, s, tn), lambda bi, j: (bi, 0, j)),
        out_shape=jax.ShapeDtypeStruct((b, s, n), BF16),
        compiler_params=_cparams(("parallel", "parallel"), est),
    )(h, wt_all, cw, cb.reshape(1, n))


def _filter_mlp_kernel(z_ref, w1_ref, b1_ref, w2_ref, b2_ref, w3_ref, b3_ref, fr_ref, h_ref):
    dot = functools.partial(jnp.dot, preferred_element_type=F32, precision=HIGHEST)
    fr = fr_ref[...]
    h = jnp.sin(fr * (dot(z_ref[...], w1_ref[...]) + b1_ref[...]))
    h = jnp.sin(fr * (dot(h, w2_ref[...]) + b2_ref[...]))
    h_ref[...] = jnp.sin(fr * (dot(h, w3_ref[...]) + b3_ref[...]))


def _filter_mlp(z, w1, b1, w2, b2, w3, b3, freq):
    n_lag = z.shape[0]
    full = lambda shape: pl.BlockSpec(shape, lambda i: (0,) * len(shape))
    est = 12 * _nbytes((n_lag, LANE), F32)
    return pl.pallas_call(
        _filter_mlp_kernel,
        name="hyena_filter_mlp",
        grid=(1,),
        in_specs=[full(z.shape), full(w1.shape), full((1, HY_FFN)), full(w2.shape), full((1, HY_FFN)),
                  full(w3.shape), full((1, HY_FFN)), full((1, HY_FFN))],
        out_specs=full((n_lag, HY_FFN)),
        out_shape=jax.ShapeDtypeStruct((n_lag, HY_FFN), F32),
        compiler_params=_cparams(("arbitrary",), est),
    )(z, w1, b1.reshape(1, -1), w2, b2.reshape(1, -1), w3, b3.reshape(1, -1), freq.reshape(1, -1))


def _filter_gen_kernel(h_ref, w4f_ref, w4b_ref, ksum_ref, kdiff_ref, nyq_ref):
    h = h_ref[...].astype(BF16)
    hf = jnp.dot(h, w4f_ref[...].astype(BF16), preferred_element_type=F32)
    hb = jnp.dot(h, w4b_ref[...].astype(BF16), preferred_element_type=F32)
    n_lag, tn = hf.shape
    row = lax.broadcasted_iota(jnp.int32, (n_lag, tn), 0)
    col = pl.program_id(1) * tn + lax.broadcasted_iota(jnp.int32, (1, tn), 1)
    t = row.astype(F32) / (n_lag - 1)
    max_decay = math.log(HY_TARGET) / HY_DECAY_PCT_SHORT
    min_decay = math.log(HY_TARGET) / HY_DECAY_PCT_LONG
    delta = min_decay + (max_decay - min_decay) * (col.astype(F32) / (HY_D - 1))
    decay = jnp.exp(-t * jnp.abs(delta))
    kf = hf * decay
    kb = jnp.where(row == 0, 0.0, hb * decay)
    inv = 1.0 / (jnp.sum(jnp.abs(kf), axis=0, keepdims=True) + jnp.sum(jnp.abs(kb), axis=0, keepdims=True))
    ksum = (kf + kb) * inv
    ksum_ref[0] = ksum.astype(BF16)
    kdiff_ref[0] = ((kb - kf) * inv).astype(BF16)
    alt = (1 - 2 * (row & 1)).astype(F32)
    nyq_ref[0] = jnp.sum(ksum * alt, axis=0, keepdims=True) * (1.0 / DFT_N)


def _filter_gen(h, w4, tn=256):
    n_lag = h.shape[0]
    nblk = HY_D // tn
    est = 10 * _nbytes((n_lag, tn), F32) + 4 * _nbytes((n_lag, LANE), F32) + 4 * _nbytes((n_lag, tn), BF16)
    out = jax.ShapeDtypeStruct((HY_ORDER, n_lag, HY_D), BF16)
    return pl.pallas_call(
        _filter_gen_kernel,
        name="hyena_filter_gen",
        grid=(HY_ORDER, nblk),
        in_specs=[
            pl.BlockSpec((n_lag, HY_FFN), lambda o, j: (0, 0)),
            pl.BlockSpec((HY_FFN, tn), lambda o, j: (0, 2 * o * nblk + j)),
            pl.BlockSpec((HY_FFN, tn), lambda o, j: (0, (2 * o + 1) * nblk + j)),
        ],
        out_specs=[pl.BlockSpec((1, n_lag, tn), lambda o, j: (o, 0, j))] * 2
        + [pl.BlockSpec((1, 1, tn), lambda o, j: (o, 0, j))],
        out_shape=[out, out, jax.ShapeDtypeStruct((HY_ORDER, 1, HY_D), F32)],
        compiler_params=_cparams(("parallel", "parallel"), est),
    )(h, w4, w4)


def _filter_dft_kernel(ce_ref, se_ref, co_ref, so_ref, flip_ref, ksum_ref, kdiff_ref, a_ref, bq_ref):
    hh = ce_ref.shape[0]
    dot = functools.partial(jnp.dot, preferred_element_type=F32)
    idx = lax.broadcasted_iota(jnp.int32, (hh, 1), 0)
    alt = _alt_sign(idx)

    def fold(x_ref):
        lo = x_ref[0, :hh, :].astype(F32)
        rev = dot(flip_ref[...], x_ref[0, hh:, :])
        mid = x_ref[0, hh:hh + 16, :].astype(F32)[0:1]
        return (lo + rev).astype(BF16), (lo - rev).astype(BF16), mid

    e_sum, o_sum, mid_sum = fold(ksum_ref)
    e_diff, o_diff, mid_diff = fold(kdiff_ref)
    wgt = 2.0 / DFT_N
    a_even = (dot(ce_ref[...], e_sum) + alt * mid_sum) * jnp.where(idx == 0, 1.0 / DFT_N, wgt)
    a_ref[0, :hh, :] = a_even.astype(BF16)
    a_ref[0, hh:, :] = (dot(co_ref[...], o_sum) * wgt).astype(BF16)
    bq_ref[0, :hh, :] = jnp.where(idx == 0, 0.0, dot(se_ref[...], o_diff) * wgt).astype(BF16)
    bq_ref[0, hh:, :] = ((dot(so_ref[...], e_diff) + alt * mid_diff) * wgt).astype(BF16)


def _filter_dft(tabs, ksum, kdiff, tn=512):
    _, s, _ = ksum.shape
    hh = s // 2
    est = (5 * _nbytes((hh, hh), BF16) + 8 * _nbytes((s, tn), BF16) + 4 * _nbytes((hh, tn), BF16)
           + 10 * _nbytes((hh, tn), F32))
    tab = pl.BlockSpec((hh, hh), lambda o, j: (0, 0), pipeline_mode=pl.Buffered(1))
    blk = pl.BlockSpec((1, s, tn), lambda o, j: (o, 0, j))
    out = jax.ShapeDtypeStruct((HY_ORDER, s, HY_D), BF16)
    return pl.pallas_call(
        _filter_dft_kernel,
        name="hyena_filter_dft",
        grid=(HY_ORDER, HY_D // tn),
        in_specs=[tab, tab, tab, tab, tab, blk, blk],
        out_specs=[blk, blk],
        out_shape=[out, out],
        compiler_params=_cparams(("parallel", "parallel"), est),
    )(tabs["ce"], tabs["se"], tabs["co"], tabs["so"], tabs["flip"], ksum, kdiff)


def _alt_sign(idx):
    return (1 - 2 * (idx & 1)).astype(F32)


def _long_conv_kernel(ce_ref, se_ref, co_ref, so_ref, cot_ref, sot_ref, flip_ref, u_ref, g_ref,
                      ae_ref, bqe_ref, ao_ref, bqo_ref, knyq_ref, skip_ref, o_ref,
                      e_scr, o_scr, yre_scr, yse_scr, yro_scr, yso_scr, d_scr):
    hh = e_scr.shape[0]
    sub = CONV_SUB_ROWS
    dot = functools.partial(jnp.dot, preferred_element_type=F32)
    alt_h = _alt_sign(lax.broadcasted_iota(jnp.int32, (hh, 1), 0))

    u_lo = u_ref[0, :hh, :].astype(F32)
    u_rev = dot(flip_ref[...], u_ref[0, hh:, :])
    e32 = u_lo + u_rev
    e_scr[...] = e32.astype(BF16)
    o_scr[...] = (u_lo - u_rev).astype(BF16)
    mid = u_ref[0, hh:hh + 16, :].astype(F32)[0:1]
    nyq_u = jnp.sum(e32 * alt_h, axis=0, keepdims=True) + mid
    e, o = e_scr[...], o_scr[...]

    for r0 in range(0, hh, sub):
        rs = slice(r0, r0 + sub)
        m = r0 + lax.broadcasted_iota(jnp.int32, (sub, 1), 0)
        corr = _alt_sign(m) * mid
        uc = dot(ce_ref[rs, :], e) + corr
        us = dot(se_ref[rs, :], o)
        a, bq = ae_ref[0, rs, :].astype(F32), bqe_ref[0, rs, :].astype(F32)
        yre_scr[rs, :] = (uc * a + us * bq).astype(BF16)
        ys = us * a - uc * bq
        if r0 == 0:
            ys = jnp.where(m == 0, nyq_u * knyq_ref[0], ys)
        yse_scr[rs, :] = ys.astype(BF16)
        uc = dot(co_ref[rs, :], o)
        us = dot(so_ref[rs, :], e) + corr
        a, bq = ao_ref[0, rs, :].astype(F32), bqo_ref[0, rs, :].astype(F32)
        yro_scr[rs, :] = (uc * a + us * bq).astype(BF16)
        yso_scr[rs, :] = (us * a - uc * bq).astype(BF16)

    yre, yse, yro, yso = yre_scr[...], yse_scr[...], yro_scr[...], yso_scr[...]
    nyq = yse_scr[0:16, :].astype(F32)[0:1]
    y_mid = jnp.sum((yre.astype(F32) + yso.astype(F32)) * alt_h, axis=0, keepdims=True) + nyq
    skip = skip_ref[0]
    for r0 in range(0, hh, sub):
        rs = slice(r0, r0 + sub)
        t = r0 + lax.broadcasted_iota(jnp.int32, (sub, 1), 0)
        pa = dot(ce_ref[rs, :], yre) + dot(sot_ref[rs, :], yso) + _alt_sign(t) * nyq
        pb = dot(se_ref[rs, :], yse) + dot(cot_ref[rs, :], yro)
        u_t = u_ref[0, rs, :].astype(F32)
        o_ref[0, rs, :] = (g_ref[0, rs, :].astype(F32) * (pa + pb + u_t * skip)).astype(BF16)
        d_scr[rs, :] = (pa - pb).astype(BF16)
    y_hi = dot(flip_ref[...], d_scr[...])
    y_hi = jnp.where(lax.broadcasted_iota(jnp.int32, (hh, 1), 0) == 0, y_mid, y_hi)
    u_hi = u_ref[0, hh:, :].astype(F32)
    o_ref[0, hh:, :] = (g_ref[0, hh:, :].astype(F32) * (y_hi + u_hi * skip)).astype(BF16)


def _long_conv(tabs, spec, order, usrc, u_blk0, gsrc, g_blk0, skip, tn=512):
    a, bq, knyq = spec
    b, s, _ = usrc.shape
    hh = s // 2
    nblk = HY_D // tn
    est = (7 * _nbytes((hh, hh), BF16) + 10 * _nbytes((s, tn), BF16) + 8 * _nbytes((hh, tn), BF16)
           + 7 * _nbytes((hh, tn), BF16) + 8 * _nbytes((hh, tn), F32) + 12 * _nbytes((CONV_SUB_ROWS, tn), F32))
    tab = pl.BlockSpec((hh, hh), lambda bi, j: (0, 0), pipeline_mode=pl.Buffered(1))
    even = pl.BlockSpec((1, hh, tn), lambda bi, j: (order, 0, j))
    odd = pl.BlockSpec((1, hh, tn), lambda bi, j: (order, 1, j))
    row = pl.BlockSpec((1, 1, tn), lambda bi, j: (order, 0, j))
    return pl.pallas_call(
        _long_conv_kernel,
        name="hyena_long_conv",
        grid=(b, nblk),
        in_specs=[
            tab, tab, tab, tab, tab, tab, tab,
            pl.BlockSpec((1, s, tn), lambda bi, j: (bi, 0, u_blk0 * nblk + j)),
            pl.BlockSpec((1, s, tn), lambda bi, j: (bi, 0, g_blk0 * nblk + j)),
            even, even, odd, odd, row, row,
        ],
        out_specs=pl.BlockSpec((1, s, tn), lambda bi, j: (bi, 0, j)),
        out_shape=jax.ShapeDtypeStruct((b, s, HY_D), BF16),
        scratch_shapes=[pltpu.VMEM((hh, tn), BF16)] * 7,
        compiler_params=_cparams(("parallel", "parallel"), est),
    )(tabs["ce"], tabs["se"], tabs["co"], tabs["so"], tabs["cot"], tabs["sot"], tabs["flip"], usrc, gsrc,
      a, bq, a, bq, knyq, skip.reshape(HY_ORDER, 1, HY_D))


def _rope_tables(seq, rot_dim):
    half = rot_dim // 2
    pos = jnp.arange(seq, dtype=F32)
    inv = ROPE_THETA ** (-jnp.arange(0, rot_dim, 2, dtype=F32) / rot_dim)
    ang = pos[:, None] * inv[None, :]
    cos, sin = jnp.cos(ang), jnp.sin(ang)
    rest = LANE - rot_dim
    return cos, sin, half, rest


def _rope_lane_tables(seq, rot_dim, rest_passthrough):
    cos, sin, half, rest = _rope_tables(seq, rot_dim)
    fill = jnp.ones((seq, rest), F32) if rest_passthrough else jnp.zeros((seq, rest), F32)
    zero_h = jnp.zeros((seq, half), F32)
    zero_r = jnp.zeros((seq, rest), F32)
    cos_t = jnp.concatenate([cos, cos, fill], axis=1)
    sin_up = jnp.concatenate([zero_h, sin, zero_r], axis=1)
    sin_dn = jnp.concatenate([-sin, zero_h, zero_r], axis=1)
    return cos_t, sin_up, sin_dn


def _dft_tables(n_half):
    hh = n_half // 2
    blk = 32
    idx = jnp.arange(hh, dtype=jnp.int32)
    unit = 2.0 * math.pi / n_half
    ang_a = ((blk * idx[:hh // blk, None] * idx[None, :]) & (n_half - 1)).astype(F32) * unit
    ang_b = ((idx[:blk, None] * idx[None, :]) & (n_half - 1)).astype(F32) * unit
    ca, sa = jnp.cos(ang_a)[:, None, :], jnp.sin(ang_a)[:, None, :]
    cb, sb = jnp.cos(ang_b)[None], jnp.sin(ang_b)[None]
    ce = (ca * cb - sa * sb).reshape(hh, hh)
    se = (sa * cb + ca * sb).reshape(hh, hh)
    turn = idx.astype(F32) * (0.5 * unit)
    c_row, s_row = jnp.cos(turn)[None, :], jnp.sin(turn)[None, :]
    c_col, s_col = jnp.cos(turn)[:, None], jnp.sin(turn)[:, None]
    flip = (idx[:, None] >= 1) & (idx[None, :] == hh - idx[:, None])
    tabs = dict(
        ce=ce, se=se,
        co=ce * c_row - se * s_row,
        so=se * c_row + ce * s_row,
        cot=ce * c_col - se * s_col, sot=se * c_col + ce * s_col, flip=flip)
    return {name: t.astype(BF16) for name, t in tabs.items()}


def _filter_features(n_lag):
    t = jnp.linspace(0.0, 1.0, n_lag, dtype=F32)[:, None]
    bands = (HY_EMB - 1) // 2
    wpos = 2.0 * math.pi * jnp.arange(n_lag, dtype=F32) / n_lag
    fb = jnp.linspace(1e-4, bands - 1, bands, dtype=F32)
    fw = wpos[:, None] * fb[None, :]
    z = jnp.concatenate([t, jnp.cos(fw), -jnp.sin(fw)], axis=-1)
    return jnp.pad(z, ((0, 0), (0, LANE - HY_EMB)))


def kernel(x, c, ada_mix_w, ada_mix_b, norm_mix_g, ada_mlp_w, ada_mlp_b, norm_mlp_g, w_mlp_in, w_mlp_out, e_w_in, e_q_norm_g, e_kv_norm_g, e_w_uq, e_w_ukv, e_conv_w, e_conv_b, e_f_w1, e_f_b1, e_f_w2, e_f_b2, e_f_w3, e_f_b3, e_f_freq, e_f_w4, e_hy_skip, e_w_out, o_w_qkv, o_sinks, o_w_o, final_norm_g):
    b, s, d = x.shape
    c8 = jnp.pad(c, ((0, 8 - b), (0, 0)))
    mod_mix = _ada(c8, ada_mix_w, ada_mix_b)[:, :b].reshape(DEPTH, b, 1, 3 * d)
    mod_mlp = _ada(c8, ada_mlp_w, ada_mlp_b)[:, :b].reshape(DEPTH, b, 1, 3 * d)

    mla_tabs = _rope_lane_tables(s, MLA_ROPE, rest_passthrough=False)
    gqa_cos, gqa_sin_up, gqa_sin_dn = _rope_lane_tables(s, GQA_ROT, rest_passthrough=True)
    gqa_tabs = (gqa_cos, gqa_sin_up + gqa_sin_dn)
    dft = _dft_tables(s)
    z_feat = _filter_features(s)
    q_scale = (MLA_NOPE + MLA_ROPE) ** -0.5 * math.log2(math.e)
    w_in_t = jnp.swapaxes(e_w_in, 1, 2)
    hy_row0 = Q_LORA + KV_LORA + MLA_ROPE

    w_uq = e_w_uq.reshape(-1, Q_LORA, MLA_HEADS, MLA_NOPE + MLA_ROPE)
    wq_all = jnp.concatenate([
        w_uq[..., :MLA_NOPE].reshape(-1, Q_LORA, MLA_HEADS * MLA_NOPE),
        jnp.pad(w_uq[..., MLA_NOPE:], ((0, 0), (0, 0), (0, 0), (0, LANE - MLA_ROPE))).reshape(
            -1, Q_LORA, MLA_HEADS * LANE),
    ], axis=2).astype(BF16)
    w_ukv = e_w_ukv.reshape(-1, KV_LORA, MLA_HEADS, MLA_NOPE + MLA_V)
    wk_all = w_ukv[..., :MLA_NOPE].reshape(-1, KV_LORA, MLA_HEADS * MLA_NOPE).astype(BF16)
    wv_all = w_ukv[..., MLA_NOPE:].reshape(-1, KV_LORA, MLA_HEADS * MLA_V).astype(BF16)

    for l in range(DEPTH):
        i = l // 2
        if l % 2 == 0:
            h = _normmod(x, mod_mix[l], norm_mix_g[l])
            u = _proj_short_conv(h, w_in_t, i, hy_row0, 3 * HY_D, e_conv_w[i], e_conv_b[i])

            q, k, v = _mla_proj(h, w_in_t, e_q_norm_g[i], e_kv_norm_g[i], wq_all, wk_all, wv_all, i, mla_tabs,
                                q_scale)
            a_mla = _mla_attn(q, k, v)

            w1 = jnp.pad(e_f_w1[i], ((0, LANE - HY_EMB), (0, 0)))
            h_filt = _filter_mlp(z_feat, w1, e_f_b1[i], e_f_w2[i], e_f_b2[i], e_f_w3[i], e_f_b3[i], e_f_freq[i])
            ksum, kdiff, knyq = _filter_gen(h_filt, e_f_w4[i])
            spec = (*_filter_dft(dft, ksum, kdiff), knyq)
            zc = _long_conv(dft, spec, 0, u, 0, u, 1, e_hy_skip[i])
            b_hy = _long_conv(dft, spec, 1, zc, 0, u, 2, e_hy_skip[i])

            x = _proj_res([a_mla, b_hy], e_w_out, i, x, mod_mix[l])
        else:
            h = _normmod(x, mod_mix[l], norm_mix_g[l])
            qkv = _matmul(h, o_w_qkv, i, BF16)
            o = _gqa_attn(qkv, o_sinks[i], gqa_tabs)
            x = _proj_res([o], o_w_o, i, x, mod_mix[l])
        x = _mlp(x, mod_mlp[l], norm_mlp_g[l], w_mlp_in, w_mlp_out, l, final_norm_g, l == DEPTH - 1)
    return x
```
